```python
import math
import jax, jax.numpy as jnp
from jax import lax
import numpy as np

D_MODEL = 1024
BATCH = 1
SEQ = 16384
DEPTH = 2

N_MIXERS = 2
N_CONV = (DEPTH + 1) // 2
N_ATTN = DEPTH // 2
CONV_WIDTH = 3
HEAD_DIM = 64
N_HEADS = D_MODEL // HEAD_DIM
Q_BLOCK = 128
RMS_EPS = 1e-6

kernel_name = "hybrid_shortconv_forgetting_attention"


def _rmsnorm(x, g):
    xf = x.astype(jnp.float32)
    inv = lax.rsqrt(jnp.mean(xf * xf, axis=-1, keepdims=True) + RMS_EPS)
    return (xf * inv * g.astype(jnp.float32)).astype(x.dtype)


def _short_conv_layer(x, norm_g, w_in, conv_w, w_out):
    h = _rmsnorm(x, norm_g)
    proj = jnp.einsum('bsd,de->bse', h, w_in)
    b_g, c_g, xin, z = jnp.split(proj, 4, axis=-1)
    u = c_g * xin
    y = lax.conv_general_dilated(
        u, conv_w[:, None, :].astype(u.dtype),
        window_strides=(1,), padding=[(CONV_WIDTH - 1, 0)],
        dimension_numbers=('NWC', 'WIO', 'NWC'),
        feature_group_count=D_MODEL)
    y = b_g * y * jax.nn.silu(z)
    return x + jnp.einsum('bse,ed->bsd', y, w_out)


def _forgetting_attention(q, k, v, log_f):
    S = q.shape[2]
    c = jnp.cumsum(log_f, axis=-1)
    scale = 1.0 / math.sqrt(HEAD_DIM)
    kpos = jnp.arange(S)
    neg = jnp.finfo(jnp.float32).min

    def block(i):
        start = i * Q_BLOCK
        qb = lax.dynamic_slice_in_dim(q, start, Q_BLOCK, axis=2)
        cb = lax.dynamic_slice_in_dim(c, start, Q_BLOCK, axis=2)
        s = jnp.einsum('bhqd,bhkd->bhqk', qb, k) * scale
        s = s + cb[..., :, None] - c[..., None, :]
        qpos = start + jnp.arange(Q_BLOCK)
        s = jnp.where(kpos[None, :] <= qpos[:, None], s, neg)
        p = jax.nn.softmax(s, axis=-1)
        return jnp.einsum('bhqk,bhkd->bhqd', p, v)

    out = lax.map(block, jnp.arange(S // Q_BLOCK))
    nb, b, h, qn, dh = out.shape
    return jnp.transpose(out, (1, 2, 0, 3, 4)).reshape(b, h, nb * qn, dh)


def _attn_layer(x, norm_g, w_in, b_f, q_g, k_g, w_out):
    B, S, D = x.shape
    h = _rmsnorm(x, norm_g)
    proj = jnp.einsum('bsd,de->bse', h, w_in)
    q = proj[..., 0 * D:1 * D]
    k = proj[..., 1 * D:2 * D]
    v = proj[..., 2 * D:3 * D]
    z = proj[..., 3 * D:4 * D]
    f_logit = proj[..., 4 * D:].astype(jnp.float32) + b_f.astype(jnp.float32)
    to_heads = lambda t: jnp.transpose(t.reshape(B, S, N_HEADS, HEAD_DIM), (0, 2, 1, 3))
    q = _rmsnorm(to_heads(q), q_g).astype(jnp.float32)
    k = _rmsnorm(to_heads(k), k_g).astype(jnp.float32)
    v = to_heads(v).astype(jnp.float32)
    log_f = jnp.transpose(jax.nn.log_sigmoid(f_logit), (0, 2, 1))
    o = _forgetting_attention(q, k, v, log_f)
    o = jnp.transpose(o, (0, 2, 1, 3)).reshape(B, S, D).astype(x.dtype)
    o = o * jax.nn.silu(z)
    return x + jnp.einsum('bse,ed->bsd', o, w_out)


def setup_inputs(seed: int = 0) -> dict:
    key = jax.random.key(seed)
    ks = jax.random.split(key, 12)
    D = D_MODEL
    s_d = D ** -0.5
    x = jax.random.normal(ks[0], (BATCH, SEQ, D), jnp.float32)
    conv_norm_g = 1.0 + 0.02 * jax.random.normal(ks[1], (N_CONV, D), jnp.float32)
    conv_w_in = jax.random.normal(ks[2], (N_CONV, D, 4 * D), jnp.float32) * s_d
    conv_w = jax.random.normal(ks[3], (N_CONV, CONV_WIDTH, D), jnp.float32) * (CONV_WIDTH ** -0.5)
    conv_w_out = jax.random.normal(ks[4], (N_CONV, D, D), jnp.float32) * s_d
    attn_norm_g = 1.0 + 0.02 * jax.random.normal(ks[5], (N_ATTN, D), jnp.float32)
    attn_w_in = jax.random.normal(ks[6], (N_ATTN, D, 4 * D + N_HEADS), jnp.float32) * s_d
    attn_b_f = jax.random.uniform(ks[7], (N_ATTN, N_HEADS), jnp.float32, 1.0, 4.0)
    attn_q_norm_g = 1.0 + 0.02 * jax.random.normal(ks[8], (N_ATTN, HEAD_DIM), jnp.float32)
    attn_k_norm_g = 1.0 + 0.02 * jax.random.normal(ks[9], (N_ATTN, HEAD_DIM), jnp.float32)
    attn_w_out = jax.random.normal(ks[10], (N_ATTN, D, D), jnp.float32) * s_d
    return {"x": x, "conv_norm_g": conv_norm_g, "conv_w_in": conv_w_in, "conv_w": conv_w,
            "conv_w_out": conv_w_out, "attn_norm_g": attn_norm_g, "attn_w_in": attn_w_in,
            "attn_b_f": attn_b_f, "attn_q_norm_g": attn_q_norm_g, "attn_k_norm_g": attn_k_norm_g,
            "attn_w_out": attn_w_out}


def reference(x, conv_norm_g, conv_w_in, conv_w, conv_w_out, attn_norm_g, attn_w_in,
              attn_b_f, attn_q_norm_g, attn_k_norm_g, attn_w_out):
    for i in range(DEPTH):
        j = i // N_MIXERS
        if i % N_MIXERS == 0:
            x = _short_conv_layer(x, conv_norm_g[j], conv_w_in[j], conv_w[j], conv_w_out[j])
        else:
            x = _attn_layer(x, attn_norm_g[j], attn_w_in[j], attn_b_f[j],
                            attn_q_norm_g[j], attn_k_norm_g[j], attn_w_out[j])
    return x
```

```python
import functools
import math

import jax
import jax.numpy as jnp
from jax import lax
from jax.experimental import pallas as pl
from jax.experimental.pallas import tpu as pltpu

D_MODEL = 1024
HEAD_DIM = 64
N_HEADS = D_MODEL // HEAD_DIM
N_PAIRS = N_HEADS // 2
CONV_WIDTH = 3
RMS_EPS = 1e-6

LANES = 128
SUBLANES = 8
MXU_DIM = 256
VMEM_LIMIT_BYTES = 56 * 1024 * 1024

ROW_TILE = 512
BLOCK_Q = 512
BLOCK_K = 512
NEG_BIG = -1e30

BF16 = jnp.bfloat16
F32 = jnp.float32


def _rmsnorm_rows(x, g):
    inv = lax.rsqrt(jnp.mean(x * x, axis=-1, keepdims=True) + RMS_EPS)
    return x * inv * g


def _silu(z):
    return z * jax.nn.sigmoid(z)


def _conv_layer_kernel(x_ref, g_ref, w_in_ref, cw_ref, w_out_ref, o_ref, tail_ref):
    i = pl.program_id(0)
    D = D_MODEL
    T = x_ref.shape[0]

    @pl.when(i == 0)
    def _():
        tail_ref[...] = jnp.zeros_like(tail_ref)

    x = x_ref[...]
    h = _rmsnorm_rows(x, g_ref[...]).astype(BF16)
    proj = jnp.dot(h, w_in_ref[...], preferred_element_type=F32)
    b_g = proj[:, 0 * D:1 * D]
    c_g = proj[:, 1 * D:2 * D]
    xin = proj[:, 2 * D:3 * D]
    z = proj[:, 3 * D:4 * D]
    u = c_g * xin

    tail = tail_ref[...]
    prev1 = tail[SUBLANES - 1:SUBLANES, :]
    prev2 = tail[SUBLANES - 2:SUBLANES - 1, :]
    row = lax.broadcasted_iota(jnp.int32, (T, 1), 0)
    u1 = jnp.where(row == 0, prev1, pltpu.roll(u, 1, axis=0))
    u2 = jnp.where(row == 0, prev2, jnp.where(row == 1, prev1, pltpu.roll(u, 2, axis=0)))
    tail_ref[...] = u[T - SUBLANES:, :]

    cw = cw_ref[...]
    y = cw[0:1, :] * u2 + cw[1:2, :] * u1 + cw[2:3, :] * u
    y = b_g * y * _silu(z)
    o_ref[...] = x + jnp.dot(y.astype(BF16), w_out_ref[...], preferred_element_type=F32)


def _conv_layer(x, g, w_in, cw, w_out):
    S, D = x.shape
    const = lambda i: (0, 0)
    return pl.pallas_call(
        _conv_layer_kernel,
        grid=(S // ROW_TILE,),
        in_specs=[
            pl.BlockSpec((ROW_TILE, D), lambda i: (i, 0)),
            pl.BlockSpec((1, D), const),
            pl.BlockSpec((D, 4 * D), const),
            pl.BlockSpec((CONV_WIDTH, D), const),
            pl.BlockSpec((D, D), const),
        ],
        out_specs=pl.BlockSpec((ROW_TILE, D), lambda i: (i, 0)),
        out_shape=jax.ShapeDtypeStruct((S, D), F32),
        scratch_shapes=[pltpu.VMEM((SUBLANES, D), F32)],
        compiler_params=pltpu.CompilerParams(
            dimension_semantics=("arbitrary",), vmem_limit_bytes=VMEM_LIMIT_BYTES),
        name="conv_layer",
    )(x, g, w_in, cw, w_out)


def _head_rmsnorm(t, g, gmat):
    sq = t * t
    hi = sq.astype(BF16)
    lo = (sq - hi.astype(F32)).astype(BF16)
    parts = []
    for c in range(D_MODEL // MXU_DIM):
        sl = slice(c * MXU_DIM, (c + 1) * MXU_DIM)
        parts.append(jnp.dot(hi[:, sl], gmat, preferred_element_type=F32)
                     + jnp.dot(lo[:, sl], gmat, preferred_element_type=F32))
    ms = jnp.concatenate(parts, axis=-1)
    return t * lax.rsqrt(ms + RMS_EPS) * g


def _attn_proj_kernel(x_ref, g_ref, w_ref, wf_ref, bf_ref, qg_ref, kg_ref, gmat_ref, tri_ref,
                      q_ref, k_ref, v_ref, z_ref, c_ref, run_ref):
    i = pl.program_id(0)
    D = D_MODEL

    @pl.when(i == 0)
    def _():
        run_ref[...] = jnp.zeros_like(run_ref)

    h = _rmsnorm_rows(x_ref[...], g_ref[...]).astype(BF16)
    proj = jnp.dot(h, w_ref[...], preferred_element_type=F32)
    gmat = gmat_ref[...]
    scale = 1.0 / math.sqrt(HEAD_DIM)
    q_ref[...] = (_head_rmsnorm(proj[:, 0 * D:1 * D], qg_ref[...], gmat) * scale).astype(BF16)
    k_ref[...] = _head_rmsnorm(proj[:, 1 * D:2 * D], kg_ref[...], gmat).astype(BF16)
    v_ref[...] = proj[:, 2 * D:3 * D].astype(BF16)
    z_ref[...] = proj[:, 3 * D:4 * D]

    f = lax.dot_general(wf_ref[...], h, (((1,), (1,)), ((), ())),
                        preferred_element_type=F32) + bf_ref[...]
    logf = -(jnp.maximum(-f, 0.0) + jnp.log1p(jnp.exp(-jnp.abs(f))))
    tri = tri_ref[...]
    p1 = logf.astype(BF16)
    r1 = logf - p1.astype(F32)
    p2 = r1.astype(BF16)
    p3 = (r1 - p2.astype(F32)).astype(BF16)
    cs = (jnp.dot(p1, tri, preferred_element_type=F32)
          + jnp.dot(p2, tri, preferred_element_type=F32)
          + jnp.dot(p3, tri, preferred_element_type=F32))
    c = cs + run_ref[...][:, 0:1]
    c_ref[...] = c
    T = c.shape[1]
    run_ref[...] = jnp.broadcast_to(c[:, T - 1:T], run_ref.shape)


def _attn_proj(x, g, w_qkvz, w_f_t, b_f, q_g, k_g):
    S, D = x.shape
    T = ROW_TILE
    head_of = jnp.arange(MXU_DIM) // HEAD_DIM
    gmat = jnp.where(head_of[:, None] == head_of[None, :], 1.0 / HEAD_DIM, 0.0).astype(BF16)
    tri = (jnp.arange(T)[:, None] <= jnp.arange(T)[None, :]).astype(BF16)
    const = lambda i: (0, 0)
    rows = lambda i: (i, 0)
    return pl.pallas_call(
        _attn_proj_kernel,
        grid=(S // T,),
        in_specs=[
            pl.BlockSpec((T, D), rows),
            pl.BlockSpec((1, D), const),
            pl.BlockSpec((D, 4 * D), const),
            pl.BlockSpec((N_HEADS, D), const),
            pl.BlockSpec((N_HEADS, 1), const),
            pl.BlockSpec((1, D), const),
            pl.BlockSpec((1, D), const),
            pl.BlockSpec((MXU_DIM, MXU_DIM), const),
            pl.BlockSpec((T, T), const),
        ],
        out_specs=[
            pl.BlockSpec((T, D), rows),
            pl.BlockSpec((T, D), rows),
            pl.BlockSpec((T, D), rows),
            pl.BlockSpec((T, D), rows),
            pl.BlockSpec((N_HEADS, T), lambda i: (0, i)),
        ],
        out_shape=[
            jax.ShapeDtypeStruct((S, D), BF16),
            jax.ShapeDtypeStruct((S, D), BF16),
            jax.ShapeDtypeStruct((S, D), BF16),
            jax.ShapeDtypeStruct((S, D), F32),
            jax.ShapeDtypeStruct((N_HEADS, S), F32),
        ],
        scratch_shapes=[pltpu.VMEM((N_HEADS, LANES), F32)],
        compiler_params=pltpu.CompilerParams(
            dimension_semantics=("arbitrary",), vmem_limit_bytes=VMEM_LIMIT_BYTES),
        name="attn_proj",
    )(x, g, w_qkvz, w_f_t, b_f, q_g, k_g, gmat, tri)


def _attn_kernel(q_ref, k_ref, v_ref, c_ref, z_ref, o_ref, m_ref, l_ref, acc_ref):
    i = pl.program_id(1)
    bq, bk = BLOCK_Q, BLOCK_K
    q2 = q_ref[...]
    lane = lax.broadcasted_iota(jnp.int32, (1, LANES), 1)
    zero = jnp.zeros_like(q2)
    q_heads = (jnp.where(lane < HEAD_DIM, q2, zero), jnp.where(lane >= HEAD_DIM, q2, zero))

    m_ref[...] = jnp.full_like(m_ref, NEG_BIG)
    l_ref[...] = jnp.zeros_like(l_ref)
    acc_ref[...] = jnp.zeros_like(acc_ref)

    def step(j, masked):
        start = pl.multiple_of(j * bk, bk)
        kb = k_ref[pl.ds(start, bk), :]
        vb = v_ref[pl.ds(start, bk), :]
        cb = c_ref[0, :, pl.ds(start, bk)]
        for hh in range(2):
            s = lax.dot_general(q_heads[hh], kb, (((1,), (1,)), ((), ())),
                                preferred_element_type=F32)
            s = s - cb[hh:hh + 1, :]
            if masked:
                row = lax.broadcasted_iota(jnp.int32, (bq, bk), 0)
                col = lax.broadcasted_iota(jnp.int32, (bq, bk), 1)
                s = jnp.where(col <= row, s, NEG_BIG)
            m_old = m_ref[hh]
            m_new = jnp.maximum(m_old, jnp.max(s, axis=-1, keepdims=True))
            alpha = jnp.exp(m_old - m_new)
            p = jnp.exp(s - m_new)
            l_ref[hh] = alpha * l_ref[hh] + jnp.sum(p, axis=-1, keepdims=True)
            acc_ref[hh] = alpha * acc_ref[hh] + jnp.dot(
                p.astype(BF16), vb, preferred_element_type=F32)
            m_ref[hh] = m_new

    def body(j, carry):
        step(j, masked=False)
        return carry

    lax.fori_loop(0, i, body, 0)
    step(i, masked=True)

    o0 = acc_ref[0] / l_ref[0]
    o1 = acc_ref[1] / l_ref[1]
    o = jnp.where(lane < HEAD_DIM, o0, o1)
    o_ref[...] = (o * _silu(z_ref[...])).astype(BF16)


def _attention(q, k, v, c, z):
    S, D = q.shape
    bq = BLOCK_Q
    c3 = c.reshape(N_PAIRS, 2, S)
    return pl.pallas_call(
        _attn_kernel,
        grid=(N_PAIRS, S // bq),
        in_specs=[
            pl.BlockSpec((bq, LANES), lambda p, i: (i, p)),
            pl.BlockSpec((S, LANES), lambda p, i: (0, p)),
            pl.BlockSpec((S, LANES), lambda p, i: (0, p)),
            pl.BlockSpec((1, 2, S), lambda p, i: (p, 0, 0)),
            pl.BlockSpec((bq, LANES), lambda p, i: (i, p)),
        ],
        out_specs=pl.BlockSpec((bq, LANES), lambda p, i: (i, p)),
        out_shape=jax.ShapeDtypeStruct((S, D), BF16),
        scratch_shapes=[
            pltpu.VMEM((2, bq, 1), F32),
            pltpu.VMEM((2, bq, 1), F32),
            pltpu.VMEM((2, bq, LANES), F32),
        ],
        compiler_params=pltpu.CompilerParams(
            dimension_semantics=("arbitrary", "arbitrary"), vmem_limit_bytes=VMEM_LIMIT_BYTES),
        name="fox_attention",
    )(q, k, v, c3, z)


def _out_proj_kernel(x_ref, a_ref, w_ref, o_ref):
    o_ref[...] = x_ref[...] + jnp.dot(a_ref[...], w_ref[...], preferred_element_type=F32)


def _out_proj(x, a, w_out):
    S, D = x.shape
    T = ROW_TILE
    return pl.pallas_call(
        _out_proj_kernel,
        grid=(S // T,),
        in_specs=[
            pl.BlockSpec((T, D), lambda i: (i, 0)),
            pl.BlockSpec((T, D), lambda i: (i, 0)),
            pl.BlockSpec((D, D), lambda i: (0, 0)),
        ],
        out_specs=pl.BlockSpec((T, D), lambda i: (i, 0)),
        out_shape=jax.ShapeDtypeStruct((S, D), F32),
        compiler_params=pltpu.CompilerParams(
            dimension_semantics=("arbitrary",), vmem_limit_bytes=VMEM_LIMIT_BYTES),
        name="attn_out_proj",
    )(x, a, w_out)


def kernel(x, conv_norm_g, conv_w_in, conv_w, conv_w_out, attn_norm_g, attn_w_in,
           attn_b_f, attn_q_norm_g, attn_k_norm_g, attn_w_out):
    B, S, D = x.shape
    assert (B, D) == (1, D_MODEL) and S % BLOCK_Q == 0 and S % ROW_TILE == 0
    assert conv_w_in.shape[0] == 1 and attn_w_in.shape[0] == 1
    x2 = x.reshape(S, D)

    x2 = _conv_layer(x2, conv_norm_g[0].reshape(1, D), conv_w_in[0].astype(BF16),
                     conv_w[0], conv_w_out[0].astype(BF16))

    w_in = attn_w_in[0]
    w_qkvz = w_in[:, :4 * D].astype(BF16)
    w_f_t = w_in[:, 4 * D:].T.astype(BF16)
    q, k, v, z, c = _attn_proj(
        x2, attn_norm_g[0].reshape(1, D), w_qkvz, w_f_t,
        attn_b_f[0].reshape(N_HEADS, 1),
        jnp.tile(attn_q_norm_g[0], N_HEADS).reshape(1, D),
        jnp.tile(attn_k_norm_g[0], N_HEADS).reshape(1, D))
    a = _attention(q, k, v, c, z)
    out = _out_proj(x2, a, attn_w_out[0].astype(BF16))
    return out.reshape(B, S, D)
```

```python
import functools
import math

import jax
import jax.numpy as jnp
from jax import lax
from jax.experimental import pallas as pl
from jax.experimental.pallas import tpu as pltpu

D_MODEL = 1024
HEAD_DIM = 64
N_HEADS = D_MODEL // HEAD_DIM
N_PAIRS = N_HEADS // 2
CONV_WIDTH = 3
RMS_EPS = 1e-6

LANES = 128
SUBLANES = 8
MXU_DIM = 256
VMEM_LIMIT_BYTES = 56 * 1024 * 1024

ROW_TILE = 512
BLOCK_Q = 512
BLOCK_K = 512
NEG_BIG = -1e30
STREAM_BLOCK_Q = 512
STREAM_BLOCK_K = 512
STREAM_UNROLL = 2
LOGIT_BOUND_MAX = 60.0

BF16 = jnp.bfloat16
F32 = jnp.float32


def _rmsnorm_rows(x, g):
    inv = lax.rsqrt(jnp.mean(x * x, axis=-1, keepdims=True) + RMS_EPS)
    return x * inv * g


def _silu(z):
    return z * jax.nn.sigmoid(z)


def _conv_layer_kernel(x_ref, g_ref, w_in_ref, cw_ref, w_out_ref, o_ref, tail_ref):
    i = pl.program_id(0)
    D = D_MODEL
    T = x_ref.shape[0]

    @pl.when(i == 0)
    def _():
        tail_ref[...] = jnp.zeros_like(tail_ref)

    x = x_ref[...]
    h = _rmsnorm_rows(x, g_ref[...]).astype(BF16)
    proj = jnp.dot(h, w_in_ref[...], preferred_element_type=F32)
    b_g = proj[:, 0 * D:1 * D]
    c_g = proj[:, 1 * D:2 * D]
    xin = proj[:, 2 * D:3 * D]
    z = proj[:, 3 * D:4 * D]
    u = c_g * xin

    tail = tail_ref[...]
    prev1 = tail[SUBLANES - 1:SUBLANES, :]
    prev2 = tail[SUBLANES - 2:SUBLANES - 1, :]
    row = lax.broadcasted_iota(jnp.int32, (T, 1), 0)
    u1 = jnp.where(row == 0, prev1, pltpu.roll(u, 1, axis=0))
    u2 = jnp.where(row == 0, prev2, jnp.where(row == 1, prev1, pltpu.roll(u, 2, axis=0)))
    tail_ref[...] = u[T - SUBLANES:, :]

    cw = cw_ref[...]
    y = cw[0:1, :] * u2 + cw[1:2, :] * u1 + cw[2:3, :] * u
    y = b_g * y * _silu(z)
    o_ref[...] = x + jnp.dot(y.astype(BF16), w_out_ref[...], preferred_element_type=F32)


def _conv_layer(x, g, w_in, cw, w_out):
    S, D = x.shape
    const = lambda i: (0, 0)
    return pl.pallas_call(
        _conv_layer_kernel,
        grid=(S // ROW_TILE,),
        in_specs=[
            pl.BlockSpec((ROW_TILE, D), lambda i: (i, 0)),
            pl.BlockSpec((1, D), const),
            pl.BlockSpec((D, 4 * D), const),
            pl.BlockSpec((CONV_WIDTH, D), const),
            pl.BlockSpec((D, D), const),
        ],
        out_specs=pl.BlockSpec((ROW_TILE, D), lambda i: (i, 0)),
        out_shape=jax.ShapeDtypeStruct((S, D), F32),
        scratch_shapes=[pltpu.VMEM((SUBLANES, D), F32)],
        compiler_params=pltpu.CompilerParams(
            dimension_semantics=("arbitrary",), vmem_limit_bytes=VMEM_LIMIT_BYTES),
        name="conv_layer",
    )(x, g, w_in, cw, w_out)


def _head_rmsnorm(t, g, gmat):
    sq = t * t
    hi = sq.astype(BF16)
    lo = (sq - hi.astype(F32)).astype(BF16)
    parts = []
    for c in range(D_MODEL // MXU_DIM):
        sl = slice(c * MXU_DIM, (c + 1) * MXU_DIM)
        parts.append(jnp.dot(hi[:, sl], gmat, preferred_element_type=F32)
                     + jnp.dot(lo[:, sl], gmat, preferred_element_type=F32))
    ms = jnp.concatenate(parts, axis=-1)
    return t * lax.rsqrt(ms + RMS_EPS) * g


def _attn_proj_kernel(x_ref, g_ref, w_ref, wf_ref, bf_ref, qg_ref, kg_ref, gmat_ref, tri_ref,
                      q_ref, k_ref, v_ref, z_ref, c_ref, crow_ref, run_ref):
    i = pl.program_id(0)
    D = D_MODEL

    @pl.when(i == 0)
    def _():
        run_ref[...] = jnp.zeros_like(run_ref)

    h = _rmsnorm_rows(x_ref[...], g_ref[...]).astype(BF16)
    proj = jnp.dot(h, w_ref[...], preferred_element_type=F32)
    gmat = gmat_ref[...]
    scale = 1.0 / math.sqrt(HEAD_DIM)
    q_ref[...] = (_head_rmsnorm(proj[:, 0 * D:1 * D], qg_ref[...], gmat) * scale).astype(BF16)
    k_ref[...] = _head_rmsnorm(proj[:, 1 * D:2 * D], kg_ref[...], gmat).astype(BF16)
    v_ref[...] = proj[:, 2 * D:3 * D].astype(BF16)
    z_ref[...] = proj[:, 3 * D:4 * D]

    f = lax.dot_general(wf_ref[...], h, (((1,), (1,)), ((), ())),
                        preferred_element_type=F32) + bf_ref[...]
    logf = -(jnp.maximum(-f, 0.0) + jnp.log1p(jnp.exp(-jnp.abs(f))))
    tri = tri_ref[...]
    p1 = logf.astype(BF16)
    r1 = logf - p1.astype(F32)
    p2 = r1.astype(BF16)
    p3 = (r1 - p2.astype(F32)).astype(BF16)
    cs = (jnp.dot(p1, tri, preferred_element_type=F32)
          + jnp.dot(p2, tri, preferred_element_type=F32)
          + jnp.dot(p3, tri, preferred_element_type=F32))
    c = cs + run_ref[...][:, 0:1]
    c_ref[...] = c
    T = c.shape[1]
    crow_ref[...] = jnp.concatenate([c, jnp.zeros((LANES - N_HEADS, T), F32)], axis=0).T
    run_ref[...] = jnp.broadcast_to(c[:, T - 1:T], run_ref.shape)


def _attn_proj(x, g, w_qkvz, w_f_t, b_f, q_g, k_g):
    S, D = x.shape
    T = ROW_TILE
    head_of = jnp.arange(MXU_DIM) // HEAD_DIM
    gmat = jnp.where(head_of[:, None] == head_of[None, :], 1.0 / HEAD_DIM, 0.0).astype(BF16)
    tri = (jnp.arange(T)[:, None] <= jnp.arange(T)[None, :]).astype(BF16)
    const = lambda i: (0, 0)
    rows = lambda i: (i, 0)
    return pl.pallas_call(
        _attn_proj_kernel,
        grid=(S // T,),
        in_specs=[
            pl.BlockSpec((T, D), rows),
            pl.BlockSpec((1, D), const),
            pl.BlockSpec((D, 4 * D), const),
            pl.BlockSpec((N_HEADS, D), const),
            pl.BlockSpec((N_HEADS, 1), const),
            pl.BlockSpec((1, D), const),
            pl.BlockSpec((1, D), const),
            pl.BlockSpec((MXU_DIM, MXU_DIM), const),
            pl.BlockSpec((T, T), const),
        ],
        out_specs=[
            pl.BlockSpec((T, D), rows),
            pl.BlockSpec((T, D), rows),
            pl.BlockSpec((T, D), rows),
            pl.BlockSpec((T, D), rows),
            pl.BlockSpec((N_HEADS, T), lambda i: (0, i)),
            pl.BlockSpec((T, LANES), rows),
        ],
        out_shape=[
            jax.ShapeDtypeStruct((S, D), BF16),
            jax.ShapeDtypeStruct((S, D), BF16),
            jax.ShapeDtypeStruct((S, D), BF16),
            jax.ShapeDtypeStruct((S, D), F32),
            jax.ShapeDtypeStruct((N_HEADS, S), F32),
            jax.ShapeDtypeStruct((S, LANES), F32),
        ],
        scratch_shapes=[pltpu.VMEM((N_HEADS, LANES), F32)],
        compiler_params=pltpu.CompilerParams(
            dimension_semantics=("arbitrary",), vmem_limit_bytes=VMEM_LIMIT_BYTES),
        name="attn_proj",
    )(x, g, w_qkvz, w_f_t, b_f, q_g, k_g, gmat, tri)


def _attn_kernel(q_ref, k_ref, v_ref, c_ref, z_ref, o_ref, m_ref, l_ref, acc_ref):
    i = pl.program_id(1)
    bq, bk = BLOCK_Q, BLOCK_K
    q2 = q_ref[...]
    lane = lax.broadcasted_iota(jnp.int32, (1, LANES), 1)
    zero = jnp.zeros_like(q2)
    q_heads = (jnp.where(lane < HEAD_DIM, q2, zero), jnp.where(lane >= HEAD_DIM, q2, zero))

    m_ref[...] = jnp.full_like(m_ref, NEG_BIG)
    l_ref[...] = jnp.zeros_like(l_ref)
    acc_ref[...] = jnp.zeros_like(acc_ref)

    def step(j, masked):
        start = pl.multiple_of(j * bk, bk)
        kb = k_ref[pl.ds(start, bk), :]
        vb = v_ref[pl.ds(start, bk), :]
        cb = c_ref[0, :, pl.ds(start, bk)]
        for hh in range(2):
            s = lax.dot_general(q_heads[hh], kb, (((1,), (1,)), ((), ())),
                                preferred_element_type=F32)
            s = s - cb[hh:hh + 1, :]
            if masked:
                row = lax.broadcasted_iota(jnp.int32, (bq, bk), 0)
                col = lax.broadcasted_iota(jnp.int32, (bq, bk), 1)
                s = jnp.where(col <= row, s, NEG_BIG)
            m_old = m_ref[hh]
            m_new = jnp.maximum(m_old, jnp.max(s, axis=-1, keepdims=True))
            alpha = jnp.exp(m_old - m_new)
            p = jnp.exp(s - m_new)
            l_ref[hh] = alpha * l_ref[hh] + jnp.sum(p, axis=-1, keepdims=True)
            acc_ref[hh] = alpha * acc_ref[hh] + jnp.dot(
                p.astype(BF16), vb, preferred_element_type=F32)
            m_ref[hh] = m_new

    def body(j, carry):
        step(j, masked=False)
        return carry

    lax.fori_loop(0, i, body, 0)
    step(i, masked=True)

    o0 = acc_ref[0] / l_ref[0]
    o1 = acc_ref[1] / l_ref[1]
    o = jnp.where(lane < HEAD_DIM, o0, o1)
    o_ref[...] = (o * _silu(z_ref[...])).astype(BF16)


def _attention(q, k, v, c, z):
    S, D = q.shape
    bq = BLOCK_Q
    c3 = c.reshape(N_PAIRS, 2, S)
    return pl.pallas_call(
        _attn_kernel,
        grid=(N_PAIRS, S // bq),
        in_specs=[
            pl.BlockSpec((bq, LANES), lambda p, i: (i, p)),
            pl.BlockSpec((S, LANES), lambda p, i: (0, p)),
            pl.BlockSpec((S, LANES), lambda p, i: (0, p)),
            pl.BlockSpec((1, 2, S), lambda p, i: (p, 0, 0)),
            pl.BlockSpec((bq, LANES), lambda p, i: (i, p)),
        ],
        out_specs=pl.BlockSpec((bq, LANES), lambda p, i: (i, p)),
        out_shape=jax.ShapeDtypeStruct((S, D), BF16),
        scratch_shapes=[
            pltpu.VMEM((2, bq, 1), F32),
            pltpu.VMEM((2, bq, 1), F32),
            pltpu.VMEM((2, bq, LANES), F32),
        ],
        compiler_params=pltpu.CompilerParams(
            dimension_semantics=("arbitrary", "arbitrary"), vmem_limit_bytes=VMEM_LIMIT_BYTES),
        name="fox_attention",
    )(q, k, v, c3, z)


def _attn_stream_kernel(q_ref, k_ref, v_ref, c_ref, crow_ref, z_ref, o_ref,
                        ci_ref, l_ref, acc_ref):
    pair = pl.program_id(0)
    i = pl.program_id(1)
    bq, bk = STREAM_BLOCK_Q, STREAM_BLOCK_K
    q2 = q_ref[...]
    lane = lax.broadcasted_iota(jnp.int32, (1, LANES), 1)
    zero = jnp.zeros_like(q2)
    qs = jnp.concatenate([jnp.where(lane < HEAD_DIM, q2, zero),
                          jnp.where(lane >= HEAD_DIM, q2, zero)], axis=0)

    crow = crow_ref[...]
    for hh in range(2):
        ci = jnp.sum(jnp.where(lane == 2 * pair + hh, crow, 0.0), axis=-1, keepdims=True)
        ci_ref[hh * bq:(hh + 1) * bq, :] = jnp.broadcast_to(ci, (bq, bk))
    l_ref[...] = jnp.zeros_like(l_ref)
    acc_ref[...] = jnp.zeros_like(acc_ref)

    def key_block(j, masked):
        start = pl.multiple_of(j * bk, bk)
        kb = k_ref[pl.ds(start, bk), :]
        vb = v_ref[pl.ds(start, bk), :]
        cj = c_ref[0, :, pl.ds(start, bk)]
        s = lax.dot_general(qs, kb, (((1,), (1,)), ((), ())),
                            preferred_element_type=F32)
        cjb = jnp.concatenate([jnp.broadcast_to(cj[0:1, :], (bq, bk)),
                               jnp.broadcast_to(cj[1:2, :], (bq, bk))], axis=0)
        s = s + (ci_ref[...] - cjb)
        if masked:
            row = lax.broadcasted_iota(jnp.int32, (2 * bq, bk), 0) & (bq - 1)
            col = lax.broadcasted_iota(jnp.int32, (2 * bq, bk), 1)
            s = jnp.where(col <= row, s, NEG_BIG)
        p = jnp.exp(s)
        lp = p[:, 0:LANES]
        for t in range(1, bk // LANES):
            lp = lp + p[:, t * LANES:(t + 1) * LANES]
        return lp, jnp.dot(p.astype(BF16), vb, preferred_element_type=F32)

    def accumulate(terms):
        lps, pvs = zip(*terms)
        l_ref[...] += functools.reduce(jnp.add, lps)
        acc_ref[...] += functools.reduce(jnp.add, pvs)

    def group_body(g, carry):
        accumulate([key_block(g * STREAM_UNROLL + u, False) for u in range(STREAM_UNROLL)])
        return carry

    def single_body(j, carry):
        accumulate([key_block(j, False)])
        return carry

    n_groups = i // STREAM_UNROLL
    lax.fori_loop(0, n_groups, group_body, 0)
    lax.fori_loop(n_groups * STREAM_UNROLL, i, single_body, 0)
    accumulate([key_block(i, True)])

    o = acc_ref[...] / jnp.sum(l_ref[...], axis=-1, keepdims=True)
    o = jnp.where(lane < HEAD_DIM, o[:bq], o[bq:])
    o_ref[...] = (o * _silu(z_ref[...])).astype(BF16)


def _attention_stream(q, k, v, c, crow, z):
    S, D = q.shape
    bq, bk = STREAM_BLOCK_Q, STREAM_BLOCK_K
    c3 = c.reshape(N_PAIRS, 2, S)
    return pl.pallas_call(
        _attn_stream_kernel,
        grid=(N_PAIRS, S // bq),
        in_specs=[
            pl.BlockSpec((bq, LANES), lambda p, i: (i, p)),
            pl.BlockSpec((S, LANES), lambda p, i: (0, p)),
            pl.BlockSpec((S, LANES), lambda p, i: (0, p)),
            pl.BlockSpec((1, 2, S), lambda p, i: (p, 0, 0)),
            pl.BlockSpec((bq, LANES), lambda p, i: (i, 0)),
            pl.BlockSpec((bq, LANES), lambda p, i: (i, p)),
        ],
        out_specs=pl.BlockSpec((bq, LANES), lambda p, i: (i, p)),
        out_shape=jax.ShapeDtypeStruct((S, D), BF16),
        scratch_shapes=[
            pltpu.VMEM((2 * bq, bk), F32),
            pltpu.VMEM((2 * bq, LANES), F32),
            pltpu.VMEM((2 * bq, LANES), F32),
        ],
        compiler_params=pltpu.CompilerParams(
            dimension_semantics=("arbitrary", "arbitrary"), vmem_limit_bytes=VMEM_LIMIT_BYTES),
        name="fox_attention_stream",
    )(q, k, v, c3, crow, z)


def _out_proj_kernel(x_ref, a_ref, w_ref, o_ref):
    o_ref[...] = x_ref[...] + jnp.dot(a_ref[...], w_ref[...], preferred_element_type=F32)


def _out_proj(x, a, w_out):
    S, D = x.shape
    T = ROW_TILE
    return pl.pallas_call(
        _out_proj_kernel,
        grid=(S // T,),
        in_specs=[
            pl.BlockSpec((T, D), lambda i: (i, 0)),
            pl.BlockSpec((T, D), lambda i: (i, 0)),
            pl.BlockSpec((D, D), lambda i: (0, 0)),
        ],
        out_specs=pl.BlockSpec((T, D), lambda i: (i, 0)),
        out_shape=jax.ShapeDtypeStruct((S, D), F32),
        compiler_params=pltpu.CompilerParams(
            dimension_semantics=("arbitrary",), vmem_limit_bytes=VMEM_LIMIT_BYTES),
        name="attn_out_proj",
    )(x, a, w_out)


def kernel(x, conv_norm_g, conv_w_in, conv_w, conv_w_out, attn_norm_g, attn_w_in,
           attn_b_f, attn_q_norm_g, attn_k_norm_g, attn_w_out):
    B, S, D = x.shape
    assert (B, D) == (1, D_MODEL) and S % BLOCK_Q == 0 and S % ROW_TILE == 0
    assert conv_w_in.shape[0] == 1 and attn_w_in.shape[0] == 1
    x2 = x.reshape(S, D)

    x2 = _conv_layer(x2, conv_norm_g[0].reshape(1, D), conv_w_in[0].astype(BF16),
                     conv_w[0], conv_w_out[0].astype(BF16))

    w_in = attn_w_in[0]
    w_qkvz = w_in[:, :4 * D].astype(BF16)
    w_f_t = w_in[:, 4 * D:].T.astype(BF16)
    q, k, v, z, c, crow = _attn_proj(
        x2, attn_norm_g[0].reshape(1, D), w_qkvz, w_f_t,
        attn_b_f[0].reshape(N_HEADS, 1),
        jnp.tile(attn_q_norm_g[0], N_HEADS).reshape(1, D),
        jnp.tile(attn_k_norm_g[0], N_HEADS).reshape(1, D))
    logit_bound = (math.sqrt(HEAD_DIM) * jnp.max(jnp.abs(attn_q_norm_g[0]))
                   * jnp.max(jnp.abs(attn_k_norm_g[0])))
    a = lax.cond(logit_bound <= LOGIT_BOUND_MAX,
                 lambda: _attention_stream(q, k, v, c, crow, z),
                 lambda: _attention(q, k, v, c, z))
    out = _out_proj(x2, a, attn_w_out[0].astype(BF16))
    return out.reshape(B, S, D)
```

```python
import functools
import math

import jax
import jax.numpy as jnp
from jax import lax
from jax.experimental import pallas as pl
from jax.experimental.pallas import tpu as pltpu

D_MODEL = 1024
HEAD_DIM = 64
N_HEADS = D_MODEL // HEAD_DIM
N_PAIRS = N_HEADS // 2
CONV_WIDTH = 3
RMS_EPS = 1e-6

LANES = 128
SUBLANES = 8
MXU_DIM = 256
VMEM_LIMIT_BYTES = 56 * 1024 * 1024

ROW_TILE = 512
BLOCK_Q = 512
BLOCK_K = 512
NEG_BIG = -1e30
STREAM_BLOCK_Q = 1024
STREAM_SUPER_K = 1024
STREAM_BLOCK_K = 256
AUG_MID = N_HEADS
AUG_LO = 2 * N_HEADS
AUG_ONE = 3 * N_HEADS
LOGIT_BOUND_MAX = 60.0

BF16 = jnp.bfloat16
F32 = jnp.float32


def _rmsnorm_rows(x, g):
    inv = lax.rsqrt(jnp.mean(x * x, axis=-1, keepdims=True) + RMS_EPS)
    return x * inv * g


def _silu(z):
    return z * jax.nn.sigmoid(z)


def _conv_layer_kernel(x_ref, g_ref, w_in_ref, cw_ref, w_out_ref, o_ref, tail_ref):
    i = pl.program_id(0)
    D = D_MODEL
    T = x_ref.shape[0]

    @pl.when(i == 0)
    def _():
        tail_ref[...] = jnp.zeros_like(tail_ref)

    x = x_ref[...]
    h = _rmsnorm_rows(x, g_ref[...]).astype(BF16)
    proj = jnp.dot(h, w_in_ref[...], preferred_element_type=F32)
    b_g = proj[:, 0 * D:1 * D]
    c_g = proj[:, 1 * D:2 * D]
    xin = proj[:, 2 * D:3 * D]
    z = proj[:, 3 * D:4 * D]
    u = c_g * xin

    tail = tail_ref[...]
    prev1 = tail[SUBLANES - 1:SUBLANES, :]
    prev2 = tail[SUBLANES - 2:SUBLANES - 1, :]
    row = lax.broadcasted_iota(jnp.int32, (T, 1), 0)
    u1 = jnp.where(row == 0, prev1, pltpu.roll(u, 1, axis=0))
    u2 = jnp.where(row == 0, prev2, jnp.where(row == 1, prev1, pltpu.roll(u, 2, axis=0)))
    tail_ref[...] = u[T - SUBLANES:, :]

    cw = cw_ref[...]
    y = cw[0:1, :] * u2 + cw[1:2, :] * u1 + cw[2:3, :] * u
    y = b_g * y * _silu(z)
    o_ref[...] = x + jnp.dot(y.astype(BF16), w_out_ref[...], preferred_element_type=F32)


def _conv_layer(x, g, w_in, cw, w_out):
    S, D = x.shape
    const = lambda i: (0, 0)
    return pl.pallas_call(
        _conv_layer_kernel,
        grid=(S // ROW_TILE,),
        in_specs=[
            pl.BlockSpec((ROW_TILE, D), lambda i: (i, 0)),
            pl.BlockSpec((1, D), const),
            pl.BlockSpec((D, 4 * D), const),
            pl.BlockSpec((CONV_WIDTH, D), const),
            pl.BlockSpec((D, D), const),
        ],
        out_specs=pl.BlockSpec((ROW_TILE, D), lambda i: (i, 0)),
        out_shape=jax.ShapeDtypeStruct((S, D), F32),
        scratch_shapes=[pltpu.VMEM((SUBLANES, D), F32)],
        compiler_params=pltpu.CompilerParams(
            dimension_semantics=("arbitrary",), vmem_limit_bytes=VMEM_LIMIT_BYTES),
        name="conv_layer",
    )(x, g, w_in, cw, w_out)


def _split3(x):
    hi = x.astype(BF16).astype(F32)
    r = x - hi
    mid = r.astype(BF16).astype(F32)
    lo = (r - mid).astype(BF16).astype(F32)
    return hi, mid, lo


def _head_rmsnorm(t, g, gmat):
    sq = t * t
    hi = sq.astype(BF16)
    lo = (sq - hi.astype(F32)).astype(BF16)
    parts = []
    for c in range(D_MODEL // MXU_DIM):
        sl = slice(c * MXU_DIM, (c + 1) * MXU_DIM)
        parts.append(jnp.dot(hi[:, sl], gmat, preferred_element_type=F32)
                     + jnp.dot(lo[:, sl], gmat, preferred_element_type=F32))
    ms = jnp.concatenate(parts, axis=-1)
    return t * lax.rsqrt(ms + RMS_EPS) * g


def _attn_proj_kernel(x_ref, g_ref, wkz_ref, wqv_ref, wf_ref, bf_ref, qg_ref, kg_ref,
                      gmat_ref, tri_ref,
                      qt_ref, k_ref, vt_ref, z_ref, c_ref, caug_ref, run_ref):
    i = pl.program_id(0)
    D = D_MODEL
    T = x_ref.shape[0]

    @pl.when(i == 0)
    def _():
        run_ref[...] = jnp.zeros_like(run_ref)

    h = _rmsnorm_rows(x_ref[...], g_ref[...]).astype(BF16)
    kz = jnp.dot(h, wkz_ref[...], preferred_element_type=F32)
    k_ref[...] = _head_rmsnorm(kz[:, :D], kg_ref[...], gmat_ref[...]).astype(BF16)
    z_ref[...] = kz[:, D:]

    qvt = lax.dot_general(wqv_ref[...], h, (((1,), (1,)), ((), ())),
                          preferred_element_type=F32)
    qt = qvt[:D, :].reshape(N_HEADS, HEAD_DIM, T)
    inv = lax.rsqrt(jnp.mean(qt * qt, axis=1, keepdims=True) + RMS_EPS)
    scale = 1.0 / math.sqrt(HEAD_DIM)
    qt = qt * inv * qg_ref[...].reshape(N_HEADS, HEAD_DIM, 1) * scale
    qt_ref[...] = qt.reshape(D, T).astype(BF16)
    vt_ref[...] = qvt[D:, :].astype(BF16)

    f = lax.dot_general(wf_ref[...], h, (((1,), (1,)), ((), ())),
                        preferred_element_type=F32) + bf_ref[...]
    logf = -(jnp.maximum(-f, 0.0) + jnp.log1p(jnp.exp(-jnp.abs(f))))
    tri = tri_ref[...]
    cs = sum(jnp.dot(piece.astype(BF16), tri, preferred_element_type=F32)
             for piece in _split3(logf))
    c = cs + run_ref[...][:, 0:1]
    c_ref[...] = c
    run_ref[...] = jnp.broadcast_to(c[:, T - 1:T], run_ref.shape)

    crow = jnp.concatenate([c, jnp.zeros((LANES - N_HEADS, T), F32)], axis=0).T
    hi, mid, lo = _split3(crow)
    lane = lax.broadcasted_iota(jnp.int32, (1, LANES), 1)
    ones = jnp.where((lane >= AUG_ONE) & (lane < AUG_ONE + 3), 1.0, 0.0)
    caug = hi + pltpu.roll(mid, AUG_MID, axis=1) + pltpu.roll(lo, AUG_LO, axis=1) + ones
    caug_ref[...] = caug.astype(BF16)


def _attn_proj(x, g, w_kz, w_qv_t, w_f_t, b_f, q_g_col, k_g_row):
    S, D = x.shape
    T = ROW_TILE
    head_of = jnp.arange(MXU_DIM) // HEAD_DIM
    gmat = jnp.where(head_of[:, None] == head_of[None, :], 1.0 / HEAD_DIM, 0.0).astype(BF16)
    tri = (jnp.arange(T)[:, None] <= jnp.arange(T)[None, :]).astype(BF16)
    const = lambda i: (0, 0)
    rows = lambda i: (i, 0)
    cols = lambda i: (0, i)
    return pl.pallas_call(
        _attn_proj_kernel,
        grid=(S // T,),
        in_specs=[
            pl.BlockSpec((T, D), rows),
            pl.BlockSpec((1, D), const),
            pl.BlockSpec((D, 2 * D), const),
            pl.BlockSpec((2 * D, D), const),
            pl.BlockSpec((N_HEADS, D), const),
            pl.BlockSpec((N_HEADS, 1), const),
            pl.BlockSpec((D, 1), const),
            pl.BlockSpec((1, D), const),
            pl.BlockSpec((MXU_DIM, MXU_DIM), const),
            pl.BlockSpec((T, T), const),
        ],
        out_specs=[
            pl.BlockSpec((D, T), cols),
            pl.BlockSpec((T, D), rows),
            pl.BlockSpec((D, T), cols),
            pl.BlockSpec((T, D), rows),
            pl.BlockSpec((N_HEADS, T), cols),
            pl.BlockSpec((T, LANES), rows),
        ],
        out_shape=[
            jax.ShapeDtypeStruct((D, S), BF16),
            jax.ShapeDtypeStruct((S, D), BF16),
            jax.ShapeDtypeStruct((D, S), BF16),
            jax.ShapeDtypeStruct((S, D), F32),
            jax.ShapeDtypeStruct((N_HEADS, S), F32),
            jax.ShapeDtypeStruct((S, LANES), BF16),
        ],
        scratch_shapes=[pltpu.VMEM((N_HEADS, LANES), F32)],
        compiler_params=pltpu.CompilerParams(
            dimension_semantics=("arbitrary",), vmem_limit_bytes=VMEM_LIMIT_BYTES),
        name="attn_proj",
    )(x, g, w_kz, w_qv_t, w_f_t, b_f, q_g_col, k_g_row, gmat, tri)


def _attn_kernel(q_ref, k_ref, v_ref, c_ref, z_ref, o_ref, m_ref, l_ref, acc_ref):
    i = pl.program_id(1)
    bq, bk = BLOCK_Q, BLOCK_K
    q2 = q_ref[...]
    lane = lax.broadcasted_iota(jnp.int32, (1, LANES), 1)
    zero = jnp.zeros_like(q2)
    q_heads = (jnp.where(lane < HEAD_DIM, q2, zero), jnp.where(lane >= HEAD_DIM, q2, zero))

    m_ref[...] = jnp.full_like(m_ref, NEG_BIG)
    l_ref[...] = jnp.zeros_like(l_ref)
    acc_ref[...] = jnp.zeros_like(acc_ref)

    def step(j, masked):
        start = pl.multiple_of(j * bk, bk)
        kb = k_ref[pl.ds(start, bk), :]
        vb = v_ref[pl.ds(start, bk), :]
        cb = c_ref[0, :, pl.ds(start, bk)]
        for hh in range(2):
            s = lax.dot_general(q_heads[hh], kb, (((1,), (1,)), ((), ())),
                                preferred_element_type=F32)
            s = s - cb[hh:hh + 1, :]
            if masked:
                row = lax.broadcasted_iota(jnp.int32, (bq, bk), 0)
                col = lax.broadcasted_iota(jnp.int32, (bq, bk), 1)
                s = jnp.where(col <= row, s, NEG_BIG)
            m_old = m_ref[hh]
            m_new = jnp.maximum(m_old, jnp.max(s, axis=-1, keepdims=True))
            alpha = jnp.exp(m_old - m_new)
            p = jnp.exp(s - m_new)
            l_ref[hh] = alpha * l_ref[hh] + jnp.sum(p, axis=-1, keepdims=True)
            acc_ref[hh] = alpha * acc_ref[hh] + jnp.dot(
                p.astype(BF16), vb, preferred_element_type=F32)
            m_ref[hh] = m_new

    def body(j, carry):
        step(j, masked=False)
        return carry

    lax.fori_loop(0, i, body, 0)
    step(i, masked=True)

    o0 = acc_ref[0] / l_ref[0]
    o1 = acc_ref[1] / l_ref[1]
    o = jnp.where(lane < HEAD_DIM, o0, o1)
    o_ref[...] = (o * _silu(z_ref[...])).astype(BF16)


def _attention(q, k, v, c, z):
    S, D = q.shape
    bq = BLOCK_Q
    c3 = c.reshape(N_PAIRS, 2, S)
    return pl.pallas_call(
        _attn_kernel,
        grid=(N_PAIRS, S // bq),
        in_specs=[
            pl.BlockSpec((bq, LANES), lambda p, i: (i, p)),
            pl.BlockSpec((S, LANES), lambda p, i: (0, p)),
            pl.BlockSpec((S, LANES), lambda p, i: (0, p)),
            pl.BlockSpec((1, 2, S), lambda p, i: (p, 0, 0)),
            pl.BlockSpec((bq, LANES), lambda p, i: (i, p)),
        ],
        out_specs=pl.BlockSpec((bq, LANES), lambda p, i: (i, p)),
        out_shape=jax.ShapeDtypeStruct((S, D), BF16),
        scratch_shapes=[
            pltpu.VMEM((2, bq, 1), F32),
            pltpu.VMEM((2, bq, 1), F32),
            pltpu.VMEM((2, bq, LANES), F32),
        ],
        compiler_params=pltpu.CompilerParams(
            dimension_semantics=("arbitrary", "arbitrary"), vmem_limit_bytes=VMEM_LIMIT_BYTES),
        name="fox_attention",
    )(q, k, v, c3, z)


def _attn_stream_kernel(qt_ref, k_ref, caug_ref, vt_ref, c_ref, z_ref, o_ref,
                        qa_ref, l_ref, acc_ref, st_ref):
    pair = pl.program_id(0)
    i = pl.program_id(1)
    bq, bk, ks = STREAM_BLOCK_Q, STREAM_BLOCK_K, STREAM_SUPER_K

    qt = qt_ref[...]
    r = lax.broadcasted_iota(jnp.int32, (LANES, 1), 0)
    zero = jnp.zeros_like(qt)
    top = jnp.concatenate([jnp.where(r < HEAD_DIM, qt, zero),
                           jnp.where(r >= HEAD_DIM, qt, zero)], axis=1)
    ci = c_ref[0]
    halves = []
    for hh in range(2):
        head = 2 * pair + hh
        c_hi, c_mid, c_lo = _split3(ci[hh:hh + 1, :])
        minus_cj = jnp.where((r == head) | (r == AUG_MID + head) | (r == AUG_LO + head), -1.0, 0.0)
        halves.append(minus_cj + jnp.where(r == AUG_ONE, c_hi, 0.0)
                      + jnp.where(r == AUG_ONE + 1, c_mid, 0.0)
                      + jnp.where(r == AUG_ONE + 2, c_lo, 0.0))
    qa_ref[...] = jnp.concatenate([top, jnp.concatenate(halves, axis=1).astype(BF16)], axis=0)
    l_ref[...] = jnp.zeros_like(l_ref)
    acc_ref[...] = jnp.zeros_like(acc_ref)

    def score_chunk(dst_ref, t, u, masked):
        start = pl.multiple_of(t * ks + u * bk, bk)
        ka = jnp.concatenate([k_ref[pl.ds(start, bk), :], caug_ref[pl.ds(start, bk), :]],
                             axis=1)
        st = jnp.dot(ka, qa_ref[...], preferred_element_type=F32)
        if masked:
            key = start + lax.broadcasted_iota(jnp.int32, (bk, 2 * bq), 0)
            qry = i * bq + (lax.broadcasted_iota(jnp.int32, (bk, 2 * bq), 1) & (bq - 1))
            st = jnp.where(key <= qry, st, NEG_BIG)
        dst_ref[u * bk:(u + 1) * bk, :] = st

    def exp_chunk(src_ref, t, u):
        start = pl.multiple_of(t * ks + u * bk, bk)
        p = jnp.exp(src_ref[u * bk:(u + 1) * bk, :])
        lp = jnp.sum(p.reshape(bk // SUBLANES, SUBLANES, 2 * bq), axis=0)
        pb = p.astype(BF16)
        pv = [jnp.dot(vt_ref[hh * HEAD_DIM:(hh + 1) * HEAD_DIM, pl.ds(start, bk)],
                      pb[:, hh * bq:(hh + 1) * bq], preferred_element_type=F32)
              for hh in range(2)]
        return lp, pv

    def stage(fill=None, drain=None):
        terms = []
        for u in range(ks // bk):
            if drain is not None:
                terms.append(exp_chunk(st_ref, drain, u))
            if fill is not None:
                score_chunk(st_ref, fill[0], u, fill[1])
        if terms:
            l_ref[...] += functools.reduce(jnp.add, [t[0] for t in terms])
            for hh in range(2):
                acc_ref[hh] += functools.reduce(jnp.add, [t[1][hh] for t in terms])

    assert bq == ks
    stage(fill=(i, True))

    def body(t, pending):
        stage(fill=(t, False), drain=pending)
        return t

    pending = lax.fori_loop(0, i, body, i)
    stage(drain=pending)

    l = jnp.sum(l_ref[...], axis=0, keepdims=True)
    ot = jnp.concatenate([acc_ref[0] / l[:, :bq], acc_ref[1] / l[:, bq:]], axis=0)
    o_ref[...] = (ot.T * _silu(z_ref[...])).astype(BF16)


def _attention_stream(qt, k, caug, vt, c, z):
    D, S = qt.shape
    bq = STREAM_BLOCK_Q
    c3 = c.reshape(N_PAIRS, 2, S)
    return pl.pallas_call(
        _attn_stream_kernel,
        grid=(N_PAIRS, S // bq),
        in_specs=[
            pl.BlockSpec((LANES, bq), lambda p, i: (p, i)),
            pl.BlockSpec((S, LANES), lambda p, i: (0, p)),
            pl.BlockSpec((S, LANES), lambda p, i: (0, 0)),
            pl.BlockSpec((LANES, S), lambda p, i: (p, 0)),
            pl.BlockSpec((1, 2, bq), lambda p, i: (p, 0, i)),
            pl.BlockSpec((bq, LANES), lambda p, i: (i, p)),
        ],
        out_specs=pl.BlockSpec((bq, LANES), lambda p, i: (i, p)),
        out_shape=jax.ShapeDtypeStruct((S, D), BF16),
        scratch_shapes=[
            pltpu.VMEM((2 * LANES, 2 * bq), BF16),
            pltpu.VMEM((SUBLANES, 2 * bq), F32),
            pltpu.VMEM((2, HEAD_DIM, bq), F32),
            pltpu.VMEM((STREAM_SUPER_K, 2 * bq), F32),
        ],
        compiler_params=pltpu.CompilerParams(
            dimension_semantics=("arbitrary", "arbitrary"), vmem_limit_bytes=VMEM_LIMIT_BYTES),
        name="fox_attention_stream",
    )(qt, k, caug, vt, c3, z)


def _out_proj_kernel(x_ref, a_ref, w_ref, o_ref):
    o_ref[...] = x_ref[...] + jnp.dot(a_ref[...], w_ref[...], preferred_element_type=F32)


def _out_proj(x, a, w_out):
    S, D = x.shape
    T = ROW_TILE
    return pl.pallas_call(
        _out_proj_kernel,
        grid=(S // T,),
        in_specs=[
            pl.BlockSpec((T, D), lambda i: (i, 0)),
            pl.BlockSpec((T, D), lambda i: (i, 0)),
            pl.BlockSpec((D, D), lambda i: (0, 0)),
        ],
        out_specs=pl.BlockSpec((T, D), lambda i: (i, 0)),
        out_shape=jax.ShapeDtypeStruct((S, D), F32),
        compiler_params=pltpu.CompilerParams(
            dimension_semantics=("arbitrary",), vmem_limit_bytes=VMEM_LIMIT_BYTES),
        name="attn_out_proj",
    )(x, a, w_out)


def kernel(x, conv_norm_g, conv_w_in, conv_w, conv_w_out, attn_norm_g, attn_w_in,
           attn_b_f, attn_q_norm_g, attn_k_norm_g, attn_w_out):
    B, S, D = x.shape
    assert (B, D) == (1, D_MODEL) and S % BLOCK_Q == 0 and S % ROW_TILE == 0
    assert conv_w_in.shape[0] == 1 and attn_w_in.shape[0] == 1
    x2 = x.reshape(S, D)

    x2 = _conv_layer(x2, conv_norm_g[0].reshape(1, D), conv_w_in[0].astype(BF16),
                     conv_w[0], conv_w_out[0].astype(BF16))

    w_in = attn_w_in[0]
    w_kz = jnp.concatenate([w_in[:, D:2 * D], w_in[:, 3 * D:4 * D]], axis=1).astype(BF16)
    w_qv_t = jnp.concatenate([w_in[:, :D], w_in[:, 2 * D:3 * D]], axis=1).T.astype(BF16)
    w_f_t = w_in[:, 4 * D:].T.astype(BF16)
    qt, k, vt, z, c, caug = _attn_proj(
        x2, attn_norm_g[0].reshape(1, D), w_kz, w_qv_t, w_f_t,
        attn_b_f[0].reshape(N_HEADS, 1),
        jnp.tile(attn_q_norm_g[0], N_HEADS).reshape(D, 1),
        jnp.tile(attn_k_norm_g[0], N_HEADS).reshape(1, D))
    logit_bound = (math.sqrt(HEAD_DIM) * jnp.max(jnp.abs(attn_q_norm_g[0]))
                   * jnp.max(jnp.abs(attn_k_norm_g[0])))
    a = lax.cond(logit_bound <= LOGIT_BOUND_MAX,
                 lambda: _attention_stream(qt, k, caug, vt, c, z),
                 lambda: _attention(qt.T, k, vt.T, c, z))
    out = _out_proj(x2, a, attn_w_out[0].astype(BF16))
    return out.reshape(B, S, D)
```

```python
import functools
import math

import jax
import jax.numpy as jnp
from jax import lax
from jax.experimental import pallas as pl
from jax.experimental.pallas import tpu as pltpu

D_MODEL = 1024
HEAD_DIM = 64
N_HEADS = D_MODEL // HEAD_DIM
N_PAIRS = N_HEADS // 2
CONV_WIDTH = 3
RMS_EPS = 1e-6

LANES = 128
SUBLANES = 8
MXU_DIM = 256
VMEM_LIMIT_BYTES = 56 * 1024 * 1024

ROW_TILE = 512
BLOCK_Q = 512
BLOCK_K = 512
NEG_BIG = -1e30
STREAM_BLOCK_Q = 1024
STREAM_SUPER_K = 1024
STREAM_BLOCK_K = 512
DIAG_BLOCK_K = 256
AUG_MID = N_HEADS
AUG_LO = 2 * N_HEADS
AUG_ONE = 3 * N_HEADS
LOGIT_BOUND_MAX = 60.0

BF16 = jnp.bfloat16
F32 = jnp.float32


def _rmsnorm_rows(x, g):
    inv = lax.rsqrt(jnp.mean(x * x, axis=-1, keepdims=True) + RMS_EPS)
    return x * inv * g


def _silu(z):
    return z * jax.nn.sigmoid(z)


def _conv_layer_kernel(x_ref, g_ref, w_in_ref, cw_ref, w_out_ref, o_ref, tail_ref):
    i = pl.program_id(0)
    D = D_MODEL
    T = x_ref.shape[0]

    @pl.when(i == 0)
    def _():
        tail_ref[...] = jnp.zeros_like(tail_ref)

    x = x_ref[...]
    h = _rmsnorm_rows(x, g_ref[...]).astype(BF16)
    proj = jnp.dot(h, w_in_ref[...], preferred_element_type=F32)
    b_g = proj[:, 0 * D:1 * D]
    c_g = proj[:, 1 * D:2 * D]
    xin = proj[:, 2 * D:3 * D]
    z = proj[:, 3 * D:4 * D]
    u = c_g * xin

    tail = tail_ref[...]
    prev1 = tail[SUBLANES - 1:SUBLANES, :]
    prev2 = tail[SUBLANES - 2:SUBLANES - 1, :]
    row = lax.broadcasted_iota(jnp.int32, (T, 1), 0)
    u1 = jnp.where(row == 0, prev1, pltpu.roll(u, 1, axis=0))
    u2 = jnp.where(row == 0, prev2, jnp.where(row == 1, prev1, pltpu.roll(u, 2, axis=0)))
    tail_ref[...] = u[T - SUBLANES:, :]

    cw = cw_ref[...]
    y = cw[0:1, :] * u2 + cw[1:2, :] * u1 + cw[2:3, :] * u
    y = b_g * y * _silu(z)
    o_ref[...] = x + jnp.dot(y.astype(BF16), w_out_ref[...], preferred_element_type=F32)


def _conv_layer(x, g, w_in, cw, w_out):
    S, D = x.shape
    const = lambda i: (0, 0)
    return pl.pallas_call(
        _conv_layer_kernel,
        grid=(S // ROW_TILE,),
        in_specs=[
            pl.BlockSpec((ROW_TILE, D), lambda i: (i, 0)),
            pl.BlockSpec((1, D), const),
            pl.BlockSpec((D, 4 * D), const),
            pl.BlockSpec((CONV_WIDTH, D), const),
            pl.BlockSpec((D, D), const),
        ],
        out_specs=pl.BlockSpec((ROW_TILE, D), lambda i: (i, 0)),
        out_shape=jax.ShapeDtypeStruct((S, D), F32),
        scratch_shapes=[pltpu.VMEM((SUBLANES, D), F32)],
        compiler_params=pltpu.CompilerParams(
            dimension_semantics=("arbitrary",), vmem_limit_bytes=VMEM_LIMIT_BYTES),
        name="conv_layer",
    )(x, g, w_in, cw, w_out)


def _split3(x):
    hi = x.astype(BF16).astype(F32)
    r = x - hi
    mid = r.astype(BF16).astype(F32)
    lo = (r - mid).astype(BF16).astype(F32)
    return hi, mid, lo


def _head_rmsnorm(t, g, gmat):
    sq = t * t
    hi = sq.astype(BF16)
    lo = (sq - hi.astype(F32)).astype(BF16)
    parts = []
    for c in range(D_MODEL // MXU_DIM):
        sl = slice(c * MXU_DIM, (c + 1) * MXU_DIM)
        parts.append(jnp.dot(hi[:, sl], gmat, preferred_element_type=F32)
                     + jnp.dot(lo[:, sl], gmat, preferred_element_type=F32))
    ms = jnp.concatenate(parts, axis=-1)
    return t * lax.rsqrt(ms + RMS_EPS) * g


def _attn_proj_kernel(x_ref, g_ref, wkz_ref, wqv_ref, wf_ref, bf_ref, qg_ref, kg_ref,
                      gmat_ref, tri_ref,
                      qt_ref, k_ref, vt_ref, z_ref, c_ref, caug_ref, run_ref):
    i = pl.program_id(0)
    D = D_MODEL
    T = x_ref.shape[0]

    @pl.when(i == 0)
    def _():
        run_ref[...] = jnp.zeros_like(run_ref)

    h = _rmsnorm_rows(x_ref[...], g_ref[...]).astype(BF16)
    kz = jnp.dot(h, wkz_ref[...], preferred_element_type=F32)
    k_ref[...] = _head_rmsnorm(kz[:, :D], kg_ref[...], gmat_ref[...]).astype(BF16)
    z_ref[...] = kz[:, D:]

    qvt = lax.dot_general(wqv_ref[...], h, (((1,), (1,)), ((), ())),
                          preferred_element_type=F32)
    qt = qvt[:D, :].reshape(N_HEADS, HEAD_DIM, T)
    inv = lax.rsqrt(jnp.mean(qt * qt, axis=1, keepdims=True) + RMS_EPS)
    scale = 1.0 / math.sqrt(HEAD_DIM)
    qt = qt * inv * qg_ref[...].reshape(N_HEADS, HEAD_DIM, 1) * scale
    qt_ref[...] = qt.reshape(D, T).astype(BF16)
    vt_ref[...] = qvt[D:, :].astype(BF16)

    f = lax.dot_general(wf_ref[...], h, (((1,), (1,)), ((), ())),
                        preferred_element_type=F32) + bf_ref[...]
    logf = -(jnp.maximum(-f, 0.0) + jnp.log1p(jnp.exp(-jnp.abs(f))))
    tri = tri_ref[...]
    cs = sum(jnp.dot(piece.astype(BF16), tri, preferred_element_type=F32)
             for piece in _split3(logf))
    c = cs + run_ref[...][:, 0:1]
    c_ref[...] = c
    run_ref[...] = jnp.broadcast_to(c[:, T - 1:T], run_ref.shape)

    crow = jnp.concatenate([c, jnp.zeros((LANES - N_HEADS, T), F32)], axis=0).T
    hi, mid, lo = _split3(crow)
    lane = lax.broadcasted_iota(jnp.int32, (1, LANES), 1)
    ones = jnp.where((lane >= AUG_ONE) & (lane < AUG_ONE + 3), 1.0, 0.0)
    caug = hi + pltpu.roll(mid, AUG_MID, axis=1) + pltpu.roll(lo, AUG_LO, axis=1) + ones
    caug_ref[...] = caug.astype(BF16)


def _attn_proj(x, g, w_kz, w_qv_t, w_f_t, b_f, q_g_col, k_g_row):
    S, D = x.shape
    T = ROW_TILE
    head_of = jnp.arange(MXU_DIM) // HEAD_DIM
    gmat = jnp.where(head_of[:, None] == head_of[None, :], 1.0 / HEAD_DIM, 0.0).astype(BF16)
    tri = (jnp.arange(T)[:, None] <= jnp.arange(T)[None, :]).astype(BF16)
    const = lambda i: (0, 0)
    rows = lambda i: (i, 0)
    cols = lambda i: (0, i)
    return pl.pallas_call(
        _attn_proj_kernel,
        grid=(S // T,),
        in_specs=[
            pl.BlockSpec((T, D), rows),
            pl.BlockSpec((1, D), const),
            pl.BlockSpec((D, 2 * D), const),
            pl.BlockSpec((2 * D, D), const),
            pl.BlockSpec((N_HEADS, D), const),
            pl.BlockSpec((N_HEADS, 1), const),
            pl.BlockSpec((D, 1), const),
            pl.BlockSpec((1, D), const),
            pl.BlockSpec((MXU_DIM, MXU_DIM), const),
            pl.BlockSpec((T, T), const),
        ],
        out_specs=[
            pl.BlockSpec((D, T), cols),
            pl.BlockSpec((T, D), rows),
            pl.BlockSpec((D, T), cols),
            pl.BlockSpec((T, D), rows),
            pl.BlockSpec((N_HEADS, T), cols),
            pl.BlockSpec((T, LANES), rows),
        ],
        out_shape=[
            jax.ShapeDtypeStruct((D, S), BF16),
            jax.ShapeDtypeStruct((S, D), BF16),
            jax.ShapeDtypeStruct((D, S), BF16),
            jax.ShapeDtypeStruct((S, D), F32),
            jax.ShapeDtypeStruct((N_HEADS, S), F32),
            jax.ShapeDtypeStruct((S, LANES), BF16),
        ],
        scratch_shapes=[pltpu.VMEM((N_HEADS, LANES), F32)],
        compiler_params=pltpu.CompilerParams(
            dimension_semantics=("arbitrary",), vmem_limit_bytes=VMEM_LIMIT_BYTES),
        name="attn_proj",
    )(x, g, w_kz, w_qv_t, w_f_t, b_f, q_g_col, k_g_row, gmat, tri)


def _attn_kernel(q_ref, k_ref, v_ref, c_ref, z_ref, o_ref, m_ref, l_ref, acc_ref):
    i = pl.program_id(1)
    bq, bk = BLOCK_Q, BLOCK_K
    q2 = q_ref[...]
    lane = lax.broadcasted_iota(jnp.int32, (1, LANES), 1)
    zero = jnp.zeros_like(q2)
    q_heads = (jnp.where(lane < HEAD_DIM, q2, zero), jnp.where(lane >= HEAD_DIM, q2, zero))

    m_ref[...] = jnp.full_like(m_ref, NEG_BIG)
    l_ref[...] = jnp.zeros_like(l_ref)
    acc_ref[...] = jnp.zeros_like(acc_ref)

    def step(j, masked):
        start = pl.multiple_of(j * bk, bk)
        kb = k_ref[pl.ds(start, bk), :]
        vb = v_ref[pl.ds(start, bk), :]
        cb = c_ref[0, :, pl.ds(start, bk)]
        for hh in range(2):
            s = lax.dot_general(q_heads[hh], kb, (((1,), (1,)), ((), ())),
                                preferred_element_type=F32)
            s = s - cb[hh:hh + 1, :]
            if masked:
                row = lax.broadcasted_iota(jnp.int32, (bq, bk), 0)
                col = lax.broadcasted_iota(jnp.int32, (bq, bk), 1)
                s = jnp.where(col <= row, s, NEG_BIG)
            m_old = m_ref[hh]
            m_new = jnp.maximum(m_old, jnp.max(s, axis=-1, keepdims=True))
            alpha = jnp.exp(m_old - m_new)
            p = jnp.exp(s - m_new)
            l_ref[hh] = alpha * l_ref[hh] + jnp.sum(p, axis=-1, keepdims=True)
            acc_ref[hh] = alpha * acc_ref[hh] + jnp.dot(
                p.astype(BF16), vb, preferred_element_type=F32)
            m_ref[hh] = m_new

    def body(j, carry):
        step(j, masked=False)
        return carry

    lax.fori_loop(0, i, body, 0)
    step(i, masked=True)

    o0 = acc_ref[0] / l_ref[0]
    o1 = acc_ref[1] / l_ref[1]
    o = jnp.where(lane < HEAD_DIM, o0, o1)
    o_ref[...] = (o * _silu(z_ref[...])).astype(BF16)


def _attention(q, k, v, c, z):
    S, D = q.shape
    bq = BLOCK_Q
    c3 = c.reshape(N_PAIRS, 2, S)
    return pl.pallas_call(
        _attn_kernel,
        grid=(N_PAIRS, S // bq),
        in_specs=[
            pl.BlockSpec((bq, LANES), lambda p, i: (i, p)),
            pl.BlockSpec((S, LANES), lambda p, i: (0, p)),
            pl.BlockSpec((S, LANES), lambda p, i: (0, p)),
            pl.BlockSpec((1, 2, S), lambda p, i: (p, 0, 0)),
            pl.BlockSpec((bq, LANES), lambda p, i: (i, p)),
        ],
        out_specs=pl.BlockSpec((bq, LANES), lambda p, i: (i, p)),
        out_shape=jax.ShapeDtypeStruct((S, D), BF16),
        scratch_shapes=[
            pltpu.VMEM((2, bq, 1), F32),
            pltpu.VMEM((2, bq, 1), F32),
            pltpu.VMEM((2, bq, LANES), F32),
        ],
        compiler_params=pltpu.CompilerParams(
            dimension_semantics=("arbitrary", "arbitrary"), vmem_limit_bytes=VMEM_LIMIT_BYTES),
        name="fox_attention",
    )(q, k, v, c3, z)


def _attn_stream_kernel(qt_ref, k_ref, caug_ref, vt_ref, c_ref, z_ref, o_ref,
                        qa_ref, l_ref, acc_ref, st_ref):
    pair = pl.program_id(0)
    i = pl.program_id(1)
    bq, bk, ks = STREAM_BLOCK_Q, STREAM_BLOCK_K, STREAM_SUPER_K

    qt = qt_ref[...]
    r = lax.broadcasted_iota(jnp.int32, (LANES, 1), 0)
    zero = jnp.zeros_like(qt)
    top = jnp.concatenate([jnp.where(r < HEAD_DIM, qt, zero),
                           jnp.where(r >= HEAD_DIM, qt, zero)], axis=1)
    ci = c_ref[0]
    halves = []
    for hh in range(2):
        head = 2 * pair + hh
        c_hi, c_mid, c_lo = _split3(ci[hh:hh + 1, :])
        minus_cj = jnp.where((r == head) | (r == AUG_MID + head) | (r == AUG_LO + head), -1.0, 0.0)
        halves.append(minus_cj + jnp.where(r == AUG_ONE, c_hi, 0.0)
                      + jnp.where(r == AUG_ONE + 1, c_mid, 0.0)
                      + jnp.where(r == AUG_ONE + 2, c_lo, 0.0))
    qa_ref[...] = jnp.concatenate([top, jnp.concatenate(halves, axis=1).astype(BF16)], axis=0)
    l_ref[...] = jnp.zeros_like(l_ref)
    acc_ref[...] = jnp.zeros_like(acc_ref)

    def key_chunk(slot, start, nk, q_lo, masked):
        n = bq - q_lo
        start = pl.multiple_of(start, nk)
        ka = jnp.concatenate([k_ref[pl.ds(start, nk), :], caug_ref[pl.ds(start, nk), :]],
                             axis=1)
        qa = qa_ref[...] if q_lo == 0 else jnp.concatenate(
            [qa_ref[:, q_lo:bq], qa_ref[:, bq + q_lo:]], axis=1)
        st = jnp.dot(ka, qa, preferred_element_type=F32)
        if masked:
            key = start + lax.broadcasted_iota(jnp.int32, (nk, 2 * n), 0)
            col = lax.broadcasted_iota(jnp.int32, (nk, 2 * n), 1)
            qry = i * bq + q_lo + jnp.where(col < n, col, col - n)
            st = jnp.where(key <= qry, st, NEG_BIG)
        st_ref[slot, :nk, :2 * n] = st
        p = jnp.exp(st_ref[slot, :nk, :2 * n])
        lp = jnp.sum(p.reshape(nk // SUBLANES, SUBLANES, 2 * n), axis=0)
        pb = p.astype(BF16)
        for hh in range(2):
            l_ref[:, hh * bq + q_lo:(hh + 1) * bq] += lp[:, hh * n:(hh + 1) * n]
            acc_ref[hh, :, q_lo:] += jnp.dot(
                vt_ref[hh * HEAD_DIM:(hh + 1) * HEAD_DIM, pl.ds(start, nk)],
                pb[:, hh * n:(hh + 1) * n], preferred_element_type=F32)

    def run(chunks):
        for slot, chunk in enumerate(chunks):
            key_chunk(slot % st_ref.shape[0], *chunk)

    def full_block(t):
        return [(t * ks + u * bk, bk, 0, False) for u in range(ks // bk)]

    diagonal = [(i * ks + u * DIAG_BLOCK_K, DIAG_BLOCK_K, u * DIAG_BLOCK_K, True)
                for u in range(ks // DIAG_BLOCK_K)]
    assert bq == ks

    @pl.when(i % 2 == 0)
    def _():
        run(diagonal)

    @pl.when(i % 2 == 1)
    def _():
        run(diagonal + full_block(i - 1))

    def pair_body(g, carry):
        run(full_block(2 * g) + full_block(2 * g + 1))
        return carry

    lax.fori_loop(0, i // 2, pair_body, 0)

    l = jnp.sum(l_ref[...], axis=0, keepdims=True)
    ot = jnp.concatenate([acc_ref[0] / l[:, :bq], acc_ref[1] / l[:, bq:]], axis=0)
    o_ref[...] = (ot.T * _silu(z_ref[...])).astype(BF16)


def _attention_stream(qt, k, caug, vt, c, z):
    D, S = qt.shape
    bq = STREAM_BLOCK_Q
    c3 = c.reshape(N_PAIRS, 2, S)
    return pl.pallas_call(
        _attn_stream_kernel,
        grid=(N_PAIRS, S // bq),
        in_specs=[
            pl.BlockSpec((LANES, bq), lambda p, i: (p, i)),
            pl.BlockSpec((S, LANES), lambda p, i: (0, p)),
            pl.BlockSpec((S, LANES), lambda p, i: (0, 0)),
            pl.BlockSpec((LANES, S), lambda p, i: (p, 0)),
            pl.BlockSpec((1, 2, bq), lambda p, i: (p, 0, i)),
            pl.BlockSpec((bq, LANES), lambda p, i: (i, p)),
        ],
        out_specs=pl.BlockSpec((bq, LANES), lambda p, i: (i, p)),
        out_shape=jax.ShapeDtypeStruct((S, D), BF16),
        scratch_shapes=[
            pltpu.VMEM((2 * LANES, 2 * bq), BF16),
            pltpu.VMEM((SUBLANES, 2 * bq), F32),
            pltpu.VMEM((2, HEAD_DIM, bq), F32),
            pltpu.VMEM((2, STREAM_BLOCK_K, 2 * bq), F32),
        ],
        compiler_params=pltpu.CompilerParams(
            dimension_semantics=("arbitrary", "arbitrary"), vmem_limit_bytes=VMEM_LIMIT_BYTES),
        name="fox_attention_stream",
    )(qt, k, caug, vt, c3, z)


def _out_proj_kernel(x_ref, a_ref, w_ref, o_ref):
    o_ref[...] = x_ref[...] + jnp.dot(a_ref[...], w_ref[...], preferred_element_type=F32)


def _out_proj(x, a, w_out):
    S, D = x.shape
    T = ROW_TILE
    return pl.pallas_call(
        _out_proj_kernel,
        grid=(S // T,),
        in_specs=[
            pl.BlockSpec((T, D), lambda i: (i, 0)),
            pl.BlockSpec((T, D), lambda i: (i, 0)),
            pl.BlockSpec((D, D), lambda i: (0, 0)),
        ],
        out_specs=pl.BlockSpec((T, D), lambda i: (i, 0)),
        out_shape=jax.ShapeDtypeStruct((S, D), F32),
        compiler_params=pltpu.CompilerParams(
            dimension_semantics=("arbitrary",), vmem_limit_bytes=VMEM_LIMIT_BYTES),
        name="attn_out_proj",
    )(x, a, w_out)


def kernel(x, conv_norm_g, conv_w_in, conv_w, conv_w_out, attn_norm_g, attn_w_in,
           attn_b_f, attn_q_norm_g, attn_k_norm_g, attn_w_out):
    B, S, D = x.shape
    assert (B, D) == (1, D_MODEL) and S % BLOCK_Q == 0 and S % ROW_TILE == 0
    assert conv_w_in.shape[0] == 1 and attn_w_in.shape[0] == 1
    x2 = x.reshape(S, D)

    x2 = _conv_layer(x2, conv_norm_g[0].reshape(1, D), conv_w_in[0].astype(BF16),
                     conv_w[0], conv_w_out[0].astype(BF16))

    w_in = attn_w_in[0]
    w_kz = jnp.concatenate([w_in[:, D:2 * D], w_in[:, 3 * D:4 * D]], axis=1).astype(BF16)
    w_qv_t = jnp.concatenate([w_in[:, :D], w_in[:, 2 * D:3 * D]], axis=1).T.astype(BF16)
    w_f_t = w_in[:, 4 * D:].T.astype(BF16)
    qt, k, vt, z, c, caug = _attn_proj(
        x2, attn_norm_g[0].reshape(1, D), w_kz, w_qv_t, w_f_t,
        attn_b_f[0].reshape(N_HEADS, 1),
        jnp.tile(attn_q_norm_g[0], N_HEADS).reshape(D, 1),
        jnp.tile(attn_k_norm_g[0], N_HEADS).reshape(1, D))
    logit_bound = (math.sqrt(HEAD_DIM) * jnp.max(jnp.abs(attn_q_norm_g[0]))
                   * jnp.max(jnp.abs(attn_k_norm_g[0])))
    a = lax.cond(logit_bound <= LOGIT_BOUND_MAX,
                 lambda: _attention_stream(qt, k, caug, vt, c, z),
                 lambda: _attention(qt.T, k, vt.T, c, z))
    out = _out_proj(x2, a, attn_w_out[0].astype(BF16))
    return out.reshape(B, S, D)
```

```python
import functools
import math

import jax
import jax.numpy as jnp
from jax import lax
from jax.experimental import pallas as pl
from jax.experimental.pallas import tpu as pltpu

D_MODEL = 1024
HEAD_DIM = 64
N_HEADS = D_MODEL // HEAD_DIM
N_PAIRS = N_HEADS // 2
CONV_WIDTH = 3
RMS_EPS = 1e-6

LANES = 128
SUBLANES = 8
MXU_DIM = 256
VMEM_LIMIT_BYTES = 56 * 1024 * 1024

ROW_TILE = 512
BLOCK_Q = 512
BLOCK_K = 512
NEG_BIG = -1e30
STREAM_BLOCK_Q = 1024
STREAM_SUPER_K = 1024
STREAM_BLOCK_K = 512
DIAG_BLOCK_K = 256
AUG_MID = N_HEADS
AUG_LO = 2 * N_HEADS
AUG_ONE = 3 * N_HEADS
LOGIT_BOUND_MAX = 60.0

BF16 = jnp.bfloat16
F32 = jnp.float32


def _rmsnorm_rows(x, g):
    inv = lax.rsqrt(jnp.mean(x * x, axis=-1, keepdims=True) + RMS_EPS)
    return x * inv * g


def _silu(z):
    return z * jax.nn.sigmoid(z)


def _conv_layer_kernel(x_ref, g_ref, w_in_ref, cw_ref, w_out_ref, o_ref, tail_ref):
    i = pl.program_id(0)
    D = D_MODEL
    T = x_ref.shape[0]

    @pl.when(i == 0)
    def _():
        tail_ref[...] = jnp.zeros_like(tail_ref)

    x = x_ref[...]
    h = _rmsnorm_rows(x, g_ref[...]).astype(BF16)
    proj = jnp.dot(h, w_in_ref[...], preferred_element_type=F32)
    b_g = proj[:, 0 * D:1 * D]
    c_g = proj[:, 1 * D:2 * D]
    xin = proj[:, 2 * D:3 * D]
    z = proj[:, 3 * D:4 * D]
    u = c_g * xin

    tail = tail_ref[...]
    prev1 = tail[SUBLANES - 1:SUBLANES, :]
    prev2 = tail[SUBLANES - 2:SUBLANES - 1, :]
    row = lax.broadcasted_iota(jnp.int32, (T, 1), 0)
    u1 = jnp.where(row == 0, prev1, pltpu.roll(u, 1, axis=0))
    u2 = jnp.where(row == 0, prev2, jnp.where(row == 1, prev1, pltpu.roll(u, 2, axis=0)))
    tail_ref[...] = u[T - SUBLANES:, :]

    cw = cw_ref[...]
    y = cw[0:1, :] * u2 + cw[1:2, :] * u1 + cw[2:3, :] * u
    y = b_g * y * _silu(z)
    o_ref[...] = x + jnp.dot(y.astype(BF16), w_out_ref[...], preferred_element_type=F32)


def _conv_layer(x, g, w_in, cw, w_out):
    S, D = x.shape
    const = lambda i: (0, 0)
    return pl.pallas_call(
        _conv_layer_kernel,
        grid=(S // ROW_TILE,),
        in_specs=[
            pl.BlockSpec((ROW_TILE, D), lambda i: (i, 0)),
            pl.BlockSpec((1, D), const),
            pl.BlockSpec((D, 4 * D), const),
            pl.BlockSpec((CONV_WIDTH, D), const),
            pl.BlockSpec((D, D), const),
        ],
        out_specs=pl.BlockSpec((ROW_TILE, D), lambda i: (i, 0)),
        out_shape=jax.ShapeDtypeStruct((S, D), F32),
        scratch_shapes=[pltpu.VMEM((SUBLANES, D), F32)],
        compiler_params=pltpu.CompilerParams(
            dimension_semantics=("arbitrary",), vmem_limit_bytes=VMEM_LIMIT_BYTES),
        name="conv_layer",
    )(x, g, w_in, cw, w_out)


def _split3(x):
    hi = x.astype(BF16).astype(F32)
    r = x - hi
    mid = r.astype(BF16).astype(F32)
    lo = (r - mid).astype(BF16).astype(F32)
    return hi, mid, lo


def _head_rmsnorm(t, g, gmat):
    sq = t * t
    hi = sq.astype(BF16)
    lo = (sq - hi.astype(F32)).astype(BF16)
    parts = []
    for c in range(D_MODEL // MXU_DIM):
        sl = slice(c * MXU_DIM, (c + 1) * MXU_DIM)
        parts.append(jnp.dot(hi[:, sl], gmat, preferred_element_type=F32)
                     + jnp.dot(lo[:, sl], gmat, preferred_element_type=F32))
    ms = jnp.concatenate(parts, axis=-1)
    return t * lax.rsqrt(ms + RMS_EPS) * g


def _attn_proj_kernel(x_ref, g_ref, wkz_ref, wqv_ref, wf_ref, bf_ref, qg_ref, kg_ref,
                      gmat_ref, tri_ref,
                      qt_ref, k_ref, vt_ref, z_ref, c_ref, caug_ref, run_ref):
    i = pl.program_id(0)
    D = D_MODEL
    T = x_ref.shape[0]

    @pl.when(i == 0)
    def _():
        run_ref[...] = jnp.zeros_like(run_ref)

    h = _rmsnorm_rows(x_ref[...], g_ref[...]).astype(BF16)
    kz = jnp.dot(h, wkz_ref[...], preferred_element_type=F32)
    k_ref[...] = _head_rmsnorm(kz[:, :D], kg_ref[...], gmat_ref[...]).astype(BF16)
    z_ref[...] = kz[:, D:]

    qvt = lax.dot_general(wqv_ref[...], h, (((1,), (1,)), ((), ())),
                          preferred_element_type=F32)
    qt = qvt[:D, :].reshape(N_HEADS, HEAD_DIM, T)
    inv = lax.rsqrt(jnp.mean(qt * qt, axis=1, keepdims=True) + RMS_EPS)
    scale = 1.0 / math.sqrt(HEAD_DIM)
    qt = qt * inv * qg_ref[...].reshape(N_HEADS, HEAD_DIM, 1) * scale
    qt_ref[...] = qt.reshape(D, T).astype(BF16)
    vt_ref[...] = qvt[D:, :].astype(BF16)

    f = lax.dot_general(wf_ref[...], h, (((1,), (1,)), ((), ())),
                        preferred_element_type=F32) + bf_ref[...]
    logf = -(jnp.maximum(-f, 0.0) + jnp.log1p(jnp.exp(-jnp.abs(f))))
    tri = tri_ref[...]
    cs = sum(jnp.dot(piece.astype(BF16), tri, preferred_element_type=F32)
             for piece in _split3(logf))
    c = cs + run_ref[...][:, 0:1]
    c_ref[...] = c
    run_ref[...] = jnp.broadcast_to(c[:, T - 1:T], run_ref.shape)

    crow = jnp.concatenate([c, jnp.zeros((LANES - N_HEADS, T), F32)], axis=0).T
    hi, mid, lo = _split3(crow)
    lane = lax.broadcasted_iota(jnp.int32, (1, LANES), 1)
    ones = jnp.where((lane >= AUG_ONE) & (lane < AUG_ONE + 3), 1.0, 0.0)
    caug = hi + pltpu.roll(mid, AUG_MID, axis=1) + pltpu.roll(lo, AUG_LO, axis=1) + ones
    caug_ref[...] = caug.astype(BF16)


def _attn_proj(x, g, w_kz, w_qv_t, w_f_t, b_f, q_g_col, k_g_row):
    S, D = x.shape
    T = ROW_TILE
    head_of = jnp.arange(MXU_DIM) // HEAD_DIM
    gmat = jnp.where(head_of[:, None] == head_of[None, :], 1.0 / HEAD_DIM, 0.0).astype(BF16)
    tri = (jnp.arange(T)[:, None] <= jnp.arange(T)[None, :]).astype(BF16)
    const = lambda i: (0, 0)
    rows = lambda i: (i, 0)
    cols = lambda i: (0, i)
    return pl.pallas_call(
        _attn_proj_kernel,
        grid=(S // T,),
        in_specs=[
            pl.BlockSpec((T, D), rows),
            pl.BlockSpec((1, D), const),
            pl.BlockSpec((D, 2 * D), const),
            pl.BlockSpec((2 * D, D), const),
            pl.BlockSpec((N_HEADS, D), const),
            pl.BlockSpec((N_HEADS, 1), const),
            pl.BlockSpec((D, 1), const),
            pl.BlockSpec((1, D), const),
            pl.BlockSpec((MXU_DIM, MXU_DIM), const),
            pl.BlockSpec((T, T), const),
        ],
        out_specs=[
            pl.BlockSpec((D, T), cols),
            pl.BlockSpec((T, D), rows),
            pl.BlockSpec((D, T), cols),
            pl.BlockSpec((T, D), rows),
            pl.BlockSpec((N_HEADS, T), cols),
            pl.BlockSpec((T, LANES), rows),
        ],
        out_shape=[
            jax.ShapeDtypeStruct((D, S), BF16),
            jax.ShapeDtypeStruct((S, D), BF16),
            jax.ShapeDtypeStruct((D, S), BF16),
            jax.ShapeDtypeStruct((S, D), F32),
            jax.ShapeDtypeStruct((N_HEADS, S), F32),
            jax.ShapeDtypeStruct((S, LANES), BF16),
        ],
        scratch_shapes=[pltpu.VMEM((N_HEADS, LANES), F32)],
        compiler_params=pltpu.CompilerParams(
            dimension_semantics=("arbitrary",), vmem_limit_bytes=VMEM_LIMIT_BYTES),
        name="attn_proj",
    )(x, g, w_kz, w_qv_t, w_f_t, b_f, q_g_col, k_g_row, gmat, tri)


def _attn_kernel(q_ref, k_ref, v_ref, c_ref, z_ref, o_ref, m_ref, l_ref, acc_ref):
    i = pl.program_id(1)
    bq, bk = BLOCK_Q, BLOCK_K
    q2 = q_ref[...]
    lane = lax.broadcasted_iota(jnp.int32, (1, LANES), 1)
    zero = jnp.zeros_like(q2)
    q_heads = (jnp.where(lane < HEAD_DIM, q2, zero), jnp.where(lane >= HEAD_DIM, q2, zero))

    m_ref[...] = jnp.full_like(m_ref, NEG_BIG)
    l_ref[...] = jnp.zeros_like(l_ref)
    acc_ref[...] = jnp.zeros_like(acc_ref)

    def step(j, masked):
        start = pl.multiple_of(j * bk, bk)
        kb = k_ref[pl.ds(start, bk), :]
        vb = v_ref[pl.ds(start, bk), :]
        cb = c_ref[0, :, pl.ds(start, bk)]
        for hh in range(2):
            s = lax.dot_general(q_heads[hh], kb, (((1,), (1,)), ((), ())),
                                preferred_element_type=F32)
            s = s - cb[hh:hh + 1, :]
            if masked:
                row = lax.broadcasted_iota(jnp.int32, (bq, bk), 0)
                col = lax.broadcasted_iota(jnp.int32, (bq, bk), 1)
                s = jnp.where(col <= row, s, NEG_BIG)
            m_old = m_ref[hh]
            m_new = jnp.maximum(m_old, jnp.max(s, axis=-1, keepdims=True))
            alpha = jnp.exp(m_old - m_new)
            p = jnp.exp(s - m_new)
            l_ref[hh] = alpha * l_ref[hh] + jnp.sum(p, axis=-1, keepdims=True)
            acc_ref[hh] = alpha * acc_ref[hh] + jnp.dot(
                p.astype(BF16), vb, preferred_element_type=F32)
            m_ref[hh] = m_new

    def body(j, carry):
        step(j, masked=False)
        return carry

    lax.fori_loop(0, i, body, 0)
    step(i, masked=True)

    o0 = acc_ref[0] / l_ref[0]
    o1 = acc_ref[1] / l_ref[1]
    o = jnp.where(lane < HEAD_DIM, o0, o1)
    o_ref[...] = (o * _silu(z_ref[...])).astype(BF16)


def _attention(q, k, v, c, z):
    S, D = q.shape
    bq = BLOCK_Q
    c3 = c.reshape(N_PAIRS, 2, S)
    return pl.pallas_call(
        _attn_kernel,
        grid=(N_PAIRS, S // bq),
        in_specs=[
            pl.BlockSpec((bq, LANES), lambda p, i: (i, p)),
            pl.BlockSpec((S, LANES), lambda p, i: (0, p)),
            pl.BlockSpec((S, LANES), lambda p, i: (0, p)),
            pl.BlockSpec((1, 2, S), lambda p, i: (p, 0, 0)),
            pl.BlockSpec((bq, LANES), lambda p, i: (i, p)),
        ],
        out_specs=pl.BlockSpec((bq, LANES), lambda p, i: (i, p)),
        out_shape=jax.ShapeDtypeStruct((S, D), BF16),
        scratch_shapes=[
            pltpu.VMEM((2, bq, 1), F32),
            pltpu.VMEM((2, bq, 1), F32),
            pltpu.VMEM((2, bq, LANES), F32),
        ],
        compiler_params=pltpu.CompilerParams(
            dimension_semantics=("arbitrary", "arbitrary"), vmem_limit_bytes=VMEM_LIMIT_BYTES),
        name="fox_attention",
    )(q, k, v, c3, z)


def _attn_stream_kernel(qt_ref, k_ref, caug_ref, vt_ref, c_ref, z_ref, o_ref,
                        qa_ref, l_ref, acc_ref, st_ref):
    pair = pl.program_id(0)
    i = pl.program_id(1)
    bq, bk, ks = STREAM_BLOCK_Q, STREAM_BLOCK_K, STREAM_SUPER_K

    qt = qt_ref[...]
    r = lax.broadcasted_iota(jnp.int32, (LANES, 1), 0)
    zero = jnp.zeros_like(qt)
    top = jnp.concatenate([jnp.where(r < HEAD_DIM, qt, zero),
                           jnp.where(r >= HEAD_DIM, qt, zero)], axis=1)
    ci = c_ref[0]
    halves = []
    for hh in range(2):
        head = 2 * pair + hh
        c_hi, c_mid, c_lo = _split3(ci[hh:hh + 1, :])
        minus_cj = jnp.where((r == head) | (r == AUG_MID + head) | (r == AUG_LO + head), -1.0, 0.0)
        halves.append(minus_cj + jnp.where(r == AUG_ONE, c_hi, 0.0)
                      + jnp.where(r == AUG_ONE + 1, c_mid, 0.0)
                      + jnp.where(r == AUG_ONE + 2, c_lo, 0.0))
    qa_ref[...] = jnp.concatenate([top, jnp.concatenate(halves, axis=1).astype(BF16)], axis=0)
    l_ref[...] = jnp.zeros_like(l_ref)
    acc_ref[...] = jnp.zeros_like(acc_ref)

    def key_chunk(slot, start, nk, q_lo, masked):
        n = bq - q_lo
        start = pl.multiple_of(start, nk)
        ka = jnp.concatenate([k_ref[pl.ds(start, nk), :], caug_ref[pl.ds(start, nk), :]],
                             axis=1)
        qa = qa_ref[...] if q_lo == 0 else jnp.concatenate(
            [qa_ref[:, q_lo:bq], qa_ref[:, bq + q_lo:]], axis=1)
        st = jnp.dot(ka, qa, preferred_element_type=F32)
        if masked:
            key = start + lax.broadcasted_iota(jnp.int32, (nk, 2 * n), 0)
            col = lax.broadcasted_iota(jnp.int32, (nk, 2 * n), 1)
            qry = i * bq + q_lo + jnp.where(col < n, col, col - n)
            st = jnp.where(key <= qry, st, NEG_BIG)
        st_ref[slot, :nk, :2 * n] = st
        p = jnp.exp(st_ref[slot, :nk, :2 * n])
        lp = jnp.sum(p.reshape(nk // SUBLANES, SUBLANES, 2 * n), axis=0)
        pb = p.astype(BF16)
        for hh in range(2):
            l_ref[:, hh * bq + q_lo:(hh + 1) * bq] += lp[:, hh * n:(hh + 1) * n]
            acc_ref[hh, :, q_lo:] += jnp.dot(
                vt_ref[hh * HEAD_DIM:(hh + 1) * HEAD_DIM, pl.ds(start, nk)],
                pb[:, hh * n:(hh + 1) * n], preferred_element_type=F32)

    def run(chunks):
        for slot, chunk in enumerate(chunks):
            key_chunk(slot % st_ref.shape[0], *chunk)

    def full_block(t):
        return [(t * ks + u * bk, bk, 0, False) for u in range(ks // bk)]

    diagonal = [(i * ks + u * DIAG_BLOCK_K, DIAG_BLOCK_K, u * DIAG_BLOCK_K, True)
                for u in range(ks // DIAG_BLOCK_K)]
    assert bq == ks

    @pl.when(i % 2 == 0)
    def _():
        run(diagonal)

    @pl.when(i % 2 == 1)
    def _():
        run(diagonal + full_block(i - 1))

    def pair_body(g, carry):
        run(full_block(2 * g) + full_block(2 * g + 1))
        return carry

    lax.fori_loop(0, i // 2, pair_body, 0)

    l = jnp.sum(l_ref[...], axis=0, keepdims=True)
    ot = jnp.concatenate([acc_ref[0] / l[:, :bq], acc_ref[1] / l[:, bq:]], axis=0)
    o_ref[...] = (ot.T * _silu(z_ref[...])).astype(BF16)


def _attention_stream(qt, k, caug, vt, c, z):
    D, S = qt.shape
    bq = STREAM_BLOCK_Q
    c3 = c.reshape(N_PAIRS, 2, S)
    return pl.pallas_call(
        _attn_stream_kernel,
        grid=(N_PAIRS, S // bq),
        in_specs=[
            pl.BlockSpec((LANES, bq), lambda p, i: (p, i)),
            pl.BlockSpec((S, LANES), lambda p, i: (0, p)),
            pl.BlockSpec((S, LANES), lambda p, i: (0, 0)),
            pl.BlockSpec((LANES, S), lambda p, i: (p, 0)),
            pl.BlockSpec((1, 2, bq), lambda p, i: (p, 0, i)),
            pl.BlockSpec((bq, LANES), lambda p, i: (i, p)),
        ],
        out_specs=pl.BlockSpec((bq, LANES), lambda p, i: (i, p)),
        out_shape=jax.ShapeDtypeStruct((S, D), BF16),
        scratch_shapes=[
            pltpu.VMEM((2 * LANES, 2 * bq), BF16),
            pltpu.VMEM((SUBLANES, 2 * bq), F32),
            pltpu.VMEM((2, HEAD_DIM, bq), F32),
            pltpu.VMEM((2, STREAM_BLOCK_K, 2 * bq), F32),
        ],
        compiler_params=pltpu.CompilerParams(
            dimension_semantics=("arbitrary", "arbitrary"), vmem_limit_bytes=VMEM_LIMIT_BYTES),
        name="fox_attention_stream",
    )(qt, k, caug, vt, c3, z)


MXU_TILE = 256
S_SLOTS = 3
S_ADDR = tuple(s * (MXU_TILE * MXU_TILE // 1024) for s in range(S_SLOTS))
O_ADDR = S_SLOTS * (MXU_TILE * MXU_TILE // 1024)
ONES_ROWS = 16


def _attn_mxu_kernel(qt_ref, k_ref, caug_ref, vt_ref, c_ref, z_ref, o_ref, qa_ref):
    pair = pl.program_id(0)
    i = pl.program_id(1)
    bq, tq, kc = STREAM_BLOCK_Q, MXU_TILE, MXU_TILE

    qt = qt_ref[...]
    r = lax.broadcasted_iota(jnp.int32, (LANES, 1), 0)
    zero = jnp.zeros_like(qt)
    top = jnp.concatenate([jnp.where(r < HEAD_DIM, qt, zero),
                           jnp.where(r >= HEAD_DIM, qt, zero)], axis=1)
    ci = c_ref[0]
    halves = []
    for hh in range(2):
        head = 2 * pair + hh
        c_hi, c_mid, c_lo = _split3(ci[hh:hh + 1, :])
        minus_cj = jnp.where((r == head) | (r == AUG_MID + head) | (r == AUG_LO + head), -1.0, 0.0)
        halves.append(minus_cj + jnp.where(r == AUG_ONE, c_hi, 0.0)
                      + jnp.where(r == AUG_ONE + 1, c_mid, 0.0)
                      + jnp.where(r == AUG_ONE + 2, c_lo, 0.0))
    qa_ref[...] = jnp.concatenate([top, jnp.concatenate(halves, axis=1).astype(BF16)], axis=0)
    ones_rows = jnp.ones((ONES_ROWS, kc), BF16)

    def q_tile(t, carry):
        q0 = i * bq + t * tq
        n_real = q0 // kc + 1
        n_iter = (n_real + S_SLOTS - 1) // S_SLOTS

        def chunk_start(c):
            return pl.multiple_of(jnp.minimum(c, n_real - 1) * kc, kc)

        def stage_a(c, slot):
            start = chunk_start(c)
            ka = jnp.concatenate([k_ref[pl.ds(start, kc), :], caug_ref[pl.ds(start, kc), :]],
                                 axis=1)
            for hh in range(2):
                q_tile_aug = qa_ref[:, pl.ds(pl.multiple_of(hh * bq + t * tq, tq), tq)]
                pltpu.matmul_push_rhs(q_tile_aug, staging_register=0, mxu_index=hh)
                pltpu.matmul_acc_lhs(S_ADDR[slot], ka, mxu_index=hh, load_staged_rhs=0)

        def stage_b(c, slot, masked):
            start = chunk_start(c)
            for hh in range(2):
                s = pltpu.matmul_pop(S_ADDR[slot], (kc, tq), F32, mxu_index=hh)
                if masked:
                    key = c * kc + lax.broadcasted_iota(jnp.int32, (kc, tq), 0)
                    qry = q0 + lax.broadcasted_iota(jnp.int32, (kc, tq), 1)
                    s = jnp.where(key <= qry, s, NEG_BIG)
                p = jnp.exp(s).astype(BF16)
                pltpu.matmul_push_rhs(p, staging_register=1, mxu_index=hh)
                vta = jnp.concatenate(
                    [vt_ref[hh * HEAD_DIM:(hh + 1) * HEAD_DIM, pl.ds(start, kc)], ones_rows],
                    axis=0)
                pltpu.matmul_acc_lhs(O_ADDR, vta, mxu_index=hh, load_staged_rhs=1)

        stage_a(0, 0)
        stage_a(1, 1)

        def body(g, carry):
            for s in range(S_SLOTS):
                c = g * S_SLOTS + s
                stage_a(c + 2, (s + 2) % S_SLOTS)
                stage_b(c, s, False)
            return carry

        lax.fori_loop(0, n_iter - 1, body, 0)
        c_last = (n_iter - 1) * S_SLOTS
        stage_a(c_last + 2, 2)
        for s in range(S_SLOTS):
            stage_b(c_last + s, s, True)

        outs = []
        for hh in range(2):
            o = pltpu.matmul_pop(O_ADDR, (HEAD_DIM + ONES_ROWS, tq), F32, mxu_index=hh)
            outs.append(o[:HEAD_DIM, :] / o[HEAD_DIM:HEAD_DIM + 1, :])
        rows = pl.ds(pl.multiple_of(t * tq, tq), tq)
        ot = jnp.concatenate(outs, axis=0)
        o_ref[rows, :] = (ot.T * _silu(z_ref[rows, :])).astype(BF16)
        return carry

    lax.fori_loop(0, bq // tq, q_tile, 0)


def _attention_mxu(qt, k, caug, vt, c, z):
    D, S = qt.shape
    bq = STREAM_BLOCK_Q
    c3 = c.reshape(N_PAIRS, 2, S)
    return pl.pallas_call(
        _attn_mxu_kernel,
        grid=(N_PAIRS, S // bq),
        in_specs=[
            pl.BlockSpec((LANES, bq), lambda p, i: (p, i)),
            pl.BlockSpec((S, LANES), lambda p, i: (0, p)),
            pl.BlockSpec((S, LANES), lambda p, i: (0, 0)),
            pl.BlockSpec((LANES, S), lambda p, i: (p, 0)),
            pl.BlockSpec((1, 2, bq), lambda p, i: (p, 0, i)),
            pl.BlockSpec((bq, LANES), lambda p, i: (i, p)),
        ],
        out_specs=pl.BlockSpec((bq, LANES), lambda p, i: (i, p)),
        out_shape=jax.ShapeDtypeStruct((S, D), BF16),
        scratch_shapes=[pltpu.VMEM((2 * LANES, 2 * bq), BF16)],
        compiler_params=pltpu.CompilerParams(
            dimension_semantics=("arbitrary", "arbitrary"), vmem_limit_bytes=VMEM_LIMIT_BYTES),
        name="fox_attention_mxu",
    )(qt, k, caug, vt, c3, z)


def _out_proj_kernel(x_ref, a_ref, w_ref, o_ref):
    o_ref[...] = x_ref[...] + jnp.dot(a_ref[...], w_ref[...], preferred_element_type=F32)


def _out_proj(x, a, w_out):
    S, D = x.shape
    T = ROW_TILE
    return pl.pallas_call(
        _out_proj_kernel,
        grid=(S // T,),
        in_specs=[
            pl.BlockSpec((T, D), lambda i: (i, 0)),
            pl.BlockSpec((T, D), lambda i: (i, 0)),
            pl.BlockSpec((D, D), lambda i: (0, 0)),
        ],
        out_specs=pl.BlockSpec((T, D), lambda i: (i, 0)),
        out_shape=jax.ShapeDtypeStruct((S, D), F32),
        compiler_params=pltpu.CompilerParams(
            dimension_semantics=("arbitrary",), vmem_limit_bytes=VMEM_LIMIT_BYTES),
        name="attn_out_proj",
    )(x, a, w_out)


def kernel(x, conv_norm_g, conv_w_in, conv_w, conv_w_out, attn_norm_g, attn_w_in,
           attn_b_f, attn_q_norm_g, attn_k_norm_g, attn_w_out):
    B, S, D = x.shape
    assert (B, D) == (1, D_MODEL) and S % BLOCK_Q == 0 and S % ROW_TILE == 0
    assert conv_w_in.shape[0] == 1 and attn_w_in.shape[0] == 1
    x2 = x.reshape(S, D)

    x2 = _conv_layer(x2, conv_norm_g[0].reshape(1, D), conv_w_in[0].astype(BF16),
                     conv_w[0], conv_w_out[0].astype(BF16))

    w_in = attn_w_in[0]
    w_kz = jnp.concatenate([w_in[:, D:2 * D], w_in[:, 3 * D:4 * D]], axis=1).astype(BF16)
    w_qv_t = jnp.concatenate([w_in[:, :D], w_in[:, 2 * D:3 * D]], axis=1).T.astype(BF16)
    w_f_t = w_in[:, 4 * D:].T.astype(BF16)
    qt, k, vt, z, c, caug = _attn_proj(
        x2, attn_norm_g[0].reshape(1, D), w_kz, w_qv_t, w_f_t,
        attn_b_f[0].reshape(N_HEADS, 1),
        jnp.tile(attn_q_norm_g[0], N_HEADS).reshape(D, 1),
        jnp.tile(attn_k_norm_g[0], N_HEADS).reshape(1, D))
    logit_bound = (math.sqrt(HEAD_DIM) * jnp.max(jnp.abs(attn_q_norm_g[0]))
                   * jnp.max(jnp.abs(attn_k_norm_g[0])))
    a = lax.cond(logit_bound <= LOGIT_BOUND_MAX,
                 lambda: _attention_mxu(qt, k, caug, vt, c, z),
                 lambda: _attention(qt.T, k, vt.T, c, z))
    out = _out_proj(x2, a, attn_w_out[0].astype(BF16))
    return out.reshape(B, S, D)
```

```python
import functools
import math

import jax
import jax.numpy as jnp
from jax import lax
from jax.experimental import pallas as pl
from jax.experimental.pallas import tpu as pltpu

D_MODEL = 1024
HEAD_DIM = 64
N_HEADS = D_MODEL // HEAD_DIM
N_PAIRS = N_HEADS // 2
CONV_WIDTH = 3
RMS_EPS = 1e-6

LANES = 128
SUBLANES = 8
MXU_DIM = 256
VMEM_LIMIT_BYTES = 56 * 1024 * 1024

ROW_TILE = 512
BLOCK_Q = 512
BLOCK_K = 512
NEG_BIG = -1e30
STREAM_BLOCK_Q = 1024
STREAM_SUPER_K = 1024
STREAM_BLOCK_K = 512
DIAG_BLOCK_K = 256
AUG_MID = N_HEADS
AUG_LO = 2 * N_HEADS
AUG_ONE = 3 * N_HEADS
LOGIT_BOUND_MAX = 60.0

BF16 = jnp.bfloat16
F32 = jnp.float32


def _rmsnorm_rows(x, g):
    inv = lax.rsqrt(jnp.mean(x * x, axis=-1, keepdims=True) + RMS_EPS)
    return x * inv * g


def _silu(z):
    return z * jax.nn.sigmoid(z)


def _conv_layer_kernel(x_ref, g_ref, w_in_ref, cw_ref, w_out_ref, o_ref, tail_ref):
    i = pl.program_id(0)
    D = D_MODEL
    T = x_ref.shape[0]

    @pl.when(i == 0)
    def _():
        tail_ref[...] = jnp.zeros_like(tail_ref)

    x = x_ref[...]
    h = _rmsnorm_rows(x, g_ref[...]).astype(BF16)
    proj = jnp.dot(h, w_in_ref[...], preferred_element_type=F32)
    b_g = proj[:, 0 * D:1 * D]
    c_g = proj[:, 1 * D:2 * D]
    xin = proj[:, 2 * D:3 * D]
    z = proj[:, 3 * D:4 * D]
    u = c_g * xin

    tail = tail_ref[...]
    prev1 = tail[SUBLANES - 1:SUBLANES, :]
    prev2 = tail[SUBLANES - 2:SUBLANES - 1, :]
    row = lax.broadcasted_iota(jnp.int32, (T, 1), 0)
    u1 = jnp.where(row == 0, prev1, pltpu.roll(u, 1, axis=0))
    u2 = jnp.where(row == 0, prev2, jnp.where(row == 1, prev1, pltpu.roll(u, 2, axis=0)))
    tail_ref[...] = u[T - SUBLANES:, :]

    cw = cw_ref[...]
    y = cw[0:1, :] * u2 + cw[1:2, :] * u1 + cw[2:3, :] * u
    y = b_g * y * _silu(z)
    o_ref[...] = x + jnp.dot(y.astype(BF16), w_out_ref[...], preferred_element_type=F32)


def _conv_layer(x, g, w_in, cw, w_out):
    S, D = x.shape
    const = lambda i: (0, 0)
    return pl.pallas_call(
        _conv_layer_kernel,
        grid=(S // ROW_TILE,),
        in_specs=[
            pl.BlockSpec((ROW_TILE, D), lambda i: (i, 0)),
            pl.BlockSpec((1, D), const),
            pl.BlockSpec((D, 4 * D), const),
            pl.BlockSpec((CONV_WIDTH, D), const),
            pl.BlockSpec((D, D), const),
        ],
        out_specs=pl.BlockSpec((ROW_TILE, D), lambda i: (i, 0)),
        out_shape=jax.ShapeDtypeStruct((S, D), F32),
        scratch_shapes=[pltpu.VMEM((SUBLANES, D), F32)],
        compiler_params=pltpu.CompilerParams(
            dimension_semantics=("arbitrary",), vmem_limit_bytes=VMEM_LIMIT_BYTES),
        name="conv_layer",
    )(x, g, w_in, cw, w_out)


def _split3(x):
    hi = x.astype(BF16).astype(F32)
    r = x - hi
    mid = r.astype(BF16).astype(F32)
    lo = (r - mid).astype(BF16).astype(F32)
    return hi, mid, lo


def _head_rmsnorm(t, g, gmat):
    sq = t * t
    hi = sq.astype(BF16)
    lo = (sq - hi.astype(F32)).astype(BF16)
    parts = []
    for c in range(D_MODEL // MXU_DIM):
        sl = slice(c * MXU_DIM, (c + 1) * MXU_DIM)
        parts.append(jnp.dot(hi[:, sl], gmat, preferred_element_type=F32)
                     + jnp.dot(lo[:, sl], gmat, preferred_element_type=F32))
    ms = jnp.concatenate(parts, axis=-1)
    return t * lax.rsqrt(ms + RMS_EPS) * g


def _attn_proj_kernel(x_ref, g_ref, wkz_ref, wqv_ref, wf_ref, bf_ref, qg_ref, kg_ref,
                      gmat_ref, tri_ref,
                      qt_ref, k_ref, vt_ref, z_ref, c_ref, caug_ref, run_ref):
    i = pl.program_id(0)
    D = D_MODEL
    T = x_ref.shape[0]

    @pl.when(i == 0)
    def _():
        run_ref[...] = jnp.zeros_like(run_ref)

    h = _rmsnorm_rows(x_ref[...], g_ref[...]).astype(BF16)
    kz = jnp.dot(h, wkz_ref[...], preferred_element_type=F32)
    k_ref[...] = _head_rmsnorm(kz[:, :D], kg_ref[...], gmat_ref[...]).astype(BF16)
    z_ref[...] = kz[:, D:]

    qvt = lax.dot_general(wqv_ref[...], h, (((1,), (1,)), ((), ())),
                          preferred_element_type=F32)
    qt = qvt[:D, :].reshape(N_HEADS, HEAD_DIM, T)
    inv = lax.rsqrt(jnp.mean(qt * qt, axis=1, keepdims=True) + RMS_EPS)
    scale = 1.0 / math.sqrt(HEAD_DIM)
    qt = qt * inv * qg_ref[...].reshape(N_HEADS, HEAD_DIM, 1) * scale
    qt_ref[...] = qt.reshape(D, T).astype(BF16)
    vt_ref[...] = qvt[D:, :].astype(BF16)

    f = lax.dot_general(wf_ref[...], h, (((1,), (1,)), ((), ())),
                        preferred_element_type=F32) + bf_ref[...]
    logf = -(jnp.maximum(-f, 0.0) + jnp.log1p(jnp.exp(-jnp.abs(f))))
    tri = tri_ref[...]
    cs = sum(jnp.dot(piece.astype(BF16), tri, preferred_element_type=F32)
             for piece in _split3(logf))
    c = cs + run_ref[...][:, 0:1]
    c_ref[...] = c
    run_ref[...] = jnp.broadcast_to(c[:, T - 1:T], run_ref.shape)

    crow = jnp.concatenate([c, jnp.zeros((LANES - N_HEADS, T), F32)], axis=0).T
    hi, mid, lo = _split3(crow)
    lane = lax.broadcasted_iota(jnp.int32, (1, LANES), 1)
    ones = jnp.where((lane >= AUG_ONE) & (lane < AUG_ONE + 3), 1.0, 0.0)
    caug = hi + pltpu.roll(mid, AUG_MID, axis=1) + pltpu.roll(lo, AUG_LO, axis=1) + ones
    caug_ref[...] = caug.astype(BF16)


def _attn_proj(x, g, w_kz, w_qv_t, w_f_t, b_f, q_g_col, k_g_row):
    S, D = x.shape
    T = ROW_TILE
    head_of = jnp.arange(MXU_DIM) // HEAD_DIM
    gmat = jnp.where(head_of[:, None] == head_of[None, :], 1.0 / HEAD_DIM, 0.0).astype(BF16)
    tri = (jnp.arange(T)[:, None] <= jnp.arange(T)[None, :]).astype(BF16)
    const = lambda i: (0, 0)
    rows = lambda i: (i, 0)
    cols = lambda i: (0, i)
    return pl.pallas_call(
        _attn_proj_kernel,
        grid=(S // T,),
        in_specs=[
            pl.BlockSpec((T, D), rows),
            pl.BlockSpec((1, D), const),
            pl.BlockSpec((D, 2 * D), const),
            pl.BlockSpec((2 * D, D), const),
            pl.BlockSpec((N_HEADS, D), const),
            pl.BlockSpec((N_HEADS, 1), const),
            pl.BlockSpec((D, 1), const),
            pl.BlockSpec((1, D), const),
            pl.BlockSpec((MXU_DIM, MXU_DIM), const),
            pl.BlockSpec((T, T), const),
        ],
        out_specs=[
            pl.BlockSpec((D, T), cols),
            pl.BlockSpec((T, D), rows),
            pl.BlockSpec((D, T), cols),
            pl.BlockSpec((T, D), rows),
            pl.BlockSpec((N_HEADS, T), cols),
            pl.BlockSpec((T, LANES), rows),
        ],
        out_shape=[
            jax.ShapeDtypeStruct((D, S), BF16),
            jax.ShapeDtypeStruct((S, D), BF16),
            jax.ShapeDtypeStruct((D, S), BF16),
            jax.ShapeDtypeStruct((S, D), F32),
            jax.ShapeDtypeStruct((N_HEADS, S), F32),
            jax.ShapeDtypeStruct((S, LANES), BF16),
        ],
        scratch_shapes=[pltpu.VMEM((N_HEADS, LANES), F32)],
        compiler_params=pltpu.CompilerParams(
            dimension_semantics=("arbitrary",), vmem_limit_bytes=VMEM_LIMIT_BYTES),
        name="attn_proj",
    )(x, g, w_kz, w_qv_t, w_f_t, b_f, q_g_col, k_g_row, gmat, tri)


def _attn_kernel(q_ref, k_ref, v_ref, c_ref, z_ref, o_ref, m_ref, l_ref, acc_ref):
    i = pl.program_id(1)
    bq, bk = BLOCK_Q, BLOCK_K
    q2 = q_ref[...]
    lane = lax.broadcasted_iota(jnp.int32, (1, LANES), 1)
    zero = jnp.zeros_like(q2)
    q_heads = (jnp.where(lane < HEAD_DIM, q2, zero), jnp.where(lane >= HEAD_DIM, q2, zero))

    m_ref[...] = jnp.full_like(m_ref, NEG_BIG)
    l_ref[...] = jnp.zeros_like(l_ref)
    acc_ref[...] = jnp.zeros_like(acc_ref)

    def step(j, masked):
        start = pl.multiple_of(j * bk, bk)
        kb = k_ref[pl.ds(start, bk), :]
        vb = v_ref[pl.ds(start, bk), :]
        cb = c_ref[0, :, pl.ds(start, bk)]
        for hh in range(2):
            s = lax.dot_general(q_heads[hh], kb, (((1,), (1,)), ((), ())),
                                preferred_element_type=F32)
            s = s - cb[hh:hh + 1, :]
            if masked:
                row = lax.broadcasted_iota(jnp.int32, (bq, bk), 0)
                col = lax.broadcasted_iota(jnp.int32, (bq, bk), 1)
                s = jnp.where(col <= row, s, NEG_BIG)
            m_old = m_ref[hh]
            m_new = jnp.maximum(m_old, jnp.max(s, axis=-1, keepdims=True))
            alpha = jnp.exp(m_old - m_new)
            p = jnp.exp(s - m_new)
            l_ref[hh] = alpha * l_ref[hh] + jnp.sum(p, axis=-1, keepdims=True)
            acc_ref[hh] = alpha * acc_ref[hh] + jnp.dot(
                p.astype(BF16), vb, preferred_element_type=F32)
            m_ref[hh] = m_new

    def body(j, carry):
        step(j, masked=False)
        return carry

    lax.fori_loop(0, i, body, 0)
    step(i, masked=True)

    o0 = acc_ref[0] / l_ref[0]
    o1 = acc_ref[1] / l_ref[1]
    o = jnp.where(lane < HEAD_DIM, o0, o1)
    o_ref[...] = (o * _silu(z_ref[...])).astype(BF16)


def _attention(q, k, v, c, z):
    S, D = q.shape
    bq = BLOCK_Q
    c3 = c.reshape(N_PAIRS, 2, S)
    return pl.pallas_call(
        _attn_kernel,
        grid=(N_PAIRS, S // bq),
        in_specs=[
            pl.BlockSpec((bq, LANES), lambda p, i: (i, p)),
            pl.BlockSpec((S, LANES), lambda p, i: (0, p)),
            pl.BlockSpec((S, LANES), lambda p, i: (0, p)),
            pl.BlockSpec((1, 2, S), lambda p, i: (p, 0, 0)),
            pl.BlockSpec((bq, LANES), lambda p, i: (i, p)),
        ],
        out_specs=pl.BlockSpec((bq, LANES), lambda p, i: (i, p)),
        out_shape=jax.ShapeDtypeStruct((S, D), BF16),
        scratch_shapes=[
            pltpu.VMEM((2, bq, 1), F32),
            pltpu.VMEM((2, bq, 1), F32),
            pltpu.VMEM((2, bq, LANES), F32),
        ],
        compiler_params=pltpu.CompilerParams(
            dimension_semantics=("arbitrary", "arbitrary"), vmem_limit_bytes=VMEM_LIMIT_BYTES),
        name="fox_attention",
    )(q, k, v, c3, z)


def _attn_stream_kernel(qt_ref, k_ref, caug_ref, vt_ref, c_ref, z_ref, o_ref,
                        qa_ref, l_ref, acc_ref, st_ref):
    pair = pl.program_id(0)
    i = pl.program_id(1)
    bq, bk, ks = STREAM_BLOCK_Q, STREAM_BLOCK_K, STREAM_SUPER_K

    qt = qt_ref[...]
    r = lax.broadcasted_iota(jnp.int32, (LANES, 1), 0)
    zero = jnp.zeros_like(qt)
    top = jnp.concatenate([jnp.where(r < HEAD_DIM, qt, zero),
                           jnp.where(r >= HEAD_DIM, qt, zero)], axis=1)
    ci = c_ref[0]
    halves = []
    for hh in range(2):
        head = 2 * pair + hh
        c_hi, c_mid, c_lo = _split3(ci[hh:hh + 1, :])
        minus_cj = jnp.where((r == head) | (r == AUG_MID + head) | (r == AUG_LO + head), -1.0, 0.0)
        halves.append(minus_cj + jnp.where(r == AUG_ONE, c_hi, 0.0)
                      + jnp.where(r == AUG_ONE + 1, c_mid, 0.0)
                      + jnp.where(r == AUG_ONE + 2, c_lo, 0.0))
    qa_ref[...] = jnp.concatenate([top, jnp.concatenate(halves, axis=1).astype(BF16)], axis=0)
    l_ref[...] = jnp.zeros_like(l_ref)
    acc_ref[...] = jnp.zeros_like(acc_ref)

    def key_chunk(slot, start, nk, q_lo, masked):
        n = bq - q_lo
        start = pl.multiple_of(start, nk)
        ka = jnp.concatenate([k_ref[pl.ds(start, nk), :], caug_ref[pl.ds(start, nk), :]],
                             axis=1)
        qa = qa_ref[...] if q_lo == 0 else jnp.concatenate(
            [qa_ref[:, q_lo:bq], qa_ref[:, bq + q_lo:]], axis=1)
        st = jnp.dot(ka, qa, preferred_element_type=F32)
        if masked:
            key = start + lax.broadcasted_iota(jnp.int32, (nk, 2 * n), 0)
            col = lax.broadcasted_iota(jnp.int32, (nk, 2 * n), 1)
            qry = i * bq + q_lo + jnp.where(col < n, col, col - n)
            st = jnp.where(key <= qry, st, NEG_BIG)
        st_ref[slot, :nk, :2 * n] = st
        p = jnp.exp(st_ref[slot, :nk, :2 * n])
        lp = jnp.sum(p.reshape(nk // SUBLANES, SUBLANES, 2 * n), axis=0)
        pb = p.astype(BF16)
        for hh in range(2):
            l_ref[:, hh * bq + q_lo:(hh + 1) * bq] += lp[:, hh * n:(hh + 1) * n]
            acc_ref[hh, :, q_lo:] += jnp.dot(
                vt_ref[hh * HEAD_DIM:(hh + 1) * HEAD_DIM, pl.ds(start, nk)],
                pb[:, hh * n:(hh + 1) * n], preferred_element_type=F32)

    def run(chunks):
        for slot, chunk in enumerate(chunks):
            key_chunk(slot % st_ref.shape[0], *chunk)

    def full_block(t):
        return [(t * ks + u * bk, bk, 0, False) for u in range(ks // bk)]

    diagonal = [(i * ks + u * DIAG_BLOCK_K, DIAG_BLOCK_K, u * DIAG_BLOCK_K, True)
                for u in range(ks // DIAG_BLOCK_K)]
    assert bq == ks

    @pl.when(i % 2 == 0)
    def _():
        run(diagonal)

    @pl.when(i % 2 == 1)
    def _():
        run(diagonal + full_block(i - 1))

    def pair_body(g, carry):
        run(full_block(2 * g) + full_block(2 * g + 1))
        return carry

    lax.fori_loop(0, i // 2, pair_body, 0)

    l = jnp.sum(l_ref[...], axis=0, keepdims=True)
    ot = jnp.concatenate([acc_ref[0] / l[:, :bq], acc_ref[1] / l[:, bq:]], axis=0)
    o_ref[...] = (ot.T * _silu(z_ref[...])).astype(BF16)


def _attention_stream(qt, k, caug, vt, c, z):
    D, S = qt.shape
    bq = STREAM_BLOCK_Q
    c3 = c.reshape(N_PAIRS, 2, S)
    return pl.pallas_call(
        _attn_stream_kernel,
        grid=(N_PAIRS, S // bq),
        in_specs=[
            pl.BlockSpec((LANES, bq), lambda p, i: (p, i)),
            pl.BlockSpec((S, LANES), lambda p, i: (0, p)),
            pl.BlockSpec((S, LANES), lambda p, i: (0, 0)),
            pl.BlockSpec((LANES, S), lambda p, i: (p, 0)),
            pl.BlockSpec((1, 2, bq), lambda p, i: (p, 0, i)),
            pl.BlockSpec((bq, LANES), lambda p, i: (i, p)),
        ],
        out_specs=pl.BlockSpec((bq, LANES), lambda p, i: (i, p)),
        out_shape=jax.ShapeDtypeStruct((S, D), BF16),
        scratch_shapes=[
            pltpu.VMEM((2 * LANES, 2 * bq), BF16),
            pltpu.VMEM((SUBLANES, 2 * bq), F32),
            pltpu.VMEM((2, HEAD_DIM, bq), F32),
            pltpu.VMEM((2, STREAM_BLOCK_K, 2 * bq), F32),
        ],
        compiler_params=pltpu.CompilerParams(
            dimension_semantics=("arbitrary", "arbitrary"), vmem_limit_bytes=VMEM_LIMIT_BYTES),
        name="fox_attention_stream",
    )(qt, k, caug, vt, c3, z)


MXU_TILE = 256
S_SLOTS = 3
S_ADDR = tuple(s * (MXU_TILE * MXU_TILE // 1024) for s in range(S_SLOTS))
O_ADDR = S_SLOTS * (MXU_TILE * MXU_TILE // 1024)
ONES_ROWS = 16


def _attn_mxu_kernel(qt_ref, k_ref, caug_ref, vt_ref, c_ref, z_ref, o_ref, qa_ref):
    pair = pl.program_id(0)
    i = pl.program_id(1)
    bq, tq, kc = STREAM_BLOCK_Q, MXU_TILE, MXU_TILE

    qt = qt_ref[...]
    r = lax.broadcasted_iota(jnp.int32, (LANES, 1), 0)
    zero = jnp.zeros_like(qt)
    top = jnp.concatenate([jnp.where(r < HEAD_DIM, qt, zero),
                           jnp.where(r >= HEAD_DIM, qt, zero)], axis=1)
    ci = c_ref[0]
    halves = []
    for hh in range(2):
        head = 2 * pair + hh
        c_hi, c_mid, c_lo = _split3(ci[hh:hh + 1, :])
        minus_cj = jnp.where((r == head) | (r == AUG_MID + head) | (r == AUG_LO + head), -1.0, 0.0)
        halves.append(minus_cj + jnp.where(r == AUG_ONE, c_hi, 0.0)
                      + jnp.where(r == AUG_ONE + 1, c_mid, 0.0)
                      + jnp.where(r == AUG_ONE + 2, c_lo, 0.0))
    qa_ref[...] = jnp.concatenate([top, jnp.concatenate(halves, axis=1).astype(BF16)], axis=0)
    ones_rows = jnp.ones((ONES_ROWS, kc), BF16)

    def q_tile(t, carry):
        q0 = i * bq + t * tq
        n_real = q0 // kc + 1
        n_iter = (n_real + S_SLOTS - 1) // S_SLOTS

        def chunk_start(c):
            return pl.multiple_of(jnp.minimum(c, n_real - 1) * kc, kc)

        def stage_q():
            for hh in range(2):
                pltpu.matmul_push_rhs(qa_ref[:, pl.ds(hh * bq + t * tq, tq)],
                                      staging_register=0, mxu_index=hh)

        def stage_a(c, slot, restage=True):
            start = chunk_start(c)
            ka = jnp.concatenate([k_ref[pl.ds(start, kc), :], caug_ref[pl.ds(start, kc), :]],
                                 axis=1)
            for hh in range(2):
                pltpu.matmul_acc_lhs(S_ADDR[slot], ka, mxu_index=hh, load_staged_rhs=0)
            if restage:
                stage_q()

        def stage_b(c, slot, masked):
            start = chunk_start(c)
            for hh in range(2):
                s = pltpu.matmul_pop(S_ADDR[slot], (kc, tq), F32, mxu_index=hh)
                if masked:
                    key = c * kc + lax.broadcasted_iota(jnp.int32, (kc, tq), 0)
                    qry = q0 + lax.broadcasted_iota(jnp.int32, (kc, tq), 1)
                    s = jnp.where(key <= qry, s, NEG_BIG)
                p = jnp.exp(s).astype(BF16)
                pltpu.matmul_push_rhs(p, staging_register=1, mxu_index=hh)
                vta = jnp.concatenate(
                    [vt_ref[hh * HEAD_DIM:(hh + 1) * HEAD_DIM, pl.ds(start, kc)], ones_rows],
                    axis=0)
                pltpu.matmul_acc_lhs(O_ADDR, vta, mxu_index=hh, load_staged_rhs=1)

        stage_q()
        stage_a(0, 0)
        stage_a(1, 1)

        def body(g, carry):
            for s in range(S_SLOTS):
                c = g * S_SLOTS + s
                stage_a(c + 2, (s + 2) % S_SLOTS)
                stage_b(c, s, False)
            return carry

        lax.fori_loop(0, n_iter - 1, body, 0)
        c_last = (n_iter - 1) * S_SLOTS
        stage_a(c_last + 2, 2, restage=False)
        for s in range(S_SLOTS):
            stage_b(c_last + s, s, True)

        outs = []
        for hh in range(2):
            o = pltpu.matmul_pop(O_ADDR, (HEAD_DIM + ONES_ROWS, tq), F32, mxu_index=hh)
            outs.append(o[:HEAD_DIM, :] / o[HEAD_DIM:HEAD_DIM + 1, :])
        rows = pl.ds(t * tq, tq)
        ot = jnp.concatenate(outs, axis=0)
        o_ref[rows, :] = (ot.T * _silu(z_ref[rows, :])).astype(BF16)
        return carry

    for t in range(bq // tq):
        q_tile(t, 0)


def _attention_mxu(qt, k, caug, vt, c, z):
    D, S = qt.shape
    bq = STREAM_BLOCK_Q
    c3 = c.reshape(N_PAIRS, 2, S)
    return pl.pallas_call(
        _attn_mxu_kernel,
        grid=(N_PAIRS, S // bq),
        in_specs=[
            pl.BlockSpec((LANES, bq), lambda p, i: (p, i)),
            pl.BlockSpec((S, LANES), lambda p, i: (0, p)),
            pl.BlockSpec((S, LANES), lambda p, i: (0, 0)),
            pl.BlockSpec((LANES, S), lambda p, i: (p, 0)),
            pl.BlockSpec((1, 2, bq), lambda p, i: (p, 0, i)),
            pl.BlockSpec((bq, LANES), lambda p, i: (i, p)),
        ],
        out_specs=pl.BlockSpec((bq, LANES), lambda p, i: (i, p)),
        out_shape=jax.ShapeDtypeStruct((S, D), BF16),
        scratch_shapes=[pltpu.VMEM((2 * LANES, 2 * bq), BF16)],
        compiler_params=pltpu.CompilerParams(
            dimension_semantics=("arbitrary", "arbitrary"), vmem_limit_bytes=VMEM_LIMIT_BYTES),
        name="fox_attention_mxu",
    )(qt, k, caug, vt, c3, z)


def _out_proj_kernel(x_ref, a_ref, w_ref, o_ref):
    o_ref[...] = x_ref[...] + jnp.dot(a_ref[...], w_ref[...], preferred_element_type=F32)


def _out_proj(x, a, w_out):
    S, D = x.shape
    T = ROW_TILE
    return pl.pallas_call(
        _out_proj_kernel,
        grid=(S // T,),
        in_specs=[
            pl.BlockSpec((T, D), lambda i: (i, 0)),
            pl.BlockSpec((T, D), lambda i: (i, 0)),
            pl.BlockSpec((D, D), lambda i: (0, 0)),
        ],
        out_specs=pl.BlockSpec((T, D), lambda i: (i, 0)),
        out_shape=jax.ShapeDtypeStruct((S, D), F32),
        compiler_params=pltpu.CompilerParams(
            dimension_semantics=("arbitrary",), vmem_limit_bytes=VMEM_LIMIT_BYTES),
        name="attn_out_proj",
    )(x, a, w_out)


def kernel(x, conv_norm_g, conv_w_in, conv_w, conv_w_out, attn_norm_g, attn_w_in,
           attn_b_f, attn_q_norm_g, attn_k_norm_g, attn_w_out):
    B, S, D = x.shape
    assert (B, D) == (1, D_MODEL) and S % BLOCK_Q == 0 and S % ROW_TILE == 0
    assert conv_w_in.shape[0] == 1 and attn_w_in.shape[0] == 1
    x2 = x.reshape(S, D)

    x2 = _conv_layer(x2, conv_norm_g[0].reshape(1, D), conv_w_in[0].astype(BF16),
                     conv_w[0], conv_w_out[0].astype(BF16))

    w_in = attn_w_in[0]
    w_kz = jnp.concatenate([w_in[:, D:2 * D], w_in[:, 3 * D:4 * D]], axis=1).astype(BF16)
    w_qv_t = jnp.concatenate([w_in[:, :D], w_in[:, 2 * D:3 * D]], axis=1).T.astype(BF16)
    w_f_t = w_in[:, 4 * D:].T.astype(BF16)
    qt, k, vt, z, c, caug = _attn_proj(
        x2, attn_norm_g[0].reshape(1, D), w_kz, w_qv_t, w_f_t,
        attn_b_f[0].reshape(N_HEADS, 1),
        jnp.tile(attn_q_norm_g[0], N_HEADS).reshape(D, 1),
        jnp.tile(attn_k_norm_g[0], N_HEADS).reshape(1, D))
    logit_bound = (math.sqrt(HEAD_DIM) * jnp.max(jnp.abs(attn_q_norm_g[0]))
                   * jnp.max(jnp.abs(attn_k_norm_g[0])))
    a = lax.cond(logit_bound <= LOGIT_BOUND_MAX,
                 lambda: _attention_mxu(qt, k, caug, vt, c, z),
                 lambda: _attention(qt.T, k, vt.T, c, z))
    out = _out_proj(x2, a, attn_w_out[0].astype(BF16))
    return out.reshape(B, S, D)
```

```python
import math

import jax
import jax.numpy as jnp
from jax import lax
from jax.experimental import pallas as pl
from jax.experimental.pallas import tpu as pltpu

D_MODEL = 1024
HEAD_DIM = 64
N_HEADS = D_MODEL // HEAD_DIM
N_PAIRS = N_HEADS // 2
CONV_WIDTH = 3
RMS_EPS = 1e-6

LANES = 128
SUBLANES = 8
MXU_DIM = 256
VMEM_LIMIT_BYTES = 56 * 1024 * 1024

ROW_TILE = 512
OUT_ROW_TILE = 1024
BLOCK_Q = 512
BLOCK_K = 512
NEG_BIG = -1e30
STREAM_BLOCK_Q = 1024
AUG_MID = N_HEADS
AUG_LO = 2 * N_HEADS
AUG_ONE = 3 * N_HEADS
LOGIT_BOUND_MAX = 60.0

BF16 = jnp.bfloat16
F32 = jnp.float32


def _rmsnorm_rows(x, g):
    inv = lax.rsqrt(jnp.mean(x * x, axis=-1, keepdims=True) + RMS_EPS)
    return x * inv * g


def _silu(z):
    return z * jax.nn.sigmoid(z)


def _conv_layer_kernel(x_ref, g_ref, w_in_ref, cw_ref, w_out_ref, o_ref, tail_ref):
    i = pl.program_id(0)
    D = D_MODEL
    T = x_ref.shape[0]

    @pl.when(i == 0)
    def _():
        tail_ref[...] = jnp.zeros_like(tail_ref)

    x = x_ref[...]
    h = _rmsnorm_rows(x, g_ref[...]).astype(BF16)
    proj = jnp.dot(h, w_in_ref[...], preferred_element_type=F32)
    b_g = proj[:, 0 * D:1 * D]
    c_g = proj[:, 1 * D:2 * D]
    xin = proj[:, 2 * D:3 * D]
    z = proj[:, 3 * D:4 * D]
    u = c_g * xin

    tail = tail_ref[...]
    prev1 = tail[SUBLANES - 1:SUBLANES, :]
    prev2 = tail[SUBLANES - 2:SUBLANES - 1, :]
    row = lax.broadcasted_iota(jnp.int32, (T, 1), 0)
    u1 = jnp.where(row == 0, prev1, pltpu.roll(u, 1, axis=0))
    u2 = jnp.where(row == 0, prev2, jnp.where(row == 1, prev1, pltpu.roll(u, 2, axis=0)))
    tail_ref[...] = u[T - SUBLANES:, :]

    cw = cw_ref[...]
    y = cw[0:1, :] * u2 + cw[1:2, :] * u1 + cw[2:3, :] * u
    y = b_g * y * _silu(z)
    o_ref[...] = x + jnp.dot(y.astype(BF16), w_out_ref[...], preferred_element_type=F32)


def _conv_layer(x, g, w_in, cw, w_out):
    S, D = x.shape
    const = lambda i: (0, 0)
    return pl.pallas_call(
        _conv_layer_kernel,
        grid=(S // ROW_TILE,),
        in_specs=[
            pl.BlockSpec((ROW_TILE, D), lambda i: (i, 0)),
            pl.BlockSpec((1, D), const),
            pl.BlockSpec((D, 4 * D), const),
            pl.BlockSpec((CONV_WIDTH, D), const),
            pl.BlockSpec((D, D), const),
        ],
        out_specs=pl.BlockSpec((ROW_TILE, D), lambda i: (i, 0)),
        out_shape=jax.ShapeDtypeStruct((S, D), F32),
        scratch_shapes=[pltpu.VMEM((SUBLANES, D), F32)],
        compiler_params=pltpu.CompilerParams(
            dimension_semantics=("arbitrary",), vmem_limit_bytes=VMEM_LIMIT_BYTES),
        name="conv_layer",
    )(x, g, w_in, cw, w_out)


def _split3(x):
    hi = x.astype(BF16).astype(F32)
    r = x - hi
    mid = r.astype(BF16).astype(F32)
    lo = (r - mid).astype(BF16).astype(F32)
    return hi, mid, lo


def _head_rmsnorm(t, g, gmat):
    sq = t * t
    hi = sq.astype(BF16)
    lo = (sq - hi.astype(F32)).astype(BF16)
    parts = []
    for c in range(D_MODEL // MXU_DIM):
        sl = slice(c * MXU_DIM, (c + 1) * MXU_DIM)
        parts.append(jnp.dot(hi[:, sl], gmat, preferred_element_type=F32)
                     + jnp.dot(lo[:, sl], gmat, preferred_element_type=F32))
    ms = jnp.concatenate(parts, axis=-1)
    return t * lax.rsqrt(ms + RMS_EPS) * g


def _attn_proj_kernel(x_ref, g_ref, wkz_ref, wqv_ref, wf_ref, bf_ref, qg_ref, kg_ref,
                      gmat_ref, tri_ref,
                      qt_ref, k_ref, vt_ref, z_ref, c_ref, caug_ref, run_ref):
    i = pl.program_id(0)
    D = D_MODEL
    T = x_ref.shape[0]

    @pl.when(i == 0)
    def _():
        run_ref[...] = jnp.zeros_like(run_ref)

    h = _rmsnorm_rows(x_ref[...], g_ref[...]).astype(BF16)
    kz = jnp.dot(h, wkz_ref[...], preferred_element_type=F32)
    k_ref[...] = _head_rmsnorm(kz[:, :D], kg_ref[...], gmat_ref[...]).astype(BF16)
    z_ref[...] = kz[:, D:]

    qvt = lax.dot_general(wqv_ref[...], h, (((1,), (1,)), ((), ())),
                          preferred_element_type=F32)
    qt = qvt[:D, :].reshape(N_HEADS, HEAD_DIM, T)
    inv = lax.rsqrt(jnp.mean(qt * qt, axis=1, keepdims=True) + RMS_EPS)
    scale = 1.0 / math.sqrt(HEAD_DIM)
    qt = qt * inv * qg_ref[...].reshape(N_HEADS, HEAD_DIM, 1) * scale
    qt_ref[...] = qt.reshape(D, T).astype(BF16)
    vt_ref[...] = qvt[D:, :].astype(BF16)

    f = lax.dot_general(wf_ref[...], h, (((1,), (1,)), ((), ())),
                        preferred_element_type=F32) + bf_ref[...]
    logf = -(jnp.maximum(-f, 0.0) + jnp.log1p(jnp.exp(-jnp.abs(f))))
    tri = tri_ref[...]
    cs = sum(jnp.dot(piece.astype(BF16), tri, preferred_element_type=F32)
             for piece in _split3(logf))
    c = cs + run_ref[...][:, 0:1]
    c_ref[...] = c
    run_ref[...] = jnp.broadcast_to(c[:, T - 1:T], run_ref.shape)

    crow = jnp.concatenate([c, jnp.zeros((LANES - N_HEADS, T), F32)], axis=0).T
    hi, mid, lo = _split3(crow)
    lane = lax.broadcasted_iota(jnp.int32, (1, LANES), 1)
    ones = jnp.where((lane >= AUG_ONE) & (lane < AUG_ONE + 3), 1.0, 0.0)
    caug = hi + pltpu.roll(mid, AUG_MID, axis=1) + pltpu.roll(lo, AUG_LO, axis=1) + ones
    caug_ref[...] = caug.astype(BF16)


def _attn_proj(x, g, w_kz, w_qv_t, w_f_t, b_f, q_g_col, k_g_row):
    S, D = x.shape
    T = ROW_TILE
    head_of = jnp.arange(MXU_DIM) // HEAD_DIM
    gmat = jnp.where(head_of[:, None] == head_of[None, :], 1.0 / HEAD_DIM, 0.0).astype(BF16)
    tri = (jnp.arange(T)[:, None] <= jnp.arange(T)[None, :]).astype(BF16)
    const = lambda i: (0, 0)
    rows = lambda i: (i, 0)
    cols = lambda i: (0, i)
    return pl.pallas_call(
        _attn_proj_kernel,
        grid=(S // T,),
        in_specs=[
            pl.BlockSpec((T, D), rows),
            pl.BlockSpec((1, D), const),
            pl.BlockSpec((D, 2 * D), const),
            pl.BlockSpec((2 * D, D), const),
            pl.BlockSpec((N_HEADS, D), const),
            pl.BlockSpec((N_HEADS, 1), const),
            pl.BlockSpec((D, 1), const),
            pl.BlockSpec((1, D), const),
            pl.BlockSpec((MXU_DIM, MXU_DIM), const),
            pl.BlockSpec((T, T), const),
        ],
        out_specs=[
            pl.BlockSpec((D, T), cols),
            pl.BlockSpec((T, D), rows),
            pl.BlockSpec((D, T), cols),
            pl.BlockSpec((T, D), rows),
            pl.BlockSpec((N_HEADS, T), cols),
            pl.BlockSpec((T, LANES), rows),
        ],
        out_shape=[
            jax.ShapeDtypeStruct((D, S), BF16),
            jax.ShapeDtypeStruct((S, D), BF16),
            jax.ShapeDtypeStruct((D, S), BF16),
            jax.ShapeDtypeStruct((S, D), F32),
            jax.ShapeDtypeStruct((N_HEADS, S), F32),
            jax.ShapeDtypeStruct((S, LANES), BF16),
        ],
        scratch_shapes=[pltpu.VMEM((N_HEADS, LANES), F32)],
        compiler_params=pltpu.CompilerParams(
            dimension_semantics=("arbitrary",), vmem_limit_bytes=VMEM_LIMIT_BYTES),
        name="attn_proj",
    )(x, g, w_kz, w_qv_t, w_f_t, b_f, q_g_col, k_g_row, gmat, tri)


def _attn_kernel(q_ref, k_ref, v_ref, c_ref, z_ref, o_ref, m_ref, l_ref, acc_ref):
    i = pl.program_id(1)
    bq, bk = BLOCK_Q, BLOCK_K
    q2 = q_ref[...]
    lane = lax.broadcasted_iota(jnp.int32, (1, LANES), 1)
    zero = jnp.zeros_like(q2)
    q_heads = (jnp.where(lane < HEAD_DIM, q2, zero), jnp.where(lane >= HEAD_DIM, q2, zero))

    m_ref[...] = jnp.full_like(m_ref, NEG_BIG)
    l_ref[...] = jnp.zeros_like(l_ref)
    acc_ref[...] = jnp.zeros_like(acc_ref)

    def step(j, masked):
        start = pl.multiple_of(j * bk, bk)
        kb = k_ref[pl.ds(start, bk), :]
        vb = v_ref[pl.ds(start, bk), :]
        cb = c_ref[0, :, pl.ds(start, bk)]
        for hh in range(2):
            s = lax.dot_general(q_heads[hh], kb, (((1,), (1,)), ((), ())),
                                preferred_element_type=F32)
            s = s - cb[hh:hh + 1, :]
            if masked:
                row = lax.broadcasted_iota(jnp.int32, (bq, bk), 0)
                col = lax.broadcasted_iota(jnp.int32, (bq, bk), 1)
                s = jnp.where(col <= row, s, NEG_BIG)
            m_old = m_ref[hh]
            m_new = jnp.maximum(m_old, jnp.max(s, axis=-1, keepdims=True))
            alpha = jnp.exp(m_old - m_new)
            p = jnp.exp(s - m_new)
            l_ref[hh] = alpha * l_ref[hh] + jnp.sum(p, axis=-1, keepdims=True)
            acc_ref[hh] = alpha * acc_ref[hh] + jnp.dot(
                p.astype(BF16), vb, preferred_element_type=F32)
            m_ref[hh] = m_new

    def body(j, carry):
        step(j, masked=False)
        return carry

    lax.fori_loop(0, i, body, 0)
    step(i, masked=True)

    o0 = acc_ref[0] / l_ref[0]
    o1 = acc_ref[1] / l_ref[1]
    o = jnp.where(lane < HEAD_DIM, o0, o1)
    o_ref[...] = (o * _silu(z_ref[...])).astype(BF16)


def _attention(q, k, v, c, z):
    S, D = q.shape
    bq = BLOCK_Q
    c3 = c.reshape(N_PAIRS, 2, S)
    return pl.pallas_call(
        _attn_kernel,
        grid=(N_PAIRS, S // bq),
        in_specs=[
            pl.BlockSpec((bq, LANES), lambda p, i: (i, p)),
            pl.BlockSpec((S, LANES), lambda p, i: (0, p)),
            pl.BlockSpec((S, LANES), lambda p, i: (0, p)),
            pl.BlockSpec((1, 2, S), lambda p, i: (p, 0, 0)),
            pl.BlockSpec((bq, LANES), lambda p, i: (i, p)),
        ],
        out_specs=pl.BlockSpec((bq, LANES), lambda p, i: (i, p)),
        out_shape=jax.ShapeDtypeStruct((S, D), BF16),
        scratch_shapes=[
            pltpu.VMEM((2, bq, 1), F32),
            pltpu.VMEM((2, bq, 1), F32),
            pltpu.VMEM((2, bq, LANES), F32),
        ],
        compiler_params=pltpu.CompilerParams(
            dimension_semantics=("arbitrary", "arbitrary"), vmem_limit_bytes=VMEM_LIMIT_BYTES),
        name="fox_attention",
    )(q, k, v, c3, z)


MXU_TILE = 256
S_SLOTS = 3
S_TILE_ENTRIES = MXU_TILE * MXU_TILE // (SUBLANES * LANES)
S_ADDR = tuple(s * S_TILE_ENTRIES for s in range(S_SLOTS))
O_ADDR = S_SLOTS * S_TILE_ENTRIES
ONES_ROWS = 16


def _attn_mxu_kernel(qt_ref, k_ref, caug_ref, vt_ref, c_ref, z_ref, o_ref, qa_ref, ot_ref):
    pair = pl.program_id(0)
    i = pl.program_id(1)
    bq, tq, kc = STREAM_BLOCK_Q, MXU_TILE, MXU_TILE

    qt = qt_ref[...]
    r = lax.broadcasted_iota(jnp.int32, (LANES, 1), 0)
    zero = jnp.zeros_like(qt)
    top = jnp.concatenate([jnp.where(r < HEAD_DIM, qt, zero),
                           jnp.where(r >= HEAD_DIM, qt, zero)], axis=1)
    ci = c_ref[0]
    halves = []
    for hh in range(2):
        head = 2 * pair + hh
        c_hi, c_mid, c_lo = _split3(ci[hh:hh + 1, :])
        minus_cj = jnp.where((r == head) | (r == AUG_MID + head) | (r == AUG_LO + head), -1.0, 0.0)
        halves.append(minus_cj + jnp.where(r == AUG_ONE, c_hi, 0.0)
                      + jnp.where(r == AUG_ONE + 1, c_mid, 0.0)
                      + jnp.where(r == AUG_ONE + 2, c_lo, 0.0))
    qa_ref[...] = jnp.concatenate([top, jnp.concatenate(halves, axis=1).astype(BF16)], axis=0)
    ones_rows = jnp.ones((ONES_ROWS, kc), BF16)

    def q_tile(t, carry):
        q0 = i * bq + t * tq
        n_real = q0 // kc + 1
        n_iter = (n_real + S_SLOTS - 1) // S_SLOTS

        def chunk_start(c):
            return pl.multiple_of(jnp.minimum(c, n_real - 1) * kc, kc)

        def stage_q():
            for hh in range(2):
                pltpu.matmul_push_rhs(qa_ref[:, pl.ds(hh * bq + t * tq, tq)],
                                      staging_register=0, mxu_index=hh)

        def stage_a(c, slot, restage=True):
            start = chunk_start(c)
            ka = jnp.concatenate([k_ref[pl.ds(start, kc), :], caug_ref[pl.ds(start, kc), :]],
                                 axis=1)
            for hh in range(2):
                pltpu.matmul_acc_lhs(S_ADDR[slot], ka, mxu_index=hh, load_staged_rhs=0)
            if restage:
                stage_q()

        def stage_b(c, slot, masked):
            start = chunk_start(c)
            for hh in range(2):
                s = pltpu.matmul_pop(S_ADDR[slot], (kc, tq), F32, mxu_index=hh)
                if masked:
                    key = c * kc + lax.broadcasted_iota(jnp.int32, (kc, tq), 0)
                    qry = q0 + lax.broadcasted_iota(jnp.int32, (kc, tq), 1)
                    s = jnp.where(key <= qry, s, NEG_BIG)
                p = jnp.exp(s).astype(BF16)
                pltpu.matmul_push_rhs(p, staging_register=1, mxu_index=hh)
                vta = jnp.concatenate(
                    [vt_ref[hh * HEAD_DIM:(hh + 1) * HEAD_DIM, pl.ds(start, kc)], ones_rows],
                    axis=0)
                pltpu.matmul_acc_lhs(O_ADDR, vta, mxu_index=hh, load_staged_rhs=1)

        stage_q()
        stage_a(0, 0)
        stage_a(1, 1)

        def body(g, carry):
            for s in range(S_SLOTS):
                c = g * S_SLOTS + s
                stage_a(c + 2, (s + 2) % S_SLOTS)
                stage_b(c, s, False)
            return carry

        lax.fori_loop(0, n_iter - 1, body, 0)
        c_last = (n_iter - 1) * S_SLOTS
        stage_a(c_last + 2, 2, restage=False)
        for s in range(S_SLOTS):
            stage_b(c_last + s, s, True)

        outs = []
        for hh in range(2):
            o = pltpu.matmul_pop(O_ADDR, (HEAD_DIM + ONES_ROWS, tq), F32, mxu_index=hh)
            outs.append(o[:HEAD_DIM, :] / o[HEAD_DIM:HEAD_DIM + 1, :])
        ot_ref[:, t * tq:(t + 1) * tq] = jnp.concatenate(outs, axis=0)
        return carry

    for t in range(bq // tq):
        q_tile(t, 0)
    o_ref[...] = (ot_ref[...].T * _silu(z_ref[...])).astype(BF16)


def _attention_mxu(qt, k, caug, vt, c, z):
    D, S = qt.shape
    bq = STREAM_BLOCK_Q
    c3 = c.reshape(N_PAIRS, 2, S)
    return pl.pallas_call(
        _attn_mxu_kernel,
        grid=(N_PAIRS, S // bq),
        in_specs=[
            pl.BlockSpec((LANES, bq), lambda p, i: (p, i)),
            pl.BlockSpec((S, LANES), lambda p, i: (0, p)),
            pl.BlockSpec((S, LANES), lambda p, i: (0, 0)),
            pl.BlockSpec((LANES, S), lambda p, i: (p, 0)),
            pl.BlockSpec((1, 2, bq), lambda p, i: (p, 0, i)),
            pl.BlockSpec((bq, LANES), lambda p, i: (i, p)),
        ],
        out_specs=pl.BlockSpec((bq, LANES), lambda p, i: (i, p)),
        out_shape=jax.ShapeDtypeStruct((S, D), BF16),
        scratch_shapes=[pltpu.VMEM((2 * LANES, 2 * bq), BF16), pltpu.VMEM((LANES, bq), F32)],
        compiler_params=pltpu.CompilerParams(
            dimension_semantics=("arbitrary", "arbitrary"), vmem_limit_bytes=VMEM_LIMIT_BYTES),
        name="fox_attention_mxu",
    )(qt, k, caug, vt, c3, z)


def _out_proj_kernel(x_ref, a_ref, w_ref, o_ref):
    o_ref[...] = x_ref[...] + jnp.dot(a_ref[...], w_ref[...], preferred_element_type=F32)


def _out_proj(x, a, w_out):
    S, D = x.shape
    T = OUT_ROW_TILE
    return pl.pallas_call(
        _out_proj_kernel,
        grid=(S // T,),
        in_specs=[
            pl.BlockSpec((T, D), lambda i: (i, 0)),
            pl.BlockSpec((T, D), lambda i: (i, 0)),
            pl.BlockSpec((D, D), lambda i: (0, 0)),
        ],
        out_specs=pl.BlockSpec((T, D), lambda i: (i, 0)),
        out_shape=jax.ShapeDtypeStruct((S, D), F32),
        compiler_params=pltpu.CompilerParams(
            dimension_semantics=("arbitrary",), vmem_limit_bytes=VMEM_LIMIT_BYTES),
        name="attn_out_proj",
    )(x, a, w_out)


def kernel(x, conv_norm_g, conv_w_in, conv_w, conv_w_out, attn_norm_g, attn_w_in,
           attn_b_f, attn_q_norm_g, attn_k_norm_g, attn_w_out):
    B, S, D = x.shape
    assert (B, D) == (1, D_MODEL)
    assert all(S % t == 0 for t in (BLOCK_Q, STREAM_BLOCK_Q, ROW_TILE, OUT_ROW_TILE))
    assert conv_w_in.shape[0] == 1 and attn_w_in.shape[0] == 1
    x2 = x.reshape(S, D)

    x2 = _conv_layer(x2, conv_norm_g[0].reshape(1, D), conv_w_in[0].astype(BF16),
                     conv_w[0], conv_w_out[0].astype(BF16))

    w_in = attn_w_in[0]
    w_kz = jnp.concatenate([w_in[:, D:2 * D], w_in[:, 3 * D:4 * D]], axis=1).astype(BF16)
    w_qv_t = jnp.concatenate([w_in[:, :D], w_in[:, 2 * D:3 * D]], axis=1).T.astype(BF16)
    w_f_t = w_in[:, 4 * D:].T.astype(BF16)
    qt, k, vt, z, c, caug = _attn_proj(
        x2, attn_norm_g[0].reshape(1, D), w_kz, w_qv_t, w_f_t,
        attn_b_f[0].reshape(N_HEADS, 1),
        jnp.tile(attn_q_norm_g[0], N_HEADS).reshape(D, 1),
        jnp.tile(attn_k_norm_g[0], N_HEADS).reshape(1, D))
    logit_bound = (math.sqrt(HEAD_DIM) * jnp.max(jnp.abs(attn_q_norm_g[0]))
                   * jnp.max(jnp.abs(attn_k_norm_g[0])))
    a = lax.cond(logit_bound <= LOGIT_BOUND_MAX,
                 lambda: _attention_mxu(qt, k, caug, vt, c, z),
                 lambda: _attention(qt.T, k, vt.T, c, z))
    out = _out_proj(x2, a, attn_w_out[0].astype(BF16))
    return out.reshape(B, S, D)
```

```python
import math

import jax
import jax.numpy as jnp
from jax import lax
from jax.experimental import pallas as pl
from jax.experimental.pallas import tpu as pltpu

D_MODEL = 1024
HEAD_DIM = 64
N_HEADS = D_MODEL // HEAD_DIM
N_PAIRS = N_HEADS // 2
CONV_WIDTH = 3
RMS_EPS = 1e-6

LANES = 128
SUBLANES = 8
MXU_DIM = 256
VMEM_LIMIT_BYTES = 56 * 1024 * 1024

ROW_TILE = 512
OUT_ROW_TILE = 1024
BLOCK_Q = 512
BLOCK_K = 512
NEG_BIG = -1e30
STREAM_BLOCK_Q = 2048
AUG_MID = N_HEADS
AUG_LO = 2 * N_HEADS
AUG_ONE = 3 * N_HEADS
LOGIT_BOUND_MAX = 60.0

BF16 = jnp.bfloat16
F32 = jnp.float32


def _rmsnorm_rows(x, g):
    inv = lax.rsqrt(jnp.mean(x * x, axis=-1, keepdims=True) + RMS_EPS)
    return x * inv * g


def _silu(z):
    return z * jax.nn.sigmoid(z)


def _conv_layer_kernel(x_ref, g_ref, w_in_ref, cw_ref, w_out_ref, o_ref, tail_ref):
    i = pl.program_id(0)
    D = D_MODEL
    T = x_ref.shape[0]

    @pl.when(i == 0)
    def _():
        tail_ref[...] = jnp.zeros_like(tail_ref)

    x = x_ref[...]
    h = _rmsnorm_rows(x, g_ref[...])
    proj = jnp.dot(h, w_in_ref[...], preferred_element_type=F32)
    b_g = proj[:, 0 * D:1 * D]
    c_g = proj[:, 1 * D:2 * D]
    xin = proj[:, 2 * D:3 * D]
    z = proj[:, 3 * D:4 * D]
    u = c_g * xin

    tail = tail_ref[...]
    prev1 = tail[SUBLANES - 1:SUBLANES, :]
    prev2 = tail[SUBLANES - 2:SUBLANES - 1, :]
    row = lax.broadcasted_iota(jnp.int32, (T, 1), 0)
    u1 = jnp.where(row == 0, prev1, pltpu.roll(u, 1, axis=0))
    u2 = jnp.where(row == 0, prev2, jnp.where(row == 1, prev1, pltpu.roll(u, 2, axis=0)))
    tail_ref[...] = u[T - SUBLANES:, :]

    cw = cw_ref[...]
    y = cw[0:1, :] * u2 + cw[1:2, :] * u1 + cw[2:3, :] * u
    y = b_g * y * _silu(z)
    o_ref[...] = x + jnp.dot(y, w_out_ref[...], preferred_element_type=F32)


def _conv_layer(x, g, w_in, cw, w_out):
    S, D = x.shape
    const = lambda i: (0, 0)
    return pl.pallas_call(
        _conv_layer_kernel,
        grid=(S // ROW_TILE,),
        in_specs=[
            pl.BlockSpec((ROW_TILE, D), lambda i: (i, 0)),
            pl.BlockSpec((1, D), const),
            pl.BlockSpec((D, 4 * D), const, pipeline_mode=pl.Buffered(1)),
            pl.BlockSpec((CONV_WIDTH, D), const),
            pl.BlockSpec((D, D), const, pipeline_mode=pl.Buffered(1)),
        ],
        out_specs=pl.BlockSpec((ROW_TILE, D), lambda i: (i, 0)),
        out_shape=jax.ShapeDtypeStruct((S, D), F32),
        scratch_shapes=[pltpu.VMEM((SUBLANES, D), F32)],
        compiler_params=pltpu.CompilerParams(
            dimension_semantics=("arbitrary",), vmem_limit_bytes=VMEM_LIMIT_BYTES),
        name="conv_layer",
    )(x, g, w_in, cw, w_out)


def _split3(x):
    hi = x.astype(BF16).astype(F32)
    r = x - hi
    mid = r.astype(BF16).astype(F32)
    lo = (r - mid).astype(BF16).astype(F32)
    return hi, mid, lo


def _head_rmsnorm(t, g, gmat):
    sq = t * t
    hi = sq.astype(BF16)
    lo = (sq - hi.astype(F32)).astype(BF16)
    parts = []
    for c in range(D_MODEL // MXU_DIM):
        sl = slice(c * MXU_DIM, (c + 1) * MXU_DIM)
        parts.append(jnp.dot(hi[:, sl], gmat, preferred_element_type=F32)
                     + jnp.dot(lo[:, sl], gmat, preferred_element_type=F32))
    ms = jnp.concatenate(parts, axis=-1)
    return t * lax.rsqrt(ms + RMS_EPS) * g


def _attn_proj_kernel(x_ref, g_ref, wkz_ref, wqv_ref, wf_ref, bf_ref, qg_ref, kg_ref,
                      gmat_ref, tri_ref,
                      qt_ref, k_ref, vt_ref, z_ref, c_ref, caug_ref, run_ref):
    i = pl.program_id(0)
    D = D_MODEL
    T = x_ref.shape[0]

    @pl.when(i == 0)
    def _():
        run_ref[...] = jnp.zeros_like(run_ref)

    h = _rmsnorm_rows(x_ref[...], g_ref[...])
    kz = jnp.dot(h, wkz_ref[...], preferred_element_type=F32)
    k_ref[...] = _head_rmsnorm(kz[:, :D], kg_ref[...], gmat_ref[...]).astype(BF16)
    z_ref[...] = kz[:, D:]

    qvt = lax.dot_general(wqv_ref[...], h, (((1,), (1,)), ((), ())),
                          preferred_element_type=F32)
    qt = qvt[:D, :].reshape(N_HEADS, HEAD_DIM, T)
    inv = lax.rsqrt(jnp.mean(qt * qt, axis=1, keepdims=True) + RMS_EPS)
    scale = 1.0 / math.sqrt(HEAD_DIM)
    qt = qt * inv * qg_ref[...].reshape(N_HEADS, HEAD_DIM, 1) * scale
    qt_ref[...] = qt.reshape(D, T).astype(BF16)
    vt_ref[...] = qvt[D:, :].astype(BF16)

    f = lax.dot_general(wf_ref[...], h, (((1,), (1,)), ((), ())),
                        preferred_element_type=F32) + bf_ref[...]
    logf = -(jnp.maximum(-f, 0.0) + jnp.log1p(jnp.exp(-jnp.abs(f))))
    tri = tri_ref[...]
    cs = sum(jnp.dot(piece.astype(BF16), tri, preferred_element_type=F32)
             for piece in _split3(logf))
    c = cs + run_ref[...][:, 0:1]
    c_ref[...] = c
    run_ref[...] = jnp.broadcast_to(c[:, T - 1:T], run_ref.shape)

    crow = jnp.concatenate([c, jnp.zeros((LANES - N_HEADS, T), F32)], axis=0).T
    hi, mid, lo = _split3(crow)
    lane = lax.broadcasted_iota(jnp.int32, (1, LANES), 1)
    ones = jnp.where((lane >= AUG_ONE) & (lane < AUG_ONE + 3), 1.0, 0.0)
    caug = hi + pltpu.roll(mid, AUG_MID, axis=1) + pltpu.roll(lo, AUG_LO, axis=1) + ones
    caug_ref[...] = caug.astype(BF16)


def _attn_proj(x, g, w_kz, w_qv_t, w_f_t, b_f, q_g_col, k_g_row):
    S, D = x.shape
    T = ROW_TILE
    head_of = jnp.arange(MXU_DIM) // HEAD_DIM
    gmat = jnp.where(head_of[:, None] == head_of[None, :], 1.0 / HEAD_DIM, 0.0).astype(BF16)
    tri = (jnp.arange(T)[:, None] <= jnp.arange(T)[None, :]).astype(BF16)
    const = lambda i: (0, 0)
    rows = lambda i: (i, 0)
    cols = lambda i: (0, i)
    return pl.pallas_call(
        _attn_proj_kernel,
        grid=(S // T,),
        in_specs=[
            pl.BlockSpec((T, D), rows),
            pl.BlockSpec((1, D), const),
            pl.BlockSpec((D, 2 * D), const, pipeline_mode=pl.Buffered(1)),
            pl.BlockSpec((2 * D, D), const, pipeline_mode=pl.Buffered(1)),
            pl.BlockSpec((N_HEADS, D), const),
            pl.BlockSpec((N_HEADS, 1), const),
            pl.BlockSpec((D, 1), const),
            pl.BlockSpec((1, D), const),
            pl.BlockSpec((MXU_DIM, MXU_DIM), const),
            pl.BlockSpec((T, T), const),
        ],
        out_specs=[
            pl.BlockSpec((D, T), cols),
            pl.BlockSpec((T, D), rows),
            pl.BlockSpec((D, T), cols),
            pl.BlockSpec((T, D), rows),
            pl.BlockSpec((N_HEADS, T), cols),
            pl.BlockSpec((T, LANES), rows),
        ],
        out_shape=[
            jax.ShapeDtypeStruct((D, S), BF16),
            jax.ShapeDtypeStruct((S, D), BF16),
            jax.ShapeDtypeStruct((D, S), BF16),
            jax.ShapeDtypeStruct((S, D), F32),
            jax.ShapeDtypeStruct((N_HEADS, S), F32),
            jax.ShapeDtypeStruct((S, LANES), BF16),
        ],
        scratch_shapes=[pltpu.VMEM((N_HEADS, LANES), F32)],
        compiler_params=pltpu.CompilerParams(
            dimension_semantics=("arbitrary",), vmem_limit_bytes=VMEM_LIMIT_BYTES),
        name="attn_proj",
    )(x, g, w_kz, w_qv_t, w_f_t, b_f, q_g_col, k_g_row, gmat, tri)


def _attn_kernel(q_ref, k_ref, v_ref, c_ref, z_ref, o_ref, m_ref, l_ref, acc_ref):
    i = pl.program_id(1)
    bq, bk = BLOCK_Q, BLOCK_K
    q2 = q_ref[...]
    lane = lax.broadcasted_iota(jnp.int32, (1, LANES), 1)
    zero = jnp.zeros_like(q2)
    q_heads = (jnp.where(lane < HEAD_DIM, q2, zero), jnp.where(lane >= HEAD_DIM, q2, zero))

    m_ref[...] = jnp.full_like(m_ref, NEG_BIG)
    l_ref[...] = jnp.zeros_like(l_ref)
    acc_ref[...] = jnp.zeros_like(acc_ref)

    def step(j, masked):
        start = pl.multiple_of(j * bk, bk)
        kb = k_ref[pl.ds(start, bk), :]
        vb = v_ref[pl.ds(start, bk), :]
        cb = c_ref[0, :, pl.ds(start, bk)]
        for hh in range(2):
            s = lax.dot_general(q_heads[hh], kb, (((1,), (1,)), ((), ())),
                                preferred_element_type=F32)
            s = s - cb[hh:hh + 1, :]
            if masked:
                row = lax.broadcasted_iota(jnp.int32, (bq, bk), 0)
                col = lax.broadcasted_iota(jnp.int32, (bq, bk), 1)
                s = jnp.where(col <= row, s, NEG_BIG)
            m_old = m_ref[hh]
            m_new = jnp.maximum(m_old, jnp.max(s, axis=-1, keepdims=True))
            alpha = jnp.exp(m_old - m_new)
            p = jnp.exp(s - m_new)
            l_ref[hh] = alpha * l_ref[hh] + jnp.sum(p, axis=-1, keepdims=True)
            acc_ref[hh] = alpha * acc_ref[hh] + jnp.dot(
                p.astype(BF16), vb, preferred_element_type=F32)
            m_ref[hh] = m_new

    def body(j, carry):
        step(j, masked=False)
        return carry

    lax.fori_loop(0, i, body, 0)
    step(i, masked=True)

    o0 = acc_ref[0] / l_ref[0]
    o1 = acc_ref[1] / l_ref[1]
    o = jnp.where(lane < HEAD_DIM, o0, o1)
    o_ref[...] = (o * _silu(z_ref[...])).astype(BF16)


def _attention(q, k, v, c, z):
    S, D = q.shape
    bq = BLOCK_Q
    c3 = c.reshape(N_PAIRS, 2, S)
    return pl.pallas_call(
        _attn_kernel,
        grid=(N_PAIRS, S // bq),
        in_specs=[
            pl.BlockSpec((bq, LANES), lambda p, i: (i, p)),
            pl.BlockSpec((S, LANES), lambda p, i: (0, p)),
            pl.BlockSpec((S, LANES), lambda p, i: (0, p)),
            pl.BlockSpec((1, 2, S), lambda p, i: (p, 0, 0)),
            pl.BlockSpec((bq, LANES), lambda p, i: (i, p)),
        ],
        out_specs=pl.BlockSpec((bq, LANES), lambda p, i: (i, p)),
        out_shape=jax.ShapeDtypeStruct((S, D), BF16),
        scratch_shapes=[
            pltpu.VMEM((2, bq, 1), F32),
            pltpu.VMEM((2, bq, 1), F32),
            pltpu.VMEM((2, bq, LANES), F32),
        ],
        compiler_params=pltpu.CompilerParams(
            dimension_semantics=("arbitrary", "arbitrary"), vmem_limit_bytes=VMEM_LIMIT_BYTES),
        name="fox_attention",
    )(q, k, v, c3, z)


MXU_TILE = 256
S_SLOTS = 3
S_TILE_ENTRIES = MXU_TILE * MXU_TILE // (SUBLANES * LANES)
S_ADDR = tuple(s * S_TILE_ENTRIES for s in range(S_SLOTS))
O_ADDR = S_SLOTS * S_TILE_ENTRIES
ONES_ROWS = 16


def _attn_mxu_kernel(qt_ref, k_ref, caug_ref, vt_ref, c_ref, z_ref, o_ref, qa_ref, ot_ref):
    pair = pl.program_id(0)
    i = pl.program_id(1)
    bq, tq, kc = STREAM_BLOCK_Q, MXU_TILE, MXU_TILE

    qt = qt_ref[...]
    r = lax.broadcasted_iota(jnp.int32, (LANES, 1), 0)
    zero = jnp.zeros_like(qt)
    top = jnp.concatenate([jnp.where(r < HEAD_DIM, qt, zero),
                           jnp.where(r >= HEAD_DIM, qt, zero)], axis=1)
    ci = c_ref[0]
    halves = []
    for hh in range(2):
        head = 2 * pair + hh
        c_hi, c_mid, c_lo = _split3(ci[hh:hh + 1, :])
        minus_cj = jnp.where((r == head) | (r == AUG_MID + head) | (r == AUG_LO + head), -1.0, 0.0)
        halves.append(minus_cj + jnp.where(r == AUG_ONE, c_hi, 0.0)
                      + jnp.where(r == AUG_ONE + 1, c_mid, 0.0)
                      + jnp.where(r == AUG_ONE + 2, c_lo, 0.0))
    qa_ref[...] = jnp.concatenate([top, jnp.concatenate(halves, axis=1).astype(BF16)], axis=0)
    ones_rows = jnp.ones((ONES_ROWS, kc), BF16)

    def q_tile(t, carry):
        q0 = i * bq + t * tq
        n_real = q0 // kc + 1
        n_iter = (n_real + S_SLOTS - 1) // S_SLOTS

        def chunk_start(c):
            return pl.multiple_of(jnp.minimum(c, n_real - 1) * kc, kc)

        def stage_q():
            for hh in range(2):
                pltpu.matmul_push_rhs(qa_ref[:, pl.ds(hh * bq + t * tq, tq)],
                                      staging_register=0, mxu_index=hh)

        def stage_a(c, slot, restage=True):
            start = chunk_start(c)
            ka = jnp.concatenate([k_ref[pl.ds(start, kc), :], caug_ref[pl.ds(start, kc), :]],
                                 axis=1)
            for hh in range(2):
                pltpu.matmul_acc_lhs(S_ADDR[slot], ka, mxu_index=hh, load_staged_rhs=0)
            if restage:
                stage_q()

        def stage_b(c, slot, masked):
            start = chunk_start(c)
            for hh in range(2):
                s = pltpu.matmul_pop(S_ADDR[slot], (kc, tq), F32, mxu_index=hh)
                if masked:
                    key = c * kc + lax.broadcasted_iota(jnp.int32, (kc, tq), 0)
                    qry = q0 + lax.broadcasted_iota(jnp.int32, (kc, tq), 1)
                    s = jnp.where(key <= qry, s, NEG_BIG)
                p = jnp.exp(s).astype(BF16)
                pltpu.matmul_push_rhs(p, staging_register=1, mxu_index=hh)
                vta = jnp.concatenate(
                    [vt_ref[hh * HEAD_DIM:(hh + 1) * HEAD_DIM, pl.ds(start, kc)], ones_rows],
                    axis=0)
                pltpu.matmul_acc_lhs(O_ADDR, vta, mxu_index=hh, load_staged_rhs=1)

        stage_q()
        stage_a(0, 0)
        stage_a(1, 1)

        def body(g, carry):
            for s in range(S_SLOTS):
                c = g * S_SLOTS + s
                stage_a(c + 2, (s + 2) % S_SLOTS)
                stage_b(c, s, False)
            return carry

        lax.fori_loop(0, n_iter - 1, body, 0)
        c_last = (n_iter - 1) * S_SLOTS
        stage_a(c_last + 2, 2, restage=False)
        for s in range(S_SLOTS):
            stage_b(c_last + s, s, True)

        outs = []
        for hh in range(2):
            o = pltpu.matmul_pop(O_ADDR, (HEAD_DIM + ONES_ROWS, tq), F32, mxu_index=hh)
            outs.append(o[:HEAD_DIM, :] / o[HEAD_DIM:HEAD_DIM + 1, :])
        ot_ref[:, t * tq:(t + 1) * tq] = jnp.concatenate(outs, axis=0)
        return carry

    for t in range(bq // tq):
        q_tile(t, 0)
    o_ref[...] = (ot_ref[...].T * _silu(z_ref[...])).astype(BF16)


def _attention_mxu(qt, k, caug, vt, c, z):
    D, S = qt.shape
    bq = STREAM_BLOCK_Q
    c3 = c.reshape(N_PAIRS, 2, S)
    return pl.pallas_call(
        _attn_mxu_kernel,
        grid=(N_PAIRS, S // bq),
        in_specs=[
            pl.BlockSpec((LANES, bq), lambda p, i: (p, i)),
            pl.BlockSpec((S, LANES), lambda p, i: (0, p)),
            pl.BlockSpec((S, LANES), lambda p, i: (0, 0)),
            pl.BlockSpec((LANES, S), lambda p, i: (p, 0)),
            pl.BlockSpec((1, 2, bq), lambda p, i: (p, 0, i)),
            pl.BlockSpec((bq, LANES), lambda p, i: (i, p)),
        ],
        out_specs=pl.BlockSpec((bq, LANES), lambda p, i: (i, p)),
        out_shape=jax.ShapeDtypeStruct((S, D), BF16),
        scratch_shapes=[pltpu.VMEM((2 * LANES, 2 * bq), BF16), pltpu.VMEM((LANES, bq), F32)],
        compiler_params=pltpu.CompilerParams(
            dimension_semantics=("arbitrary", "arbitrary"), vmem_limit_bytes=VMEM_LIMIT_BYTES),
        name="fox_attention_mxu",
    )(qt, k, caug, vt, c3, z)


def _out_proj_kernel(x_ref, a_ref, w_ref, o_ref):
    o_ref[...] = x_ref[...] + jnp.dot(a_ref[...], w_ref[...], preferred_element_type=F32)


def _out_proj(x, a, w_out):
    S, D = x.shape
    T = OUT_ROW_TILE
    return pl.pallas_call(
        _out_proj_kernel,
        grid=(S // T,),
        in_specs=[
            pl.BlockSpec((T, D), lambda i: (i, 0)),
            pl.BlockSpec((T, D), lambda i: (i, 0)),
            pl.BlockSpec((D, D), lambda i: (0, 0)),
        ],
        out_specs=pl.BlockSpec((T, D), lambda i: (i, 0)),
        out_shape=jax.ShapeDtypeStruct((S, D), F32),
        compiler_params=pltpu.CompilerParams(
            dimension_semantics=("arbitrary",), vmem_limit_bytes=VMEM_LIMIT_BYTES),
        name="attn_out_proj",
    )(x, a, w_out)


def kernel(x, conv_norm_g, conv_w_in, conv_w, conv_w_out, attn_norm_g, attn_w_in,
           attn_b_f, attn_q_norm_g, attn_k_norm_g, attn_w_out):
    B, S, D = x.shape
    assert (B, D) == (1, D_MODEL)
    assert all(S % t == 0 for t in (BLOCK_Q, STREAM_BLOCK_Q, ROW_TILE, OUT_ROW_TILE))
    assert conv_w_in.shape[0] == 1 and attn_w_in.shape[0] == 1
    x2 = x.reshape(S, D)

    x2 = _conv_layer(x2, conv_norm_g[0].reshape(1, D), conv_w_in[0], conv_w[0], conv_w_out[0])

    w_in = attn_w_in[0]
    w_kz = jnp.concatenate([w_in[:, D:2 * D], w_in[:, 3 * D:4 * D]], axis=1)
    w_qv_t = jnp.concatenate([w_in[:, :D], w_in[:, 2 * D:3 * D]], axis=1).T
    w_f_t = w_in[:, 4 * D:].T
    qt, k, vt, z, c, caug = _attn_proj(
        x2, attn_norm_g[0].reshape(1, D), w_kz, w_qv_t, w_f_t,
        attn_b_f[0].reshape(N_HEADS, 1),
        jnp.tile(attn_q_norm_g[0], N_HEADS).reshape(D, 1),
        jnp.tile(attn_k_norm_g[0], N_HEADS).reshape(1, D))
    logit_bound = (math.sqrt(HEAD_DIM) * jnp.max(jnp.abs(attn_q_norm_g[0]))
                   * jnp.max(jnp.abs(attn_k_norm_g[0])))
    a = lax.cond(logit_bound <= LOGIT_BOUND_MAX,
                 lambda: _attention_mxu(qt, k, caug, vt, c, z),
                 lambda: _attention(qt.T, k, vt.T, c, z))
    out = _out_proj(x2, a, attn_w_out[0].astype(BF16))
    return out.reshape(B, S, D)
```

```python
import math

import jax
import jax.numpy as jnp
from jax import lax
from jax.experimental import pallas as pl
from jax.experimental.pallas import tpu as pltpu

D_MODEL = 1024
HEAD_DIM = 64
N_HEADS = D_MODEL // HEAD_DIM
N_PAIRS = N_HEADS // 2
CONV_WIDTH = 3
RMS_EPS = 1e-6

LANES = 128
SUBLANES = 8
MXU_DIM = 256
VMEM_LIMIT_BYTES = 56 * 1024 * 1024

ROW_TILE = 512
OUT_ROW_TILE = 1024
BLOCK_Q = 512
BLOCK_K = 512
NEG_BIG = -1e30
STREAM_BLOCK_Q = 2048
AUG_MID = N_HEADS
AUG_LO = 2 * N_HEADS
AUG_ONE = 3 * N_HEADS
LOGIT_BOUND_MAX = 60.0

BF16 = jnp.bfloat16
F32 = jnp.float32


def _rmsnorm_rows(x, g):
    inv = lax.rsqrt(jnp.mean(x * x, axis=-1, keepdims=True) + RMS_EPS)
    return x * inv * g


def _silu(z):
    return z * jax.nn.sigmoid(z)


def _conv_layer_kernel(x_ref, g_ref, w_in_ref, cw_ref, w_out_ref, o_ref, tail_ref):
    i = pl.program_id(0)
    D = D_MODEL
    T = x_ref.shape[0]

    @pl.when(i == 0)
    def _():
        tail_ref[...] = jnp.zeros_like(tail_ref)

    x = x_ref[...]
    h = _rmsnorm_rows(x, g_ref[...])
    proj = jnp.dot(h, w_in_ref[...], preferred_element_type=F32)
    b_g = proj[:, 0 * D:1 * D]
    c_g = proj[:, 1 * D:2 * D]
    xin = proj[:, 2 * D:3 * D]
    z = proj[:, 3 * D:4 * D]
    u = c_g * xin

    tail = tail_ref[...]
    prev1 = tail[SUBLANES - 1:SUBLANES, :]
    prev2 = tail[SUBLANES - 2:SUBLANES - 1, :]
    row = lax.broadcasted_iota(jnp.int32, (T, 1), 0)
    u1 = jnp.where(row == 0, prev1, pltpu.roll(u, 1, axis=0))
    u2 = jnp.where(row == 0, prev2, jnp.where(row == 1, prev1, pltpu.roll(u, 2, axis=0)))
    tail_ref[...] = u[T - SUBLANES:, :]

    cw = cw_ref[...]
    y = cw[0:1, :] * u2 + cw[1:2, :] * u1 + cw[2:3, :] * u
    y = b_g * y * _silu(z)
    o_ref[...] = x + jnp.dot(y, w_out_ref[...], preferred_element_type=F32)


def _conv_layer(x, g, w_in, cw, w_out):
    S, D = x.shape
    const = lambda i: (0, 0)
    return pl.pallas_call(
        _conv_layer_kernel,
        grid=(S // ROW_TILE,),
        in_specs=[
            pl.BlockSpec((ROW_TILE, D), lambda i: (i, 0)),
            pl.BlockSpec((1, D), const),
            pl.BlockSpec((D, 4 * D), const, pipeline_mode=pl.Buffered(1)),
            pl.BlockSpec((CONV_WIDTH, D), const),
            pl.BlockSpec((D, D), const, pipeline_mode=pl.Buffered(1)),
        ],
        out_specs=pl.BlockSpec((ROW_TILE, D), lambda i: (i, 0)),
        out_shape=jax.ShapeDtypeStruct((S, D), F32),
        scratch_shapes=[pltpu.VMEM((SUBLANES, D), F32)],
        compiler_params=pltpu.CompilerParams(
            dimension_semantics=("arbitrary",), vmem_limit_bytes=VMEM_LIMIT_BYTES),
        name="conv_layer",
    )(x, g, w_in, cw, w_out)


def _split3(x):
    hi = x.astype(BF16).astype(F32)
    r = x - hi
    mid = r.astype(BF16).astype(F32)
    lo = (r - mid).astype(BF16).astype(F32)
    return hi, mid, lo


def _head_rmsnorm(t, g, gmat):
    sq = t * t
    hi = sq.astype(BF16)
    lo = (sq - hi.astype(F32)).astype(BF16)
    parts = []
    for c in range(D_MODEL // MXU_DIM):
        sl = slice(c * MXU_DIM, (c + 1) * MXU_DIM)
        parts.append(jnp.dot(hi[:, sl], gmat, preferred_element_type=F32)
                     + jnp.dot(lo[:, sl], gmat, preferred_element_type=F32))
    ms = jnp.concatenate(parts, axis=-1)
    return t * lax.rsqrt(ms + RMS_EPS) * g


def _attn_proj_kernel(x_ref, g_ref, wk_ref, wz_ref, wqt_ref, wvt_ref, wf_ref, bf_ref, qg_ref, kg_ref,
                      gmat_ref, tri_ref,
                      qt_ref, k_ref, vt_ref, z_ref, c_ref, caug_ref, run_ref):
    i = pl.program_id(0)
    D = D_MODEL
    T = x_ref.shape[0]

    @pl.when(i == 0)
    def _():
        run_ref[...] = jnp.zeros_like(run_ref)

    h = _rmsnorm_rows(x_ref[...], g_ref[...])
    k = jnp.dot(h, wk_ref[...], preferred_element_type=F32)
    k_ref[...] = _head_rmsnorm(k, kg_ref[...], gmat_ref[...]).astype(BF16)
    z_ref[...] = jnp.dot(h, wz_ref[...], preferred_element_type=F32)

    trans_b = (((1,), (1,)), ((), ()))
    qt = lax.dot_general(wqt_ref[...], h, trans_b, preferred_element_type=F32)
    qt = qt.reshape(N_HEADS, HEAD_DIM, T)
    inv = lax.rsqrt(jnp.mean(qt * qt, axis=1, keepdims=True) + RMS_EPS)
    scale = 1.0 / math.sqrt(HEAD_DIM)
    qt = qt * inv * qg_ref[...].reshape(N_HEADS, HEAD_DIM, 1) * scale
    qt_ref[...] = qt.reshape(D, T).astype(BF16)
    vt_ref[...] = lax.dot_general(wvt_ref[...], h, trans_b,
                                  preferred_element_type=F32).astype(BF16)

    f = lax.dot_general(wf_ref[...], h, trans_b, preferred_element_type=F32) + bf_ref[...]
    logf = -(jnp.maximum(-f, 0.0) + jnp.log1p(jnp.exp(-jnp.abs(f))))
    tri = tri_ref[...]
    cs = sum(jnp.dot(piece.astype(BF16), tri, preferred_element_type=F32)
             for piece in _split3(logf))
    c = cs + run_ref[...][:, 0:1]
    c_ref[...] = c
    run_ref[...] = jnp.broadcast_to(c[:, T - 1:T], run_ref.shape)

    crow = jnp.concatenate([c, jnp.zeros((LANES - N_HEADS, T), F32)], axis=0).T
    hi, mid, lo = _split3(crow)
    lane = lax.broadcasted_iota(jnp.int32, (1, LANES), 1)
    ones = jnp.where((lane >= AUG_ONE) & (lane < AUG_ONE + 3), 1.0, 0.0)
    caug = hi + pltpu.roll(mid, AUG_MID, axis=1) + pltpu.roll(lo, AUG_LO, axis=1) + ones
    caug_ref[...] = caug.astype(BF16)


def _attn_proj(x, g, w_in, w_q_t, w_v_t, w_f_t, b_f, q_g_col, k_g_row):
    S, D = x.shape
    T = ROW_TILE
    head_of = jnp.arange(MXU_DIM) // HEAD_DIM
    gmat = jnp.where(head_of[:, None] == head_of[None, :], 1.0 / HEAD_DIM, 0.0).astype(BF16)
    tri = (jnp.arange(T)[:, None] <= jnp.arange(T)[None, :]).astype(BF16)
    const = lambda i: (0, 0)
    rows = lambda i: (i, 0)
    cols = lambda i: (0, i)
    return pl.pallas_call(
        _attn_proj_kernel,
        grid=(S // T,),
        in_specs=[
            pl.BlockSpec((T, D), rows),
            pl.BlockSpec((1, D), const),
            pl.BlockSpec((D, D), lambda i: (0, 1), pipeline_mode=pl.Buffered(1)),
            pl.BlockSpec((D, D), lambda i: (0, 3), pipeline_mode=pl.Buffered(1)),
            pl.BlockSpec((D, D), const, pipeline_mode=pl.Buffered(1)),
            pl.BlockSpec((D, D), const, pipeline_mode=pl.Buffered(1)),
            pl.BlockSpec((N_HEADS, D), const),
            pl.BlockSpec((N_HEADS, 1), const),
            pl.BlockSpec((D, 1), const),
            pl.BlockSpec((1, D), const),
            pl.BlockSpec((MXU_DIM, MXU_DIM), const),
            pl.BlockSpec((T, T), const),
        ],
        out_specs=[
            pl.BlockSpec((D, T), cols),
            pl.BlockSpec((T, D), rows),
            pl.BlockSpec((D, T), cols),
            pl.BlockSpec((T, D), rows),
            pl.BlockSpec((N_HEADS, T), cols),
            pl.BlockSpec((T, LANES), rows),
        ],
        out_shape=[
            jax.ShapeDtypeStruct((D, S), BF16),
            jax.ShapeDtypeStruct((S, D), BF16),
            jax.ShapeDtypeStruct((D, S), BF16),
            jax.ShapeDtypeStruct((S, D), F32),
            jax.ShapeDtypeStruct((N_HEADS, S), F32),
            jax.ShapeDtypeStruct((S, LANES), BF16),
        ],
        scratch_shapes=[pltpu.VMEM((N_HEADS, LANES), F32)],
        compiler_params=pltpu.CompilerParams(
            dimension_semantics=("arbitrary",), vmem_limit_bytes=VMEM_LIMIT_BYTES),
        name="attn_proj",
    )(x, g, w_in, w_in, w_q_t, w_v_t, w_f_t, b_f, q_g_col, k_g_row, gmat, tri)


def _attn_kernel(q_ref, k_ref, v_ref, c_ref, z_ref, o_ref, m_ref, l_ref, acc_ref):
    i = pl.program_id(1)
    bq, bk = BLOCK_Q, BLOCK_K
    q2 = q_ref[...]
    lane = lax.broadcasted_iota(jnp.int32, (1, LANES), 1)
    zero = jnp.zeros_like(q2)
    q_heads = (jnp.where(lane < HEAD_DIM, q2, zero), jnp.where(lane >= HEAD_DIM, q2, zero))

    m_ref[...] = jnp.full_like(m_ref, NEG_BIG)
    l_ref[...] = jnp.zeros_like(l_ref)
    acc_ref[...] = jnp.zeros_like(acc_ref)

    def step(j, masked):
        start = pl.multiple_of(j * bk, bk)
        kb = k_ref[pl.ds(start, bk), :]
        vb = v_ref[pl.ds(start, bk), :]
        cb = c_ref[0, :, pl.ds(start, bk)]
        for hh in range(2):
            s = lax.dot_general(q_heads[hh], kb, (((1,), (1,)), ((), ())),
                                preferred_element_type=F32)
            s = s - cb[hh:hh + 1, :]
            if masked:
                row = lax.broadcasted_iota(jnp.int32, (bq, bk), 0)
                col = lax.broadcasted_iota(jnp.int32, (bq, bk), 1)
                s = jnp.where(col <= row, s, NEG_BIG)
            m_old = m_ref[hh]
            m_new = jnp.maximum(m_old, jnp.max(s, axis=-1, keepdims=True))
            alpha = jnp.exp(m_old - m_new)
            p = jnp.exp(s - m_new)
            l_ref[hh] = alpha * l_ref[hh] + jnp.sum(p, axis=-1, keepdims=True)
            acc_ref[hh] = alpha * acc_ref[hh] + jnp.dot(
                p.astype(BF16), vb, preferred_element_type=F32)
            m_ref[hh] = m_new

    def body(j, carry):
        step(j, masked=False)
        return carry

    lax.fori_loop(0, i, body, 0)
    step(i, masked=True)

    o0 = acc_ref[0] / l_ref[0]
    o1 = acc_ref[1] / l_ref[1]
    o = jnp.where(lane < HEAD_DIM, o0, o1)
    o_ref[...] = (o * _silu(z_ref[...])).astype(BF16)


def _attention(q, k, v, c, z):
    S, D = q.shape
    bq = BLOCK_Q
    c3 = c.reshape(N_PAIRS, 2, S)
    return pl.pallas_call(
        _attn_kernel,
        grid=(N_PAIRS, S // bq),
        in_specs=[
            pl.BlockSpec((bq, LANES), lambda p, i: (i, p)),
            pl.BlockSpec((S, LANES), lambda p, i: (0, p)),
            pl.BlockSpec((S, LANES), lambda p, i: (0, p)),
            pl.BlockSpec((1, 2, S), lambda p, i: (p, 0, 0)),
            pl.BlockSpec((bq, LANES), lambda p, i: (i, p)),
        ],
        out_specs=pl.BlockSpec((bq, LANES), lambda p, i: (i, p)),
        out_shape=jax.ShapeDtypeStruct((S, D), BF16),
        scratch_shapes=[
            pltpu.VMEM((2, bq, 1), F32),
            pltpu.VMEM((2, bq, 1), F32),
            pltpu.VMEM((2, bq, LANES), F32),
        ],
        compiler_params=pltpu.CompilerParams(
            dimension_semantics=("arbitrary", "arbitrary"), vmem_limit_bytes=VMEM_LIMIT_BYTES),
        name="fox_attention",
    )(q, k, v, c3, z)


MXU_TILE = 256
S_SLOTS = 3
S_TILE_ENTRIES = MXU_TILE * MXU_TILE // (SUBLANES * LANES)
S_ADDR = tuple(s * S_TILE_ENTRIES for s in range(S_SLOTS))
O_ADDR = S_SLOTS * S_TILE_ENTRIES
ONES_ROWS = 16


def _attn_mxu_kernel(qt_ref, k_ref, caug_ref, vt_ref, c_ref, z_ref, o_ref, qa_ref, ot_ref):
    pair = pl.program_id(0)
    i = pl.program_id(1)
    bq, tq, kc = STREAM_BLOCK_Q, MXU_TILE, MXU_TILE

    qt = qt_ref[...]
    r = lax.broadcasted_iota(jnp.int32, (LANES, 1), 0)
    zero = jnp.zeros_like(qt)
    top = jnp.concatenate([jnp.where(r < HEAD_DIM, qt, zero),
                           jnp.where(r >= HEAD_DIM, qt, zero)], axis=1)
    ci = c_ref[0]
    halves = []
    for hh in range(2):
        head = 2 * pair + hh
        c_hi, c_mid, c_lo = _split3(ci[hh:hh + 1, :])
        minus_cj = jnp.where((r == head) | (r == AUG_MID + head) | (r == AUG_LO + head), -1.0, 0.0)
        halves.append(minus_cj + jnp.where(r == AUG_ONE, c_hi, 0.0)
                      + jnp.where(r == AUG_ONE + 1, c_mid, 0.0)
                      + jnp.where(r == AUG_ONE + 2, c_lo, 0.0))
    qa_ref[...] = jnp.concatenate([top, jnp.concatenate(halves, axis=1).astype(BF16)], axis=0)
    ones_rows = jnp.ones((ONES_ROWS, kc), BF16)

    def q_tile(t, carry):
        q0 = i * bq + t * tq
        n_real = q0 // kc + 1
        n_iter = (n_real + S_SLOTS - 1) // S_SLOTS

        def chunk_start(c):
            return pl.multiple_of(jnp.minimum(c, n_real - 1) * kc, kc)

        def stage_q():
            for hh in range(2):
                pltpu.matmul_push_rhs(qa_ref[:, pl.ds(hh * bq + t * tq, tq)],
                                      staging_register=0, mxu_index=hh)

        def stage_a(c, slot, restage=True):
            start = chunk_start(c)
            ka = jnp.concatenate([k_ref[pl.ds(start, kc), :], caug_ref[pl.ds(start, kc), :]],
                                 axis=1)
            for hh in range(2):
                pltpu.matmul_acc_lhs(S_ADDR[slot], ka, mxu_index=hh, load_staged_rhs=0)
            if restage:
                stage_q()

        def stage_b(c, slot, masked):
            start = chunk_start(c)
            for hh in range(2):
                s = pltpu.matmul_pop(S_ADDR[slot], (kc, tq), F32, mxu_index=hh)
                if masked:
                    key = c * kc + lax.broadcasted_iota(jnp.int32, (kc, tq), 0)
                    qry = q0 + lax.broadcasted_iota(jnp.int32, (kc, tq), 1)
                    s = jnp.where(key <= qry, s, NEG_BIG)
                p = jnp.exp(s).astype(BF16)
                pltpu.matmul_push_rhs(p, staging_register=1, mxu_index=hh)
                vta = jnp.concatenate(
                    [vt_ref[hh * HEAD_DIM:(hh + 1) * HEAD_DIM, pl.ds(start, kc)], ones_rows],
                    axis=0)
                pltpu.matmul_acc_lhs(O_ADDR, vta, mxu_index=hh, load_staged_rhs=1)

        stage_q()
        stage_a(0, 0)
        stage_a(1, 1)

        def body(g, carry):
            for s in range(S_SLOTS):
                c = g * S_SLOTS + s
                stage_a(c + 2, (s + 2) % S_SLOTS)
                stage_b(c, s, False)
            return carry

        lax.fori_loop(0, n_iter - 1, body, 0)
        c_last = (n_iter - 1) * S_SLOTS
        stage_a(c_last + 2, 2, restage=False)
        for s in range(S_SLOTS):
            stage_b(c_last + s, s, True)

        outs = []
        for hh in range(2):
            o = pltpu.matmul_pop(O_ADDR, (HEAD_DIM + ONES_ROWS, tq), F32, mxu_index=hh)
            outs.append(o[:HEAD_DIM, :] / o[HEAD_DIM:HEAD_DIM + 1, :])
        ot_ref[:, t * tq:(t + 1) * tq] = jnp.concatenate(outs, axis=0)
        return carry

    for t in range(bq // tq):
        q_tile(t, 0)
    o_ref[...] = (ot_ref[...].T * _silu(z_ref[...])).astype(BF16)


def _attention_mxu(qt, k, caug, vt, c, z):
    D, S = qt.shape
    bq = STREAM_BLOCK_Q
    c3 = c.reshape(N_PAIRS, 2, S)
    return pl.pallas_call(
        _attn_mxu_kernel,
        grid=(N_PAIRS, S // bq),
        in_specs=[
            pl.BlockSpec((LANES, bq), lambda p, i: (p, i)),
            pl.BlockSpec((S, LANES), lambda p, i: (0, p)),
            pl.BlockSpec((S, LANES), lambda p, i: (0, 0)),
            pl.BlockSpec((LANES, S), lambda p, i: (p, 0)),
            pl.BlockSpec((1, 2, bq), lambda p, i: (p, 0, i)),
            pl.BlockSpec((bq, LANES), lambda p, i: (i, p)),
        ],
        out_specs=pl.BlockSpec((bq, LANES), lambda p, i: (i, p)),
        out_shape=jax.ShapeDtypeStruct((S, D), BF16),
        scratch_shapes=[pltpu.VMEM((2 * LANES, 2 * bq), BF16), pltpu.VMEM((LANES, bq), F32)],
        compiler_params=pltpu.CompilerParams(
            dimension_semantics=("arbitrary", "arbitrary"), vmem_limit_bytes=VMEM_LIMIT_BYTES),
        name="fox_attention_mxu",
    )(qt, k, caug, vt, c3, z)


def _out_proj_kernel(x_ref, a_ref, w_ref, o_ref):
    o_ref[...] = x_ref[...] + jnp.dot(a_ref[...], w_ref[...].astype(BF16),
                                      preferred_element_type=F32)


def _out_proj(x, a, w_out):
    S, D = x.shape
    T = OUT_ROW_TILE
    return pl.pallas_call(
        _out_proj_kernel,
        grid=(S // T,),
        in_specs=[
            pl.BlockSpec((T, D), lambda i: (i, 0)),
            pl.BlockSpec((T, D), lambda i: (i, 0)),
            pl.BlockSpec((D, D), lambda i: (0, 0), pipeline_mode=pl.Buffered(1)),
        ],
        out_specs=pl.BlockSpec((T, D), lambda i: (i, 0)),
        out_shape=jax.ShapeDtypeStruct((S, D), F32),
        compiler_params=pltpu.CompilerParams(
            dimension_semantics=("arbitrary",), vmem_limit_bytes=VMEM_LIMIT_BYTES),
        name="attn_out_proj",
    )(x, a, w_out)


def kernel(x, conv_norm_g, conv_w_in, conv_w, conv_w_out, attn_norm_g, attn_w_in,
           attn_b_f, attn_q_norm_g, attn_k_norm_g, attn_w_out):
    B, S, D = x.shape
    assert (B, D) == (1, D_MODEL)
    assert all(S % t == 0 for t in (BLOCK_Q, STREAM_BLOCK_Q, ROW_TILE, OUT_ROW_TILE))
    assert conv_w_in.shape[0] == 1 and attn_w_in.shape[0] == 1
    x2 = x.reshape(S, D)

    x2 = _conv_layer(x2, conv_norm_g[0].reshape(1, D), conv_w_in[0], conv_w[0], conv_w_out[0])

    w_in = attn_w_in[0]
    qt, k, vt, z, c, caug = _attn_proj(
        x2, attn_norm_g[0].reshape(1, D), w_in, w_in[:, :D].T, w_in[:, 2 * D:3 * D].T,
        w_in[:, 4 * D:].T,
        attn_b_f[0].reshape(N_HEADS, 1),
        jnp.tile(attn_q_norm_g[0], N_HEADS).reshape(D, 1),
        jnp.tile(attn_k_norm_g[0], N_HEADS).reshape(1, D))
    logit_bound = (math.sqrt(HEAD_DIM) * jnp.max(jnp.abs(attn_q_norm_g[0]))
                   * jnp.max(jnp.abs(attn_k_norm_g[0])))
    a = lax.cond(logit_bound <= LOGIT_BOUND_MAX,
                 lambda: _attention_mxu(qt, k, caug, vt, c, z),
                 lambda: _attention(qt.T, k, vt.T, c, z))
    out = _out_proj(x2, a, attn_w_out[0])
    return out.reshape(B, S, D)
```

```python
import math

import jax
import jax.numpy as jnp
from jax import lax
from jax.experimental import pallas as pl
from jax.experimental.pallas import tpu as pltpu

D_MODEL = 1024
HEAD_DIM = 64
N_HEADS = D_MODEL // HEAD_DIM
N_PAIRS = N_HEADS // 2
CONV_WIDTH = 3
RMS_EPS = 1e-6

LANES = 128
SUBLANES = 8
MXU_DIM = 256
VMEM_LIMIT_BYTES = 56 * 1024 * 1024

ROW_TILE = 512
OUT_ROW_TILE = 1024
BLOCK_Q = 512
BLOCK_K = 512
NEG_BIG = -1e30
STREAM_BLOCK_Q = 2048
AUG_MID = N_HEADS
AUG_LO = 2 * N_HEADS
AUG_ONE = 3 * N_HEADS
LOGIT_BOUND_MAX = 60.0

BF16 = jnp.bfloat16
F32 = jnp.float32


def _rmsnorm_rows(x, g):
    inv = lax.rsqrt(jnp.mean(x * x, axis=-1, keepdims=True) + RMS_EPS)
    return x * inv * g


def _silu(z):
    return z * jax.nn.sigmoid(z)


def _conv_layer_kernel(x_ref, g_ref, w_in_ref, cw_ref, w_out_ref, o_ref, tail_ref):
    i = pl.program_id(0)
    D = D_MODEL
    T = x_ref.shape[0]

    @pl.when(i == 0)
    def _():
        tail_ref[...] = jnp.zeros_like(tail_ref)

    x = x_ref[...]
    h = _rmsnorm_rows(x, g_ref[...])
    proj = jnp.dot(h, w_in_ref[...], preferred_element_type=F32)
    b_g = proj[:, 0 * D:1 * D]
    c_g = proj[:, 1 * D:2 * D]
    xin = proj[:, 2 * D:3 * D]
    z = proj[:, 3 * D:4 * D]
    u = c_g * xin

    tail = tail_ref[...]
    prev1 = tail[SUBLANES - 1:SUBLANES, :]
    prev2 = tail[SUBLANES - 2:SUBLANES - 1, :]
    row = lax.broadcasted_iota(jnp.int32, (T, 1), 0)
    u1 = jnp.where(row == 0, prev1, pltpu.roll(u, 1, axis=0))
    u2 = jnp.where(row == 0, prev2, jnp.where(row == 1, prev1, pltpu.roll(u, 2, axis=0)))
    tail_ref[...] = u[T - SUBLANES:, :]

    cw = cw_ref[...]
    y = cw[0:1, :] * u2 + cw[1:2, :] * u1 + cw[2:3, :] * u
    y = b_g * y * _silu(z)
    o_ref[...] = x + jnp.dot(y, w_out_ref[...], preferred_element_type=F32)


def _conv_layer(x, g, w_in, cw, w_out):
    S, D = x.shape
    const = lambda i: (0, 0)
    return pl.pallas_call(
        _conv_layer_kernel,
        grid=(S // ROW_TILE,),
        in_specs=[
            pl.BlockSpec((ROW_TILE, D), lambda i: (i, 0)),
            pl.BlockSpec((1, D), const),
            pl.BlockSpec((D, 4 * D), const, pipeline_mode=pl.Buffered(1)),
            pl.BlockSpec((CONV_WIDTH, D), const),
            pl.BlockSpec((D, D), const, pipeline_mode=pl.Buffered(1)),
        ],
        out_specs=pl.BlockSpec((ROW_TILE, D), lambda i: (i, 0)),
        out_shape=jax.ShapeDtypeStruct((S, D), F32),
        scratch_shapes=[pltpu.VMEM((SUBLANES, D), F32)],
        compiler_params=pltpu.CompilerParams(
            dimension_semantics=("arbitrary",), vmem_limit_bytes=VMEM_LIMIT_BYTES),
        name="conv_layer",
    )(x, g, w_in, cw, w_out)


def _split3(x):
    hi = x.astype(BF16).astype(F32)
    r = x - hi
    mid = r.astype(BF16).astype(F32)
    lo = (r - mid).astype(BF16).astype(F32)
    return hi, mid, lo


def _head_rmsnorm(t, g, gmat):
    sq = t * t
    hi = sq.astype(BF16)
    lo = (sq - hi.astype(F32)).astype(BF16)
    parts = []
    for c in range(D_MODEL // MXU_DIM):
        sl = slice(c * MXU_DIM, (c + 1) * MXU_DIM)
        parts.append(jnp.dot(hi[:, sl], gmat, preferred_element_type=F32)
                     + jnp.dot(lo[:, sl], gmat, preferred_element_type=F32))
    ms = jnp.concatenate(parts, axis=-1)
    return t * lax.rsqrt(ms + RMS_EPS) * g


def _attn_proj_kernel(x_ref, g_ref, wq_ref, wk_ref, wv_ref, wz_ref, wf_ref, bf_ref, qg_ref, kg_ref,
                      gmat_ref, tri_ref,
                      qt_ref, k_ref, vt_ref, z_ref, c_ref, caug_ref, run_ref):
    i = pl.program_id(0)
    D = D_MODEL
    T = x_ref.shape[0]

    @pl.when(i == 0)
    def _():
        run_ref[...] = jnp.zeros_like(run_ref)

    h = _rmsnorm_rows(x_ref[...], g_ref[...])
    k = jnp.dot(h, wk_ref[...], preferred_element_type=F32)
    k_ref[...] = _head_rmsnorm(k, kg_ref[...], gmat_ref[...]).astype(BF16)
    z_ref[...] = jnp.dot(h, wz_ref[...], preferred_element_type=F32)

    trans_b = (((1,), (1,)), ((), ()))
    qt = jnp.dot(h, wq_ref[...], preferred_element_type=F32).T
    qt = qt.reshape(N_HEADS, HEAD_DIM, T)
    inv = lax.rsqrt(jnp.mean(qt * qt, axis=1, keepdims=True) + RMS_EPS)
    scale = 1.0 / math.sqrt(HEAD_DIM)
    qt = qt * inv * qg_ref[...].reshape(N_HEADS, HEAD_DIM, 1) * scale
    qt_ref[...] = qt.reshape(D, T).astype(BF16)
    vt_ref[...] = jnp.dot(h, wv_ref[...], preferred_element_type=F32).T.astype(BF16)

    f = lax.dot_general(wf_ref[...], h, trans_b, preferred_element_type=F32) + bf_ref[...]
    logf = -(jnp.maximum(-f, 0.0) + jnp.log1p(jnp.exp(-jnp.abs(f))))
    tri = tri_ref[...]
    cs = sum(jnp.dot(piece.astype(BF16), tri, preferred_element_type=F32)
             for piece in _split3(logf))
    c = cs + run_ref[...][:, 0:1]
    c_ref[...] = c
    run_ref[...] = jnp.broadcast_to(c[:, T - 1:T], run_ref.shape)

    crow = jnp.concatenate([c, jnp.zeros((LANES - N_HEADS, T), F32)], axis=0).T
    hi, mid, lo = _split3(crow)
    lane = lax.broadcasted_iota(jnp.int32, (1, LANES), 1)
    ones = jnp.where((lane >= AUG_ONE) & (lane < AUG_ONE + 3), 1.0, 0.0)
    caug = hi + pltpu.roll(mid, AUG_MID, axis=1) + pltpu.roll(lo, AUG_LO, axis=1) + ones
    caug_ref[...] = caug.astype(BF16)


def _attn_proj(x, g, w_in, w_f_t, b_f, q_g_col, k_g_row):
    S, D = x.shape
    T = ROW_TILE
    head_of = jnp.arange(MXU_DIM) // HEAD_DIM
    gmat = jnp.where(head_of[:, None] == head_of[None, :], 1.0 / HEAD_DIM, 0.0).astype(BF16)
    tri = (jnp.arange(T)[:, None] <= jnp.arange(T)[None, :]).astype(BF16)
    const = lambda i: (0, 0)
    rows = lambda i: (i, 0)
    cols = lambda i: (0, i)
    return pl.pallas_call(
        _attn_proj_kernel,
        grid=(S // T,),
        in_specs=[
            pl.BlockSpec((T, D), rows),
            pl.BlockSpec((1, D), const),
            pl.BlockSpec((D, D), lambda i: (0, 0), pipeline_mode=pl.Buffered(1)),
            pl.BlockSpec((D, D), lambda i: (0, 1), pipeline_mode=pl.Buffered(1)),
            pl.BlockSpec((D, D), lambda i: (0, 2), pipeline_mode=pl.Buffered(1)),
            pl.BlockSpec((D, D), lambda i: (0, 3), pipeline_mode=pl.Buffered(1)),
            pl.BlockSpec((N_HEADS, D), const),
            pl.BlockSpec((N_HEADS, 1), const),
            pl.BlockSpec((D, 1), const),
            pl.BlockSpec((1, D), const),
            pl.BlockSpec((MXU_DIM, MXU_DIM), const),
            pl.BlockSpec((T, T), const),
        ],
        out_specs=[
            pl.BlockSpec((D, T), cols),
            pl.BlockSpec((T, D), rows),
            pl.BlockSpec((D, T), cols),
            pl.BlockSpec((T, D), rows),
            pl.BlockSpec((N_HEADS, T), cols),
            pl.BlockSpec((T, LANES), rows),
        ],
        out_shape=[
            jax.ShapeDtypeStruct((D, S), BF16),
            jax.ShapeDtypeStruct((S, D), BF16),
            jax.ShapeDtypeStruct((D, S), BF16),
            jax.ShapeDtypeStruct((S, D), F32),
            jax.ShapeDtypeStruct((N_HEADS, S), F32),
            jax.ShapeDtypeStruct((S, LANES), BF16),
        ],
        scratch_shapes=[pltpu.VMEM((N_HEADS, LANES), F32)],
        compiler_params=pltpu.CompilerParams(
            dimension_semantics=("arbitrary",), vmem_limit_bytes=VMEM_LIMIT_BYTES),
        name="attn_proj",
    )(x, g, w_in, w_in, w_in, w_in, w_f_t, b_f, q_g_col, k_g_row, gmat, tri)


def _attn_kernel(q_ref, k_ref, v_ref, c_ref, z_ref, o_ref, m_ref, l_ref, acc_ref):
    i = pl.program_id(1)
    bq, bk = BLOCK_Q, BLOCK_K
    q2 = q_ref[...]
    lane = lax.broadcasted_iota(jnp.int32, (1, LANES), 1)
    zero = jnp.zeros_like(q2)
    q_heads = (jnp.where(lane < HEAD_DIM, q2, zero), jnp.where(lane >= HEAD_DIM, q2, zero))

    m_ref[...] = jnp.full_like(m_ref, NEG_BIG)
    l_ref[...] = jnp.zeros_like(l_ref)
    acc_ref[...] = jnp.zeros_like(acc_ref)

    def step(j, masked):
        start = pl.multiple_of(j * bk, bk)
        kb = k_ref[pl.ds(start, bk), :]
        vb = v_ref[pl.ds(start, bk), :]
        cb = c_ref[0, :, pl.ds(start, bk)]
        for hh in range(2):
            s = lax.dot_general(q_heads[hh], kb, (((1,), (1,)), ((), ())),
                                preferred_element_type=F32)
            s = s - cb[hh:hh + 1, :]
            if masked:
                row = lax.broadcasted_iota(jnp.int32, (bq, bk), 0)
                col = lax.broadcasted_iota(jnp.int32, (bq, bk), 1)
                s = jnp.where(col <= row, s, NEG_BIG)
            m_old = m_ref[hh]
            m_new = jnp.maximum(m_old, jnp.max(s, axis=-1, keepdims=True))
            alpha = jnp.exp(m_old - m_new)
            p = jnp.exp(s - m_new)
            l_ref[hh] = alpha * l_ref[hh] + jnp.sum(p, axis=-1, keepdims=True)
            acc_ref[hh] = alpha * acc_ref[hh] + jnp.dot(
                p.astype(BF16), vb, preferred_element_type=F32)
            m_ref[hh] = m_new

    def body(j, carry):
        step(j, masked=False)
        return carry

    lax.fori_loop(0, i, body, 0)
    step(i, masked=True)

    o0 = acc_ref[0] / l_ref[0]
    o1 = acc_ref[1] / l_ref[1]
    o = jnp.where(lane < HEAD_DIM, o0, o1)
    o_ref[...] = (o * _silu(z_ref[...])).astype(BF16)


def _attention(q, k, v, c, z):
    S, D = q.shape
    bq = BLOCK_Q
    c3 = c.reshape(N_PAIRS, 2, S)
    return pl.pallas_call(
        _attn_kernel,
        grid=(N_PAIRS, S // bq),
        in_specs=[
            pl.BlockSpec((bq, LANES), lambda p, i: (i, p)),
            pl.BlockSpec((S, LANES), lambda p, i: (0, p)),
            pl.BlockSpec((S, LANES), lambda p, i: (0, p)),
            pl.BlockSpec((1, 2, S), lambda p, i: (p, 0, 0)),
            pl.BlockSpec((bq, LANES), lambda p, i: (i, p)),
        ],
        out_specs=pl.BlockSpec((bq, LANES), lambda p, i: (i, p)),
        out_shape=jax.ShapeDtypeStruct((S, D), BF16),
        scratch_shapes=[
            pltpu.VMEM((2, bq, 1), F32),
            pltpu.VMEM((2, bq, 1), F32),
            pltpu.VMEM((2, bq, LANES), F32),
        ],
        compiler_params=pltpu.CompilerParams(
            dimension_semantics=("arbitrary", "arbitrary"), vmem_limit_bytes=VMEM_LIMIT_BYTES),
        name="fox_attention",
    )(q, k, v, c3, z)


MXU_TILE = 256
S_SLOTS = 3
S_TILE_ENTRIES = MXU_TILE * MXU_TILE // (SUBLANES * LANES)
S_ADDR = tuple(s * S_TILE_ENTRIES for s in range(S_SLOTS))
O_ADDR = S_SLOTS * S_TILE_ENTRIES
ONES_ROWS = 16


def _attn_mxu_kernel(qt_ref, k_ref, caug_ref, vt_ref, c_ref, z_ref, o_ref, qa_ref, ot_ref):
    pair = pl.program_id(0)
    i = pl.program_id(1)
    bq, tq, kc = STREAM_BLOCK_Q, MXU_TILE, MXU_TILE

    qt = qt_ref[...]
    r = lax.broadcasted_iota(jnp.int32, (LANES, 1), 0)
    zero = jnp.zeros_like(qt)
    top = jnp.concatenate([jnp.where(r < HEAD_DIM, qt, zero),
                           jnp.where(r >= HEAD_DIM, qt, zero)], axis=1)
    ci = c_ref[0]
    halves = []
    for hh in range(2):
        head = 2 * pair + hh
        c_hi, c_mid, c_lo = _split3(ci[hh:hh + 1, :])
        minus_cj = jnp.where((r == head) | (r == AUG_MID + head) | (r == AUG_LO + head), -1.0, 0.0)
        halves.append(minus_cj + jnp.where(r == AUG_ONE, c_hi, 0.0)
                      + jnp.where(r == AUG_ONE + 1, c_mid, 0.0)
                      + jnp.where(r == AUG_ONE + 2, c_lo, 0.0))
    qa_ref[...] = jnp.concatenate([top, jnp.concatenate(halves, axis=1).astype(BF16)], axis=0)
    ones_rows = jnp.ones((ONES_ROWS, kc), BF16)

    def q_tile(t, carry):
        q0 = i * bq + t * tq
        n_real = q0 // kc + 1
        n_iter = (n_real + S_SLOTS - 1) // S_SLOTS

        def chunk_start(c):
            return pl.multiple_of(jnp.minimum(c, n_real - 1) * kc, kc)

        def stage_q():
            for hh in range(2):
                pltpu.matmul_push_rhs(qa_ref[:, pl.ds(hh * bq + t * tq, tq)],
                                      staging_register=0, mxu_index=hh)

        def stage_a(c, slot, restage=True):
            start = chunk_start(c)
            ka = jnp.concatenate([k_ref[pl.ds(start, kc), :], caug_ref[pl.ds(start, kc), :]],
                                 axis=1)
            for hh in range(2):
                pltpu.matmul_acc_lhs(S_ADDR[slot], ka, mxu_index=hh, load_staged_rhs=0)
            if restage:
                stage_q()

        def stage_b(c, slot, masked):
            start = chunk_start(c)
            for hh in range(2):
                s = pltpu.matmul_pop(S_ADDR[slot], (kc, tq), F32, mxu_index=hh)
                if masked:
                    key = c * kc + lax.broadcasted_iota(jnp.int32, (kc, tq), 0)
                    qry = q0 + lax.broadcasted_iota(jnp.int32, (kc, tq), 1)
                    s = jnp.where(key <= qry, s, NEG_BIG)
                p = jnp.exp(s).astype(BF16)
                pltpu.matmul_push_rhs(p, staging_register=1, mxu_index=hh)
                vta = jnp.concatenate(
                    [vt_ref[hh * HEAD_DIM:(hh + 1) * HEAD_DIM, pl.ds(start, kc)], ones_rows],
                    axis=0)
                pltpu.matmul_acc_lhs(O_ADDR, vta, mxu_index=hh, load_staged_rhs=1)

        stage_q()
        stage_a(0, 0)
        stage_a(1, 1)

        def body(g, carry):
            for s in range(S_SLOTS):
                c = g * S_SLOTS + s
                stage_a(c + 2, (s + 2) % S_SLOTS)
                stage_b(c, s, False)
            return carry

        lax.fori_loop(0, n_iter - 1, body, 0)
        c_last = (n_iter - 1) * S_SLOTS
        stage_a(c_last + 2, 2, restage=False)
        for s in range(S_SLOTS):
            stage_b(c_last + s, s, True)

        outs = []
        for hh in range(2):
            o = pltpu.matmul_pop(O_ADDR, (HEAD_DIM + ONES_ROWS, tq), F32, mxu_index=hh)
            outs.append(o[:HEAD_DIM, :] / o[HEAD_DIM:HEAD_DIM + 1, :])
        ot_ref[:, t * tq:(t + 1) * tq] = jnp.concatenate(outs, axis=0)
        return carry

    for t in range(bq // tq):
        q_tile(t, 0)
    o_ref[...] = (ot_ref[...].T * _silu(z_ref[...])).astype(BF16)


def _attention_mxu(qt, k, caug, vt, c, z):
    D, S = qt.shape
    bq = STREAM_BLOCK_Q
    c3 = c.reshape(N_PAIRS, 2, S)
    return pl.pallas_call(
        _attn_mxu_kernel,
        grid=(N_PAIRS, S // bq),
        in_specs=[
            pl.BlockSpec((LANES, bq), lambda p, i: (p, i)),
            pl.BlockSpec((S, LANES), lambda p, i: (0, p)),
            pl.BlockSpec((S, LANES), lambda p, i: (0, 0)),
            pl.BlockSpec((LANES, S), lambda p, i: (p, 0)),
            pl.BlockSpec((1, 2, bq), lambda p, i: (p, 0, i)),
            pl.BlockSpec((bq, LANES), lambda p, i: (i, p)),
        ],
        out_specs=pl.BlockSpec((bq, LANES), lambda p, i: (i, p)),
        out_shape=jax.ShapeDtypeStruct((S, D), BF16),
        scratch_shapes=[pltpu.VMEM((2 * LANES, 2 * bq), BF16), pltpu.VMEM((LANES, bq), F32)],
        compiler_params=pltpu.CompilerParams(
            dimension_semantics=("arbitrary", "arbitrary"), vmem_limit_bytes=VMEM_LIMIT_BYTES),
        name="fox_attention_mxu",
    )(qt, k, caug, vt, c3, z)


def _out_proj_kernel(x_ref, a_ref, w_ref, o_ref):
    o_ref[...] = x_ref[...] + jnp.dot(a_ref[...], w_ref[...].astype(BF16),
                                      preferred_element_type=F32)


def _out_proj(x, a, w_out):
    S, D = x.shape
    T = OUT_ROW_TILE
    return pl.pallas_call(
        _out_proj_kernel,
        grid=(S // T,),
        in_specs=[
            pl.BlockSpec((T, D), lambda i: (i, 0)),
            pl.BlockSpec((T, D), lambda i: (i, 0)),
            pl.BlockSpec((D, D), lambda i: (0, 0), pipeline_mode=pl.Buffered(1)),
        ],
        out_specs=pl.BlockSpec((T, D), lambda i: (i, 0)),
        out_shape=jax.ShapeDtypeStruct((S, D), F32),
        compiler_params=pltpu.CompilerParams(
            dimension_semantics=("arbitrary",), vmem_limit_bytes=VMEM_LIMIT_BYTES),
        name="attn_out_proj",
    )(x, a, w_out)


def kernel(x, conv_norm_g, conv_w_in, conv_w, conv_w_out, attn_norm_g, attn_w_in,
           attn_b_f, attn_q_norm_g, attn_k_norm_g, attn_w_out):
    B, S, D = x.shape
    assert (B, D) == (1, D_MODEL)
    assert all(S % t == 0 for t in (BLOCK_Q, STREAM_BLOCK_Q, ROW_TILE, OUT_ROW_TILE))
    assert conv_w_in.shape[0] == 1 and attn_w_in.shape[0] == 1
    x2 = x.reshape(S, D)

    x2 = _conv_layer(x2, conv_norm_g[0].reshape(1, D), conv_w_in[0], conv_w[0], conv_w_out[0])

    w_in = attn_w_in[0]
    qt, k, vt, z, c, caug = _attn_proj(
        x2, attn_norm_g[0].reshape(1, D), w_in, w_in[:, 4 * D:].T,
        attn_b_f[0].reshape(N_HEADS, 1),
        jnp.tile(attn_q_norm_g[0], N_HEADS).reshape(D, 1),
        jnp.tile(attn_k_norm_g[0], N_HEADS).reshape(1, D))
    logit_bound = (math.sqrt(HEAD_DIM) * jnp.max(jnp.abs(attn_q_norm_g[0]))
                   * jnp.max(jnp.abs(attn_k_norm_g[0])))
    a = lax.cond(logit_bound <= LOGIT_BOUND_MAX,
                 lambda: _attention_mxu(qt, k, caug, vt, c, z),
                 lambda: _attention(qt.T, k, vt.T, c, z))
    out = _out_proj(x2, a, attn_w_out[0])
    return out.reshape(B, S, D)
```

```python
import math

import jax
import jax.numpy as jnp
from jax import lax
from jax.experimental import pallas as pl
from jax.experimental.pallas import tpu as pltpu

D_MODEL = 1024
HEAD_DIM = 64
N_HEADS = D_MODEL // HEAD_DIM
N_PAIRS = N_HEADS // 2
CONV_WIDTH = 3
RMS_EPS = 1e-6

LANES = 128
SUBLANES = 8
MXU_DIM = 256
VMEM_LIMIT_BYTES = 56 * 1024 * 1024

ROW_TILE = 512
OUT_ROW_TILE = 1024
BLOCK_Q = 512
BLOCK_K = 512
NEG_BIG = -1e30
STREAM_BLOCK_Q = 2048
AUG_MID = N_HEADS
AUG_LO = 2 * N_HEADS
AUG_ONE = 3 * N_HEADS
LOGIT_BOUND_MAX = 60.0

BF16 = jnp.bfloat16
F32 = jnp.float32


def _rmsnorm_rows(x, g):
    inv = lax.rsqrt(jnp.mean(x * x, axis=-1, keepdims=True) + RMS_EPS)
    return x * inv * g


def _silu(z):
    return z * jax.nn.sigmoid(z)


def _conv_layer_kernel(x_ref, g_ref, w_in_ref, cw_ref, w_out_ref, o_ref, tail_ref):
    i = pl.program_id(0)
    D = D_MODEL
    T = x_ref.shape[0]

    @pl.when(i == 0)
    def _():
        tail_ref[...] = jnp.zeros_like(tail_ref)

    x = x_ref[...]
    h = _rmsnorm_rows(x, g_ref[...])
    proj = jnp.dot(h, w_in_ref[...], preferred_element_type=F32)
    b_g = proj[:, 0 * D:1 * D]
    c_g = proj[:, 1 * D:2 * D]
    xin = proj[:, 2 * D:3 * D]
    z = proj[:, 3 * D:4 * D]
    u = c_g * xin

    tail = tail_ref[...]
    prev1 = tail[SUBLANES - 1:SUBLANES, :]
    prev2 = tail[SUBLANES - 2:SUBLANES - 1, :]
    row = lax.broadcasted_iota(jnp.int32, (T, 1), 0)
    u1 = jnp.where(row == 0, prev1, pltpu.roll(u, 1, axis=0))
    u2 = jnp.where(row == 0, prev2, jnp.where(row == 1, prev1, pltpu.roll(u, 2, axis=0)))
    tail_ref[...] = u[T - SUBLANES:, :]

    cw = cw_ref[...]
    y = cw[0:1, :] * u2 + cw[1:2, :] * u1 + cw[2:3, :] * u
    y = b_g * y * _silu(z)
    o_ref[...] = x + jnp.dot(y, w_out_ref[...], preferred_element_type=F32)


def _conv_layer(x, g, w_in, cw, w_out):
    S, D = x.shape
    const = lambda i: (0, 0)
    return pl.pallas_call(
        _conv_layer_kernel,
        grid=(S // ROW_TILE,),
        in_specs=[
            pl.BlockSpec((ROW_TILE, D), lambda i: (i, 0)),
            pl.BlockSpec((1, D), const),
            pl.BlockSpec((D, 4 * D), const, pipeline_mode=pl.Buffered(1)),
            pl.BlockSpec((CONV_WIDTH, D), const),
            pl.BlockSpec((D, D), const, pipeline_mode=pl.Buffered(1)),
        ],
        out_specs=pl.BlockSpec((ROW_TILE, D), lambda i: (i, 0)),
        out_shape=jax.ShapeDtypeStruct((S, D), F32),
        scratch_shapes=[pltpu.VMEM((SUBLANES, D), F32)],
        compiler_params=pltpu.CompilerParams(
            dimension_semantics=("arbitrary",), vmem_limit_bytes=VMEM_LIMIT_BYTES),
        name="conv_layer",
    )(x, g, w_in, cw, w_out)


def _split3(x):
    hi = x.astype(BF16).astype(F32)
    r = x - hi
    mid = r.astype(BF16).astype(F32)
    lo = (r - mid).astype(BF16).astype(F32)
    return hi, mid, lo


def _head_rmsnorm(t, g, gmat):
    sq = t * t
    hi = sq.astype(BF16)
    lo = (sq - hi.astype(F32)).astype(BF16)
    parts = []
    for c in range(D_MODEL // MXU_DIM):
        sl = slice(c * MXU_DIM, (c + 1) * MXU_DIM)
        parts.append(jnp.dot(hi[:, sl], gmat, preferred_element_type=F32)
                     + jnp.dot(lo[:, sl], gmat, preferred_element_type=F32))
    ms = jnp.concatenate(parts, axis=-1)
    return t * lax.rsqrt(ms + RMS_EPS) * g


def _attn_proj_kernel(x_ref, g_ref, wq_ref, wk_ref, wv_ref, wz_ref, wf_ref, bf_ref, qg_ref, kg_ref,
                      gmat_ref, tri_ref,
                      qt_ref, k_ref, vt_ref, z_ref, c_ref, caug_ref, run_ref):
    i = pl.program_id(0)
    D = D_MODEL
    T = x_ref.shape[0]

    @pl.when(i == 0)
    def _():
        run_ref[...] = jnp.zeros_like(run_ref)

    trans_b = (((1,), (1,)), ((), ()))
    h = _rmsnorm_rows(x_ref[...], g_ref[...])
    k = lax.dot_general(h, wk_ref[...], trans_b, preferred_element_type=F32)
    k_ref[...] = _head_rmsnorm(k, kg_ref[...], gmat_ref[...]).astype(BF16)
    z_ref[...] = lax.dot_general(h, wz_ref[...], trans_b, preferred_element_type=F32)

    qt = lax.dot_general(wq_ref[...], h, trans_b, preferred_element_type=F32)
    qt = qt.reshape(N_HEADS, HEAD_DIM, T)
    inv = lax.rsqrt(jnp.mean(qt * qt, axis=1, keepdims=True) + RMS_EPS)
    scale = 1.0 / math.sqrt(HEAD_DIM)
    qt = qt * inv * qg_ref[...].reshape(N_HEADS, HEAD_DIM, 1) * scale
    qt_ref[...] = qt.reshape(D, T).astype(BF16)
    vt_ref[...] = lax.dot_general(wv_ref[...], h, trans_b,
                                  preferred_element_type=F32).astype(BF16)

    f = lax.dot_general(wf_ref[...], h, trans_b, preferred_element_type=F32) + bf_ref[...]
    logf = -(jnp.maximum(-f, 0.0) + jnp.log1p(jnp.exp(-jnp.abs(f))))
    tri = tri_ref[...]
    cs = sum(jnp.dot(piece.astype(BF16), tri, preferred_element_type=F32)
             for piece in _split3(logf))
    c = cs + run_ref[...][:, 0:1]
    c_ref[...] = c
    run_ref[...] = jnp.broadcast_to(c[:, T - 1:T], run_ref.shape)

    crow = jnp.concatenate([c, jnp.zeros((LANES - N_HEADS, T), F32)], axis=0).T
    hi, mid, lo = _split3(crow)
    lane = lax.broadcasted_iota(jnp.int32, (1, LANES), 1)
    ones = jnp.where((lane >= AUG_ONE) & (lane < AUG_ONE + 3), 1.0, 0.0)
    caug = hi + pltpu.roll(mid, AUG_MID, axis=1) + pltpu.roll(lo, AUG_LO, axis=1) + ones
    caug_ref[...] = caug.astype(BF16)


def _attn_proj(x, g, w_in_t, b_f, q_g_col, k_g_row):
    S, D = x.shape
    T = ROW_TILE
    head_of = jnp.arange(MXU_DIM) // HEAD_DIM
    gmat = jnp.where(head_of[:, None] == head_of[None, :], 1.0 / HEAD_DIM, 0.0).astype(BF16)
    tri = (jnp.arange(T)[:, None] <= jnp.arange(T)[None, :]).astype(BF16)
    const = lambda i: (0, 0)
    rows = lambda i: (i, 0)
    cols = lambda i: (0, i)
    return pl.pallas_call(
        _attn_proj_kernel,
        grid=(S // T,),
        in_specs=[
            pl.BlockSpec((T, D), rows),
            pl.BlockSpec((1, D), const),
            pl.BlockSpec((None, D, D), lambda i: (0, 0, 0), pipeline_mode=pl.Buffered(1)),
            pl.BlockSpec((None, D, D), lambda i: (0, 1, 0), pipeline_mode=pl.Buffered(1)),
            pl.BlockSpec((None, D, D), lambda i: (0, 2, 0), pipeline_mode=pl.Buffered(1)),
            pl.BlockSpec((None, D, D), lambda i: (0, 3, 0), pipeline_mode=pl.Buffered(1)),
            pl.BlockSpec((None, N_HEADS, D), lambda i: (0, 4 * D // N_HEADS, 0)),
            pl.BlockSpec((N_HEADS, 1), const),
            pl.BlockSpec((D, 1), const),
            pl.BlockSpec((1, D), const),
            pl.BlockSpec((MXU_DIM, MXU_DIM), const),
            pl.BlockSpec((T, T), const),
        ],
        out_specs=[
            pl.BlockSpec((D, T), cols),
            pl.BlockSpec((T, D), rows),
            pl.BlockSpec((D, T), cols),
            pl.BlockSpec((T, D), rows),
            pl.BlockSpec((N_HEADS, T), cols),
            pl.BlockSpec((T, LANES), rows),
        ],
        out_shape=[
            jax.ShapeDtypeStruct((D, S), BF16),
            jax.ShapeDtypeStruct((S, D), BF16),
            jax.ShapeDtypeStruct((D, S), BF16),
            jax.ShapeDtypeStruct((S, D), F32),
            jax.ShapeDtypeStruct((N_HEADS, S), F32),
            jax.ShapeDtypeStruct((S, LANES), BF16),
        ],
        scratch_shapes=[pltpu.VMEM((N_HEADS, LANES), F32)],
        compiler_params=pltpu.CompilerParams(
            dimension_semantics=("arbitrary",), vmem_limit_bytes=VMEM_LIMIT_BYTES),
        name="attn_proj",
    )(x, g, w_in_t, w_in_t, w_in_t, w_in_t, w_in_t, b_f, q_g_col, k_g_row, gmat, tri)


def _attn_kernel(q_ref, k_ref, v_ref, c_ref, z_ref, o_ref, m_ref, l_ref, acc_ref):
    i = pl.program_id(1)
    bq, bk = BLOCK_Q, BLOCK_K
    q2 = q_ref[...]
    lane = lax.broadcasted_iota(jnp.int32, (1, LANES), 1)
    zero = jnp.zeros_like(q2)
    q_heads = (jnp.where(lane < HEAD_DIM, q2, zero), jnp.where(lane >= HEAD_DIM, q2, zero))

    m_ref[...] = jnp.full_like(m_ref, NEG_BIG)
    l_ref[...] = jnp.zeros_like(l_ref)
    acc_ref[...] = jnp.zeros_like(acc_ref)

    def step(j, masked):
        start = pl.multiple_of(j * bk, bk)
        kb = k_ref[pl.ds(start, bk), :]
        vb = v_ref[pl.ds(start, bk), :]
        cb = c_ref[0, :, pl.ds(start, bk)]
        for hh in range(2):
            s = lax.dot_general(q_heads[hh], kb, (((1,), (1,)), ((), ())),
                                preferred_element_type=F32)
            s = s - cb[hh:hh + 1, :]
            if masked:
                row = lax.broadcasted_iota(jnp.int32, (bq, bk), 0)
                col = lax.broadcasted_iota(jnp.int32, (bq, bk), 1)
                s = jnp.where(col <= row, s, NEG_BIG)
            m_old = m_ref[hh]
            m_new = jnp.maximum(m_old, jnp.max(s, axis=-1, keepdims=True))
            alpha = jnp.exp(m_old - m_new)
            p = jnp.exp(s - m_new)
            l_ref[hh] = alpha * l_ref[hh] + jnp.sum(p, axis=-1, keepdims=True)
            acc_ref[hh] = alpha * acc_ref[hh] + jnp.dot(
                p.astype(BF16), vb, preferred_element_type=F32)
            m_ref[hh] = m_new

    def body(j, carry):
        step(j, masked=False)
        return carry

    lax.fori_loop(0, i, body, 0)
    step(i, masked=True)

    o0 = acc_ref[0] / l_ref[0]
    o1 = acc_ref[1] / l_ref[1]
    o = jnp.where(lane < HEAD_DIM, o0, o1)
    o_ref[...] = (o * _silu(z_ref[...])).astype(BF16)


def _attention(q, k, v, c, z):
    S, D = q.shape
    bq = BLOCK_Q
    c3 = c.reshape(N_PAIRS, 2, S)
    return pl.pallas_call(
        _attn_kernel,
        grid=(N_PAIRS, S // bq),
        in_specs=[
            pl.BlockSpec((bq, LANES), lambda p, i: (i, p)),
            pl.BlockSpec((S, LANES), lambda p, i: (0, p)),
            pl.BlockSpec((S, LANES), lambda p, i: (0, p)),
            pl.BlockSpec((1, 2, S), lambda p, i: (p, 0, 0)),
            pl.BlockSpec((bq, LANES), lambda p, i: (i, p)),
        ],
        out_specs=pl.BlockSpec((bq, LANES), lambda p, i: (i, p)),
        out_shape=jax.ShapeDtypeStruct((S, D), BF16),
        scratch_shapes=[
            pltpu.VMEM((2, bq, 1), F32),
            pltpu.VMEM((2, bq, 1), F32),
            pltpu.VMEM((2, bq, LANES), F32),
        ],
        compiler_params=pltpu.CompilerParams(
            dimension_semantics=("arbitrary", "arbitrary"), vmem_limit_bytes=VMEM_LIMIT_BYTES),
        name="fox_attention",
    )(q, k, v, c3, z)


MXU_TILE = 256
S_SLOTS = 3
S_TILE_ENTRIES = MXU_TILE * MXU_TILE // (SUBLANES * LANES)
S_ADDR = tuple(s * S_TILE_ENTRIES for s in range(S_SLOTS))
O_ADDR = S_SLOTS * S_TILE_ENTRIES
ONES_ROWS = 16


def _attn_mxu_kernel(qt_ref, k_ref, caug_ref, vt_ref, c_ref, z_ref, o_ref, qa_ref, ot_ref):
    pair = pl.program_id(0)
    i = pl.program_id(1)
    bq, tq, kc = STREAM_BLOCK_Q, MXU_TILE, MXU_TILE

    qt = qt_ref[...]
    r = lax.broadcasted_iota(jnp.int32, (LANES, 1), 0)
    zero = jnp.zeros_like(qt)
    top = jnp.concatenate([jnp.where(r < HEAD_DIM, qt, zero),
                           jnp.where(r >= HEAD_DIM, qt, zero)], axis=1)
    ci = c_ref[0]
    halves = []
    for hh in range(2):
        head = 2 * pair + hh
        c_hi, c_mid, c_lo = _split3(ci[hh:hh + 1, :])
        minus_cj = jnp.where((r == head) | (r == AUG_MID + head) | (r == AUG_LO + head), -1.0, 0.0)
        halves.append(minus_cj + jnp.where(r == AUG_ONE, c_hi, 0.0)
                      + jnp.where(r == AUG_ONE + 1, c_mid, 0.0)
                      + jnp.where(r == AUG_ONE + 2, c_lo, 0.0))
    qa_ref[...] = jnp.concatenate([top, jnp.concatenate(halves, axis=1).astype(BF16)], axis=0)
    ones_rows = jnp.ones((ONES_ROWS, kc), BF16)

    def q_tile(t, carry):
        q0 = i * bq + t * tq
        n_real = q0 // kc + 1
        n_iter = (n_real + S_SLOTS - 1) // S_SLOTS

        def chunk_start(c):
            return pl.multiple_of(jnp.minimum(c, n_real - 1) * kc, kc)

        def stage_q():
            for hh in range(2):
                pltpu.matmul_push_rhs(qa_ref[:, pl.ds(hh * bq + t * tq, tq)],
                                      staging_register=0, mxu_index=hh)

        def stage_a(c, slot, restage=True):
            start = chunk_start(c)
            ka = jnp.concatenate([k_ref[pl.ds(start, kc), :], caug_ref[pl.ds(start, kc), :]],
                                 axis=1)
            for hh in range(2):
                pltpu.matmul_acc_lhs(S_ADDR[slot], ka, mxu_index=hh, load_staged_rhs=0)
            if restage:
                stage_q()

        def stage_b(c, slot, masked):
            start = chunk_start(c)
            for hh in range(2):
                s = pltpu.matmul_pop(S_ADDR[slot], (kc, tq), F32, mxu_index=hh)
                if masked:
                    key = c * kc + lax.broadcasted_iota(jnp.int32, (kc, tq), 0)
                    qry = q0 + lax.broadcasted_iota(jnp.int32, (kc, tq), 1)
                    s = jnp.where(key <= qry, s, NEG_BIG)
                p = jnp.exp(s).astype(BF16)
                pltpu.matmul_push_rhs(p, staging_register=1, mxu_index=hh)
                vta = jnp.concatenate(
                    [vt_ref[hh * HEAD_DIM:(hh + 1) * HEAD_DIM, pl.ds(start, kc)], ones_rows],
                    axis=0)
                pltpu.matmul_acc_lhs(O_ADDR, vta, mxu_index=hh, load_staged_rhs=1)

        stage_q()
        stage_a(0, 0)
        stage_a(1, 1)

        def body(g, carry):
            for s in range(S_SLOTS):
                c = g * S_SLOTS + s
                stage_a(c + 2, (s + 2) % S_SLOTS)
                stage_b(c, s, False)
            return carry

        lax.fori_loop(0, n_iter - 1, body, 0)
        c_last = (n_iter - 1) * S_SLOTS
        stage_a(c_last + 2, 2, restage=False)
        for s in range(S_SLOTS):
            stage_b(c_last + s, s, True)

        outs = []
        for hh in range(2):
            o = pltpu.matmul_pop(O_ADDR, (HEAD_DIM + ONES_ROWS, tq), F32, mxu_index=hh)
            outs.append(o[:HEAD_DIM, :] / o[HEAD_DIM:HEAD_DIM + 1, :])
        ot_ref[:, t * tq:(t + 1) * tq] = jnp.concatenate(outs, axis=0)
        return carry

    for t in range(bq // tq):
        q_tile(t, 0)
    o_ref[...] = (ot_ref[...].T * _silu(z_ref[...])).astype(BF16)


def _attention_mxu(qt, k, caug, vt, c, z):
    D, S = qt.shape
    bq = STREAM_BLOCK_Q
    c3 = c.reshape(N_PAIRS, 2, S)
    return pl.pallas_call(
        _attn_mxu_kernel,
        grid=(N_PAIRS, S // bq),
        in_specs=[
            pl.BlockSpec((LANES, bq), lambda p, i: (p, i)),
            pl.BlockSpec((S, LANES), lambda p, i: (0, p)),
            pl.BlockSpec((S, LANES), lambda p, i: (0, 0)),
            pl.BlockSpec((LANES, S), lambda p, i: (p, 0)),
            pl.BlockSpec((1, 2, bq), lambda p, i: (p, 0, i)),
            pl.BlockSpec((bq, LANES), lambda p, i: (i, p)),
        ],
        out_specs=pl.BlockSpec((bq, LANES), lambda p, i: (i, p)),
        out_shape=jax.ShapeDtypeStruct((S, D), BF16),
        scratch_shapes=[pltpu.VMEM((2 * LANES, 2 * bq), BF16), pltpu.VMEM((LANES, bq), F32)],
        compiler_params=pltpu.CompilerParams(
            dimension_semantics=("arbitrary", "arbitrary"), vmem_limit_bytes=VMEM_LIMIT_BYTES),
        name="fox_attention_mxu",
    )(qt, k, caug, vt, c3, z)


def _out_proj_kernel(x_ref, a_ref, w_ref, o_ref):
    o_ref[...] = x_ref[...] + jnp.dot(a_ref[...], w_ref[...].astype(BF16),
                                      preferred_element_type=F32)


def _out_proj(x, a, w_out):
    S, D = x.shape
    T = OUT_ROW_TILE
    return pl.pallas_call(
        _out_proj_kernel,
        grid=(S // T,),
        in_specs=[
            pl.BlockSpec((T, D), lambda i: (i, 0)),
            pl.BlockSpec((T, D), lambda i: (i, 0)),
            pl.BlockSpec((D, D), lambda i: (0, 0), pipeline_mode=pl.Buffered(1)),
        ],
        out_specs=pl.BlockSpec((T, D), lambda i: (i, 0)),
        out_shape=jax.ShapeDtypeStruct((S, D), F32),
        compiler_params=pltpu.CompilerParams(
            dimension_semantics=("arbitrary",), vmem_limit_bytes=VMEM_LIMIT_BYTES),
        name="attn_out_proj",
    )(x, a, w_out)


def kernel(x, conv_norm_g, conv_w_in, conv_w, conv_w_out, attn_norm_g, attn_w_in,
           attn_b_f, attn_q_norm_g, attn_k_norm_g, attn_w_out):
    B, S, D = x.shape
    assert (B, D) == (1, D_MODEL)
    assert all(S % t == 0 for t in (BLOCK_Q, STREAM_BLOCK_Q, ROW_TILE, OUT_ROW_TILE))
    assert conv_w_in.shape[0] == 1 and attn_w_in.shape[0] == 1
    x2 = x.reshape(S, D)

    x2 = _conv_layer(x2, conv_norm_g[0].reshape(1, D), conv_w_in[0], conv_w[0], conv_w_out[0])

    qt, k, vt, z, c, caug = _attn_proj(
        x2, attn_norm_g[0].reshape(1, D), jnp.swapaxes(attn_w_in, 1, 2),
        attn_b_f[0].reshape(N_HEADS, 1),
        jnp.tile(attn_q_norm_g[0], N_HEADS).reshape(D, 1),
        jnp.tile(attn_k_norm_g[0], N_HEADS).reshape(1, D))
    logit_bound = (math.sqrt(HEAD_DIM) * jnp.max(jnp.abs(attn_q_norm_g[0]))
                   * jnp.max(jnp.abs(attn_k_norm_g[0])))
    a = lax.cond(logit_bound <= LOGIT_BOUND_MAX,
                 lambda: _attention_mxu(qt, k, caug, vt, c, z),
                 lambda: _attention(qt.T, k, vt.T, c, z))
    out = _out_proj(x2, a, attn_w_out[0])
    return out.reshape(B, S, D)
```

```python
import math

import jax
import jax.numpy as jnp
from jax import lax
from jax.experimental import pallas as pl
from jax.experimental.pallas import tpu as pltpu

D_MODEL = 1024
HEAD_DIM = 64
N_HEADS = D_MODEL // HEAD_DIM
N_PAIRS = N_HEADS // 2
CONV_WIDTH = 3
RMS_EPS = 1e-6

LANES = 128
SUBLANES = 8
VMEM_LIMIT_BYTES = 56 * 1024 * 1024

ROW_TILE = 512
OUT_ROW_TILE = 1024
BLOCK_Q = 512
BLOCK_K = 512
NEG_BIG = -1e30
STREAM_BLOCK_Q = 2048
AUG_MID = N_HEADS
AUG_LO = 2 * N_HEADS
AUG_ONE = 3 * N_HEADS
LOGIT_BOUND_MAX = 60.0

BF16 = jnp.bfloat16
F32 = jnp.float32


def _rmsnorm_rows(x, g):
    inv = lax.rsqrt(jnp.mean(x * x, axis=-1, keepdims=True) + RMS_EPS)
    return x * inv * g


def _silu(z):
    return z * jax.nn.sigmoid(z)


def _conv_layer_kernel(x_ref, g_ref, w_in_ref, cw_ref, w_out_ref, o_ref, tail_ref):
    i = pl.program_id(0)
    D = D_MODEL
    T = x_ref.shape[0]

    @pl.when(i == 0)
    def _():
        tail_ref[...] = jnp.zeros_like(tail_ref)

    x = x_ref[...]
    h = _rmsnorm_rows(x, g_ref[...])
    proj = jnp.dot(h, w_in_ref[...], preferred_element_type=F32)
    b_g = proj[:, 0 * D:1 * D]
    c_g = proj[:, 1 * D:2 * D]
    xin = proj[:, 2 * D:3 * D]
    z = proj[:, 3 * D:4 * D]
    u = c_g * xin

    tail = tail_ref[...]
    prev1 = tail[SUBLANES - 1:SUBLANES, :]
    prev2 = tail[SUBLANES - 2:SUBLANES - 1, :]
    row = lax.broadcasted_iota(jnp.int32, (T, 1), 0)
    u1 = jnp.where(row == 0, prev1, pltpu.roll(u, 1, axis=0))
    u2 = jnp.where(row == 0, prev2, jnp.where(row == 1, prev1, pltpu.roll(u, 2, axis=0)))
    tail_ref[...] = u[T - SUBLANES:, :]

    cw = cw_ref[...]
    y = cw[0:1, :] * u2 + cw[1:2, :] * u1 + cw[2:3, :] * u
    y = b_g * y * _silu(z)
    o_ref[...] = x + jnp.dot(y, w_out_ref[...], preferred_element_type=F32)


def _conv_layer(x, g, w_in, cw, w_out):
    S, D = x.shape
    const = lambda i: (0, 0)
    return pl.pallas_call(
        _conv_layer_kernel,
        grid=(S // ROW_TILE,),
        in_specs=[
            pl.BlockSpec((ROW_TILE, D), lambda i: (i, 0)),
            pl.BlockSpec((1, D), const),
            pl.BlockSpec((D, 4 * D), const, pipeline_mode=pl.Buffered(1)),
            pl.BlockSpec((CONV_WIDTH, D), const),
            pl.BlockSpec((D, D), const, pipeline_mode=pl.Buffered(1)),
        ],
        out_specs=pl.BlockSpec((ROW_TILE, D), lambda i: (i, 0)),
        out_shape=jax.ShapeDtypeStruct((S, D), F32),
        scratch_shapes=[pltpu.VMEM((SUBLANES, D), F32)],
        compiler_params=pltpu.CompilerParams(
            dimension_semantics=("arbitrary",), vmem_limit_bytes=VMEM_LIMIT_BYTES),
        name="conv_layer",
    )(x, g, w_in, cw, w_out)


def _split3(x):
    hi = x.astype(BF16).astype(F32)
    r = x - hi
    mid = r.astype(BF16).astype(F32)
    lo = (r - mid).astype(BF16).astype(F32)
    return hi, mid, lo


def _head_rmsnorm_t(xt, g_col):
    t = xt.shape[1]
    x3 = xt.reshape(N_HEADS, HEAD_DIM, t)
    inv = lax.rsqrt(jnp.mean(x3 * x3, axis=1, keepdims=True) + RMS_EPS)
    return (x3 * inv * g_col.reshape(N_HEADS, HEAD_DIM, 1)).reshape(N_HEADS * HEAD_DIM, t)


def _attn_proj_kernel(x_ref, g_ref, wq_ref, wk_ref, wv_ref, wz_ref, wf_ref, bf_ref, qg_ref, kg_ref,
                      tri_ref,
                      qt_ref, k_ref, vt_ref, z_ref, c_ref, caug_ref, run_ref):
    i = pl.program_id(0)
    D = D_MODEL
    T = x_ref.shape[0]

    @pl.when(i == 0)
    def _():
        run_ref[...] = jnp.zeros_like(run_ref)

    trans_b = (((1,), (1,)), ((), ()))
    h = _rmsnorm_rows(x_ref[...], g_ref[...])
    z_ref[...] = lax.dot_general(h, wz_ref[...], trans_b, preferred_element_type=F32)

    qt = lax.dot_general(wq_ref[...], h, trans_b, preferred_element_type=F32)
    scale = 1.0 / math.sqrt(HEAD_DIM)
    qt_ref[...] = (_head_rmsnorm_t(qt, qg_ref[...]) * scale).astype(BF16)
    kt = lax.dot_general(wk_ref[...], h, trans_b, preferred_element_type=F32)
    k_ref[...] = _head_rmsnorm_t(kt, kg_ref[...]).T.astype(BF16)
    vt_ref[...] = lax.dot_general(wv_ref[...], h, trans_b,
                                  preferred_element_type=F32).astype(BF16)

    f = lax.dot_general(wf_ref[...], h, trans_b, preferred_element_type=F32) + bf_ref[...]
    logf = -(jnp.maximum(-f, 0.0) + jnp.log1p(jnp.exp(-jnp.abs(f))))
    tri = tri_ref[...]
    cs = sum(jnp.dot(piece.astype(BF16), tri, preferred_element_type=F32)
             for piece in _split3(logf))
    c = cs + run_ref[...][:, 0:1]
    c_ref[...] = c
    run_ref[...] = jnp.broadcast_to(c[:, T - 1:T], run_ref.shape)

    crow = jnp.concatenate([c, jnp.zeros((LANES - N_HEADS, T), F32)], axis=0).T
    hi, mid, lo = _split3(crow)
    lane = lax.broadcasted_iota(jnp.int32, (1, LANES), 1)
    ones = jnp.where((lane >= AUG_ONE) & (lane < AUG_ONE + 3), 1.0, 0.0)
    caug = hi + pltpu.roll(mid, AUG_MID, axis=1) + pltpu.roll(lo, AUG_LO, axis=1) + ones
    caug_ref[...] = caug.astype(BF16)


def _attn_proj(x, g, w_in_t, b_f, q_g_col, k_g_col):
    S, D = x.shape
    T = ROW_TILE
    tri = (jnp.arange(T)[:, None] <= jnp.arange(T)[None, :]).astype(BF16)
    const = lambda i: (0, 0)
    rows = lambda i: (i, 0)
    cols = lambda i: (0, i)
    return pl.pallas_call(
        _attn_proj_kernel,
        grid=(S // T,),
        in_specs=[
            pl.BlockSpec((T, D), rows),
            pl.BlockSpec((1, D), const),
            pl.BlockSpec((None, D, D), lambda i: (0, 0, 0), pipeline_mode=pl.Buffered(1)),
            pl.BlockSpec((None, D, D), lambda i: (0, 1, 0), pipeline_mode=pl.Buffered(1)),
            pl.BlockSpec((None, D, D), lambda i: (0, 2, 0), pipeline_mode=pl.Buffered(1)),
            pl.BlockSpec((None, D, D), lambda i: (0, 3, 0), pipeline_mode=pl.Buffered(1)),
            pl.BlockSpec((None, N_HEADS, D), lambda i: (0, 4 * D // N_HEADS, 0)),
            pl.BlockSpec((N_HEADS, 1), const),
            pl.BlockSpec((D, 1), const),
            pl.BlockSpec((D, 1), const),
            pl.BlockSpec((T, T), const),
        ],
        out_specs=[
            pl.BlockSpec((D, T), cols),
            pl.BlockSpec((T, D), rows),
            pl.BlockSpec((D, T), cols),
            pl.BlockSpec((T, D), rows),
            pl.BlockSpec((N_HEADS, T), cols),
            pl.BlockSpec((T, LANES), rows),
        ],
        out_shape=[
            jax.ShapeDtypeStruct((D, S), BF16),
            jax.ShapeDtypeStruct((S, D), BF16),
            jax.ShapeDtypeStruct((D, S), BF16),
            jax.ShapeDtypeStruct((S, D), F32),
            jax.ShapeDtypeStruct((N_HEADS, S), F32),
            jax.ShapeDtypeStruct((S, LANES), BF16),
        ],
        scratch_shapes=[pltpu.VMEM((N_HEADS, LANES), F32)],
        compiler_params=pltpu.CompilerParams(
            dimension_semantics=("arbitrary",), vmem_limit_bytes=VMEM_LIMIT_BYTES),
        name="attn_proj",
    )(x, g, w_in_t, w_in_t, w_in_t, w_in_t, w_in_t, b_f, q_g_col, k_g_col, tri)


def _attn_kernel(q_ref, k_ref, v_ref, c_ref, z_ref, o_ref, m_ref, l_ref, acc_ref):
    i = pl.program_id(1)
    bq, bk = BLOCK_Q, BLOCK_K
    q2 = q_ref[...]
    lane = lax.broadcasted_iota(jnp.int32, (1, LANES), 1)
    zero = jnp.zeros_like(q2)
    q_heads = (jnp.where(lane < HEAD_DIM, q2, zero), jnp.where(lane >= HEAD_DIM, q2, zero))

    m_ref[...] = jnp.full_like(m_ref, NEG_BIG)
    l_ref[...] = jnp.zeros_like(l_ref)
    acc_ref[...] = jnp.zeros_like(acc_ref)

    def step(j, masked):
        start = pl.multiple_of(j * bk, bk)
        kb = k_ref[pl.ds(start, bk), :]
        vb = v_ref[pl.ds(start, bk), :]
        cb = c_ref[0, :, pl.ds(start, bk)]
        for hh in range(2):
            s = lax.dot_general(q_heads[hh], kb, (((1,), (1,)), ((), ())),
                                preferred_element_type=F32)
            s = s - cb[hh:hh + 1, :]
            if masked:
                row = lax.broadcasted_iota(jnp.int32, (bq, bk), 0)
                col = lax.broadcasted_iota(jnp.int32, (bq, bk), 1)
                s = jnp.where(col <= row, s, NEG_BIG)
            m_old = m_ref[hh]
            m_new = jnp.maximum(m_old, jnp.max(s, axis=-1, keepdims=True))
            alpha = jnp.exp(m_old - m_new)
            p = jnp.exp(s - m_new)
            l_ref[hh] = alpha * l_ref[hh] + jnp.sum(p, axis=-1, keepdims=True)
            acc_ref[hh] = alpha * acc_ref[hh] + jnp.dot(
                p.astype(BF16), vb, preferred_element_type=F32)
            m_ref[hh] = m_new

    def body(j, carry):
        step(j, masked=False)
        return carry

    lax.fori_loop(0, i, body, 0)
    step(i, masked=True)

    o0 = acc_ref[0] / l_ref[0]
    o1 = acc_ref[1] / l_ref[1]
    o = jnp.where(lane < HEAD_DIM, o0, o1)
    o_ref[...] = (o * _silu(z_ref[...])).astype(BF16)


def _attention(q, k, v, c, z):
    S, D = q.shape
    bq = BLOCK_Q
    c3 = c.reshape(N_PAIRS, 2, S)
    return pl.pallas_call(
        _attn_kernel,
        grid=(N_PAIRS, S // bq),
        in_specs=[
            pl.BlockSpec((bq, LANES), lambda p, i: (i, p)),
            pl.BlockSpec((S, LANES), lambda p, i: (0, p)),
            pl.BlockSpec((S, LANES), lambda p, i: (0, p)),
            pl.BlockSpec((1, 2, S), lambda p, i: (p, 0, 0)),
            pl.BlockSpec((bq, LANES), lambda p, i: (i, p)),
        ],
        out_specs=pl.BlockSpec((bq, LANES), lambda p, i: (i, p)),
        out_shape=jax.ShapeDtypeStruct((S, D), BF16),
        scratch_shapes=[
            pltpu.VMEM((2, bq, 1), F32),
            pltpu.VMEM((2, bq, 1), F32),
            pltpu.VMEM((2, bq, LANES), F32),
        ],
        compiler_params=pltpu.CompilerParams(
            dimension_semantics=("arbitrary", "arbitrary"), vmem_limit_bytes=VMEM_LIMIT_BYTES),
        name="fox_attention",
    )(q, k, v, c3, z)


MXU_TILE = 256
S_SLOTS = 3
S_TILE_ENTRIES = MXU_TILE * MXU_TILE // (SUBLANES * LANES)
S_ADDR = tuple(s * S_TILE_ENTRIES for s in range(S_SLOTS))
O_ADDR = S_SLOTS * S_TILE_ENTRIES
ONES_ROWS = 16


def _attn_mxu_kernel(qt_ref, k_ref, caug_ref, vt_ref, c_ref, z_ref, o_ref, qa_ref, ot_ref):
    pair = pl.program_id(0)
    i = pl.program_id(1)
    bq, tq, kc = STREAM_BLOCK_Q, MXU_TILE, MXU_TILE

    qt = qt_ref[...]
    r = lax.broadcasted_iota(jnp.int32, (LANES, 1), 0)
    zero = jnp.zeros_like(qt)
    top = jnp.concatenate([jnp.where(r < HEAD_DIM, qt, zero),
                           jnp.where(r >= HEAD_DIM, qt, zero)], axis=1)
    ci = c_ref[0]
    halves = []
    for hh in range(2):
        head = 2 * pair + hh
        c_hi, c_mid, c_lo = _split3(ci[hh:hh + 1, :])
        minus_cj = jnp.where((r == head) | (r == AUG_MID + head) | (r == AUG_LO + head), -1.0, 0.0)
        halves.append(minus_cj + jnp.where(r == AUG_ONE, c_hi, 0.0)
                      + jnp.where(r == AUG_ONE + 1, c_mid, 0.0)
                      + jnp.where(r == AUG_ONE + 2, c_lo, 0.0))
    qa_ref[...] = jnp.concatenate([top, jnp.concatenate(halves, axis=1).astype(BF16)], axis=0)
    ones_rows = jnp.ones((ONES_ROWS, kc), BF16)

    def q_tile(t, carry):
        q0 = i * bq + t * tq
        n_real = q0 // kc + 1
        n_iter = (n_real + S_SLOTS - 1) // S_SLOTS

        def chunk_start(c):
            return pl.multiple_of(jnp.minimum(c, n_real - 1) * kc, kc)

        def stage_q():
            for hh in range(2):
                pltpu.matmul_push_rhs(qa_ref[:, pl.ds(hh * bq + t * tq, tq)],
                                      staging_register=0, mxu_index=hh)

        def stage_a(c, slot, restage=True):
            start = chunk_start(c)
            ka = jnp.concatenate([k_ref[pl.ds(start, kc), :], caug_ref[pl.ds(start, kc), :]],
                                 axis=1)
            for hh in range(2):
                pltpu.matmul_acc_lhs(S_ADDR[slot], ka, mxu_index=hh, load_staged_rhs=0)
            if restage:
                stage_q()

        def stage_b(c, slot, masked):
            start = chunk_start(c)
            for hh in range(2):
                s = pltpu.matmul_pop(S_ADDR[slot], (kc, tq), F32, mxu_index=hh)
                if masked:
                    key = c * kc + lax.broadcasted_iota(jnp.int32, (kc, tq), 0)
                    qry = q0 + lax.broadcasted_iota(jnp.int32, (kc, tq), 1)
                    s = jnp.where(key <= qry, s, NEG_BIG)
                p = jnp.exp(s).astype(BF16)
                pltpu.matmul_push_rhs(p, staging_register=1, mxu_index=hh)
                vta = jnp.concatenate(
                    [vt_ref[hh * HEAD_DIM:(hh + 1) * HEAD_DIM, pl.ds(start, kc)], ones_rows],
                    axis=0)
                pltpu.matmul_acc_lhs(O_ADDR, vta, mxu_index=hh, load_staged_rhs=1)

        stage_q()
        stage_a(0, 0)
        stage_a(1, 1)

        def body(g, carry):
            for s in range(S_SLOTS):
                c = g * S_SLOTS + s
                stage_a(c + 2, (s + 2) % S_SLOTS)
                stage_b(c, s, False)
            return carry

        lax.fori_loop(0, n_iter - 1, body, 0)
        c_last = (n_iter - 1) * S_SLOTS
        stage_a(c_last + 2, 2, restage=False)
        for s in range(S_SLOTS):
            stage_b(c_last + s, s, True)

        outs = []
        for hh in range(2):
            o = pltpu.matmul_pop(O_ADDR, (HEAD_DIM + ONES_ROWS, tq), F32, mxu_index=hh)
            outs.append(o[:HEAD_DIM, :] / o[HEAD_DIM:HEAD_DIM + 1, :])
        ot_ref[:, t * tq:(t + 1) * tq] = jnp.concatenate(outs, axis=0)
        return carry

    for t in range(bq // tq):
        q_tile(t, 0)
    o_ref[...] = (ot_ref[...].T * _silu(z_ref[...])).astype(BF16)


def _attention_mxu(qt, k, caug, vt, c, z):
    D, S = qt.shape
    bq = STREAM_BLOCK_Q
    c3 = c.reshape(N_PAIRS, 2, S)
    return pl.pallas_call(
        _attn_mxu_kernel,
        grid=(N_PAIRS, S // bq),
        in_specs=[
            pl.BlockSpec((LANES, bq), lambda p, i: (p, i)),
            pl.BlockSpec((S, LANES), lambda p, i: (0, p)),
            pl.BlockSpec((S, LANES), lambda p, i: (0, 0)),
            pl.BlockSpec((LANES, S), lambda p, i: (p, 0)),
            pl.BlockSpec((1, 2, bq), lambda p, i: (p, 0, i)),
            pl.BlockSpec((bq, LANES), lambda p, i: (i, p)),
        ],
        out_specs=pl.BlockSpec((bq, LANES), lambda p, i: (i, p)),
        out_shape=jax.ShapeDtypeStruct((S, D), BF16),
        scratch_shapes=[pltpu.VMEM((2 * LANES, 2 * bq), BF16), pltpu.VMEM((LANES, bq), F32)],
        compiler_params=pltpu.CompilerParams(
            dimension_semantics=("arbitrary", "arbitrary"), vmem_limit_bytes=VMEM_LIMIT_BYTES),
        name="fox_attention_mxu",
    )(qt, k, caug, vt, c3, z)


def _out_proj_kernel(x_ref, a_ref, w_ref, o_ref):
    o_ref[...] = x_ref[...] + jnp.dot(a_ref[...], w_ref[...].astype(BF16),
                                      preferred_element_type=F32)


def _out_proj(x, a, w_out):
    S, D = x.shape
    T = OUT_ROW_TILE
    return pl.pallas_call(
        _out_proj_kernel,
        grid=(S // T,),
        in_specs=[
            pl.BlockSpec((T, D), lambda i: (i, 0)),
            pl.BlockSpec((T, D), lambda i: (i, 0)),
            pl.BlockSpec((D, D), lambda i: (0, 0), pipeline_mode=pl.Buffered(1)),
        ],
        out_specs=pl.BlockSpec((T, D), lambda i: (i, 0)),
        out_shape=jax.ShapeDtypeStruct((S, D), F32),
        compiler_params=pltpu.CompilerParams(
            dimension_semantics=("arbitrary",), vmem_limit_bytes=VMEM_LIMIT_BYTES),
        name="attn_out_proj",
    )(x, a, w_out)


def kernel(x, conv_norm_g, conv_w_in, conv_w, conv_w_out, attn_norm_g, attn_w_in,
           attn_b_f, attn_q_norm_g, attn_k_norm_g, attn_w_out):
    B, S, D = x.shape
    assert (B, D) == (1, D_MODEL)
    assert all(S % t == 0 for t in (BLOCK_Q, STREAM_BLOCK_Q, ROW_TILE, OUT_ROW_TILE))
    assert conv_w_in.shape[0] == 1 and attn_w_in.shape[0] == 1
    x2 = x.reshape(S, D)

    x2 = _conv_layer(x2, conv_norm_g[0].reshape(1, D), conv_w_in[0], conv_w[0], conv_w_out[0])

    qt, k, vt, z, c, caug = _attn_proj(
        x2, attn_norm_g[0].reshape(1, D), jnp.swapaxes(attn_w_in, 1, 2),
        attn_b_f[0].reshape(N_HEADS, 1),
        jnp.tile(attn_q_norm_g[0], N_HEADS).reshape(D, 1),
        jnp.tile(attn_k_norm_g[0], N_HEADS).reshape(D, 1))
    logit_bound = (math.sqrt(HEAD_DIM) * jnp.max(jnp.abs(attn_q_norm_g[0]))
                   * jnp.max(jnp.abs(attn_k_norm_g[0])))
    a = lax.cond(logit_bound <= LOGIT_BOUND_MAX,
                 lambda: _attention_mxu(qt, k, caug, vt, c, z),
                 lambda: _attention(qt.T, k, vt.T, c, z))
    out = _out_proj(x2, a, attn_w_out[0])
    return out.reshape(B, S, D)
```

```python
import math

import jax
import jax.numpy as jnp
from jax import lax
from jax.experimental import pallas as pl
from jax.experimental.pallas import tpu as pltpu

D_MODEL = 1024
HEAD_DIM = 64
N_HEADS = D_MODEL // HEAD_DIM
N_PAIRS = N_HEADS // 2
CONV_WIDTH = 3
RMS_EPS = 1e-6

LANES = 128
SUBLANES = 8
VMEM_LIMIT_BYTES = 56 * 1024 * 1024

ROW_TILE = 512
OUT_ROW_TILE = 1024
BLOCK_Q = 512
BLOCK_K = 512
NEG_BIG = -1e30
STREAM_BLOCK_Q = 2048
AUG_MID = N_HEADS
AUG_LO = 2 * N_HEADS
AUG_ONE = 3 * N_HEADS
LOGIT_BOUND_MAX = 60.0

BF16 = jnp.bfloat16
F32 = jnp.float32


def _rmsnorm_rows(x, g):
    inv = lax.rsqrt(jnp.mean(x * x, axis=-1, keepdims=True) + RMS_EPS)
    return x * inv * g


def _silu(z):
    return z * jax.nn.sigmoid(z)


def _conv_layer_kernel(x_ref, g_ref, w_in_ref, cw_ref, w_out_ref, o_ref, tail_ref):
    i = pl.program_id(0)
    D = D_MODEL
    T = x_ref.shape[0]

    @pl.when(i == 0)
    def _():
        tail_ref[...] = jnp.zeros_like(tail_ref)

    x = x_ref[...]
    h = _rmsnorm_rows(x, g_ref[...])
    proj = jnp.dot(h, w_in_ref[...], preferred_element_type=F32)
    b_g = proj[:, 0 * D:1 * D]
    c_g = proj[:, 1 * D:2 * D]
    xin = proj[:, 2 * D:3 * D]
    z = proj[:, 3 * D:4 * D]
    u = c_g * xin

    tail = tail_ref[...]
    prev1 = tail[SUBLANES - 1:SUBLANES, :]
    prev2 = tail[SUBLANES - 2:SUBLANES - 1, :]
    row = lax.broadcasted_iota(jnp.int32, (T, 1), 0)
    u1 = jnp.where(row == 0, prev1, pltpu.roll(u, 1, axis=0))
    u2 = jnp.where(row == 0, prev2, jnp.where(row == 1, prev1, pltpu.roll(u, 2, axis=0)))
    tail_ref[...] = u[T - SUBLANES:, :]

    cw = cw_ref[...]
    y = cw[0:1, :] * u2 + cw[1:2, :] * u1 + cw[2:3, :] * u
    y = b_g * y * _silu(z)
    o_ref[...] = x + jnp.dot(y, w_out_ref[...], preferred_element_type=F32)


def _conv_layer(x, g, w_in, cw, w_out):
    S, D = x.shape
    const = lambda i: (0, 0)
    return pl.pallas_call(
        _conv_layer_kernel,
        grid=(S // ROW_TILE,),
        in_specs=[
            pl.BlockSpec((ROW_TILE, D), lambda i: (i, 0)),
            pl.BlockSpec((1, D), const),
            pl.BlockSpec((D, 4 * D), const, pipeline_mode=pl.Buffered(1)),
            pl.BlockSpec((CONV_WIDTH, D), const),
            pl.BlockSpec((D, D), const, pipeline_mode=pl.Buffered(1)),
        ],
        out_specs=pl.BlockSpec((ROW_TILE, D), lambda i: (i, 0)),
        out_shape=jax.ShapeDtypeStruct((S, D), F32),
        scratch_shapes=[pltpu.VMEM((SUBLANES, D), F32)],
        compiler_params=pltpu.CompilerParams(
            dimension_semantics=("arbitrary",), vmem_limit_bytes=VMEM_LIMIT_BYTES),
        name="conv_layer",
    )(x, g, w_in, cw, w_out)


def _split3(x):
    hi = x.astype(BF16).astype(F32)
    r = x - hi
    mid = r.astype(BF16).astype(F32)
    lo = (r - mid).astype(BF16).astype(F32)
    return hi, mid, lo


def _head_rmsnorm_t(xt, g_col):
    t = xt.shape[1]
    x3 = xt.reshape(N_HEADS, HEAD_DIM, t)
    inv = lax.rsqrt(jnp.mean(x3 * x3, axis=1, keepdims=True) + RMS_EPS)
    return (x3 * inv * g_col.reshape(N_HEADS, HEAD_DIM, 1)).reshape(N_HEADS * HEAD_DIM, t)


def _attn_proj_kernel(x_ref, g_ref, wq_ref, wk_ref, wv_ref, wz_ref, wf_ref, bf_ref, qg_ref, kg_ref,
                      tri_ref,
                      qt_ref, k_ref, vt_ref, z_ref, c_ref, caug_ref, run_ref):
    i = pl.program_id(0)
    D = D_MODEL
    T = x_ref.shape[0]

    @pl.when(i == 0)
    def _():
        run_ref[...] = jnp.zeros_like(run_ref)

    trans_b = (((1,), (1,)), ((), ()))
    h = _rmsnorm_rows(x_ref[...], g_ref[...])
    z_ref[...] = lax.dot_general(h, wz_ref[...], trans_b, preferred_element_type=F32)

    qt = lax.dot_general(wq_ref[...], h, trans_b, preferred_element_type=F32)
    scale = 1.0 / math.sqrt(HEAD_DIM)
    qt_ref[...] = (_head_rmsnorm_t(qt, qg_ref[...]) * scale).astype(BF16)
    kt = lax.dot_general(wk_ref[...], h, trans_b, preferred_element_type=F32)
    k_ref[...] = _head_rmsnorm_t(kt, kg_ref[...]).T.astype(BF16)
    vt_ref[...] = lax.dot_general(wv_ref[...], h, trans_b,
                                  preferred_element_type=F32).astype(BF16)

    f = lax.dot_general(wf_ref[...], h, trans_b, preferred_element_type=F32) + bf_ref[...]
    logf = -(jnp.maximum(-f, 0.0) + jnp.log1p(jnp.exp(-jnp.abs(f))))
    pieces = jnp.concatenate(_split3(logf), axis=0).astype(BF16)
    cs = jnp.dot(pieces, tri_ref[...], preferred_element_type=F32)
    cs = cs[:N_HEADS] + cs[N_HEADS:2 * N_HEADS] + cs[2 * N_HEADS:]
    c = cs + run_ref[...][:, 0:1]
    c_ref[...] = c
    run_ref[...] = jnp.broadcast_to(c[:, T - 1:T], run_ref.shape)

    crow = jnp.concatenate([c, jnp.zeros((LANES - N_HEADS, T), F32)], axis=0).T
    hi, mid, lo = _split3(crow)
    lane = lax.broadcasted_iota(jnp.int32, (1, LANES), 1)
    ones = jnp.where((lane >= AUG_ONE) & (lane < AUG_ONE + 3), 1.0, 0.0)
    caug = hi + pltpu.roll(mid, AUG_MID, axis=1) + pltpu.roll(lo, AUG_LO, axis=1) + ones
    caug_ref[...] = caug.astype(BF16)


def _attn_proj(x, g, w_in_t, b_f, q_g_col, k_g_col):
    S, D = x.shape
    T = ROW_TILE
    tri = (jnp.arange(T)[:, None] <= jnp.arange(T)[None, :]).astype(BF16)
    const = lambda i: (0, 0)
    rows = lambda i: (i, 0)
    cols = lambda i: (0, i)
    return pl.pallas_call(
        _attn_proj_kernel,
        grid=(S // T,),
        in_specs=[
            pl.BlockSpec((T, D), rows),
            pl.BlockSpec((1, D), const),
            pl.BlockSpec((None, D, D), lambda i: (0, 0, 0), pipeline_mode=pl.Buffered(1)),
            pl.BlockSpec((None, D, D), lambda i: (0, 1, 0), pipeline_mode=pl.Buffered(1)),
            pl.BlockSpec((None, D, D), lambda i: (0, 2, 0), pipeline_mode=pl.Buffered(1)),
            pl.BlockSpec((None, D, D), lambda i: (0, 3, 0), pipeline_mode=pl.Buffered(1)),
            pl.BlockSpec((None, N_HEADS, D), lambda i: (0, 4 * D // N_HEADS, 0)),
            pl.BlockSpec((N_HEADS, 1), const),
            pl.BlockSpec((D, 1), const),
            pl.BlockSpec((D, 1), const),
            pl.BlockSpec((T, T), const),
        ],
        out_specs=[
            pl.BlockSpec((D, T), cols),
            pl.BlockSpec((T, D), rows),
            pl.BlockSpec((D, T), cols),
            pl.BlockSpec((T, D), rows),
            pl.BlockSpec((N_HEADS, T), cols),
            pl.BlockSpec((T, LANES), rows),
        ],
        out_shape=[
            jax.ShapeDtypeStruct((D, S), BF16),
            jax.ShapeDtypeStruct((S, D), BF16),
            jax.ShapeDtypeStruct((D, S), BF16),
            jax.ShapeDtypeStruct((S, D), F32),
            jax.ShapeDtypeStruct((N_HEADS, S), F32),
            jax.ShapeDtypeStruct((S, LANES), BF16),
        ],
        scratch_shapes=[pltpu.VMEM((N_HEADS, LANES), F32)],
        compiler_params=pltpu.CompilerParams(
            dimension_semantics=("arbitrary",), vmem_limit_bytes=VMEM_LIMIT_BYTES),
        name="attn_proj",
    )(x, g, w_in_t, w_in_t, w_in_t, w_in_t, w_in_t, b_f, q_g_col, k_g_col, tri)


def _attn_kernel(q_ref, k_ref, v_ref, c_ref, z_ref, o_ref, m_ref, l_ref, acc_ref):
    i = pl.program_id(1)
    bq, bk = BLOCK_Q, BLOCK_K
    q2 = q_ref[...]
    lane = lax.broadcasted_iota(jnp.int32, (1, LANES), 1)
    zero = jnp.zeros_like(q2)
    q_heads = (jnp.where(lane < HEAD_DIM, q2, zero), jnp.where(lane >= HEAD_DIM, q2, zero))

    m_ref[...] = jnp.full_like(m_ref, NEG_BIG)
    l_ref[...] = jnp.zeros_like(l_ref)
    acc_ref[...] = jnp.zeros_like(acc_ref)

    def step(j, masked):
        start = pl.multiple_of(j * bk, bk)
        kb = k_ref[pl.ds(start, bk), :]
        vb = v_ref[pl.ds(start, bk), :]
        cb = c_ref[0, :, pl.ds(start, bk)]
        for hh in range(2):
            s = lax.dot_general(q_heads[hh], kb, (((1,), (1,)), ((), ())),
                                preferred_element_type=F32)
            s = s - cb[hh:hh + 1, :]
            if masked:
                row = lax.broadcasted_iota(jnp.int32, (bq, bk), 0)
                col = lax.broadcasted_iota(jnp.int32, (bq, bk), 1)
                s = jnp.where(col <= row, s, NEG_BIG)
            m_old = m_ref[hh]
            m_new = jnp.maximum(m_old, jnp.max(s, axis=-1, keepdims=True))
            alpha = jnp.exp(m_old - m_new)
            p = jnp.exp(s - m_new)
            l_ref[hh] = alpha * l_ref[hh] + jnp.sum(p, axis=-1, keepdims=True)
            acc_ref[hh] = alpha * acc_ref[hh] + jnp.dot(
                p.astype(BF16), vb, preferred_element_type=F32)
            m_ref[hh] = m_new

    def body(j, carry):
        step(j, masked=False)
        return carry

    lax.fori_loop(0, i, body, 0)
    step(i, masked=True)

    o0 = acc_ref[0] / l_ref[0]
    o1 = acc_ref[1] / l_ref[1]
    o = jnp.where(lane < HEAD_DIM, o0, o1)
    o_ref[...] = (o * _silu(z_ref[...])).astype(BF16)


def _attention(q, k, v, c, z):
    S, D = q.shape
    bq = BLOCK_Q
    c3 = c.reshape(N_PAIRS, 2, S)
    return pl.pallas_call(
        _attn_kernel,
        grid=(N_PAIRS, S // bq),
        in_specs=[
            pl.BlockSpec((bq, LANES), lambda p, i: (i, p)),
            pl.BlockSpec((S, LANES), lambda p, i: (0, p)),
            pl.BlockSpec((S, LANES), lambda p, i: (0, p)),
            pl.BlockSpec((1, 2, S), lambda p, i: (p, 0, 0)),
            pl.BlockSpec((bq, LANES), lambda p, i: (i, p)),
        ],
        out_specs=pl.BlockSpec((bq, LANES), lambda p, i: (i, p)),
        out_shape=jax.ShapeDtypeStruct((S, D), BF16),
        scratch_shapes=[
            pltpu.VMEM((2, bq, 1), F32),
            pltpu.VMEM((2, bq, 1), F32),
            pltpu.VMEM((2, bq, LANES), F32),
        ],
        compiler_params=pltpu.CompilerParams(
            dimension_semantics=("arbitrary", "arbitrary"), vmem_limit_bytes=VMEM_LIMIT_BYTES),
        name="fox_attention",
    )(q, k, v, c3, z)


MXU_TILE = 256
S_SLOTS = 3
S_TILE_ENTRIES = MXU_TILE * MXU_TILE // (SUBLANES * LANES)
S_ADDR = tuple(s * S_TILE_ENTRIES for s in range(S_SLOTS))
O_ADDR = S_SLOTS * S_TILE_ENTRIES
ONES_ROWS = 16


def _attn_mxu_kernel(qt_ref, k_ref, caug_ref, vt_ref, c_ref, z_ref, o_ref, qa_ref, ot_ref):
    pair = pl.program_id(0)
    i = pl.program_id(1)
    bq, tq, kc = STREAM_BLOCK_Q, MXU_TILE, MXU_TILE

    qt = qt_ref[...]
    r = lax.broadcasted_iota(jnp.int32, (LANES, 1), 0)
    zero = jnp.zeros_like(qt)
    top = jnp.concatenate([jnp.where(r < HEAD_DIM, qt, zero),
                           jnp.where(r >= HEAD_DIM, qt, zero)], axis=1)
    ci = c_ref[0]
    halves = []
    for hh in range(2):
        head = 2 * pair + hh
        c_hi, c_mid, c_lo = _split3(ci[hh:hh + 1, :])
        minus_cj = jnp.where((r == head) | (r == AUG_MID + head) | (r == AUG_LO + head), -1.0, 0.0)
        halves.append(minus_cj + jnp.where(r == AUG_ONE, c_hi, 0.0)
                      + jnp.where(r == AUG_ONE + 1, c_mid, 0.0)
                      + jnp.where(r == AUG_ONE + 2, c_lo, 0.0))
    qa_ref[...] = jnp.concatenate([top, jnp.concatenate(halves, axis=1).astype(BF16)], axis=0)
    ones_rows = jnp.ones((ONES_ROWS, kc), BF16)

    def q_tile(t, carry):
        q0 = i * bq + t * tq
        n_real = q0 // kc + 1
        n_iter = (n_real + S_SLOTS - 1) // S_SLOTS

        def chunk_start(c):
            return pl.multiple_of(jnp.minimum(c, n_real - 1) * kc, kc)

        def stage_q():
            for hh in range(2):
                pltpu.matmul_push_rhs(qa_ref[:, pl.ds(hh * bq + t * tq, tq)],
                                      staging_register=0, mxu_index=hh)

        def stage_a(c, slot, restage=True):
            start = chunk_start(c)
            ka = jnp.concatenate([k_ref[pl.ds(start, kc), :], caug_ref[pl.ds(start, kc), :]],
                                 axis=1)
            for hh in range(2):
                pltpu.matmul_acc_lhs(S_ADDR[slot], ka, mxu_index=hh, load_staged_rhs=0)
            if restage:
                stage_q()

        def stage_b(c, slot, masked):
            start = chunk_start(c)
            for hh in range(2):
                s = pltpu.matmul_pop(S_ADDR[slot], (kc, tq), F32, mxu_index=hh)
                if masked:
                    key = c * kc + lax.broadcasted_iota(jnp.int32, (kc, tq), 0)
                    qry = q0 + lax.broadcasted_iota(jnp.int32, (kc, tq), 1)
                    s = jnp.where(key <= qry, s, NEG_BIG)
                p = jnp.exp(s).astype(BF16)
                pltpu.matmul_push_rhs(p, staging_register=1, mxu_index=hh)
                vta = jnp.concatenate(
                    [vt_ref[hh * HEAD_DIM:(hh + 1) * HEAD_DIM, pl.ds(start, kc)], ones_rows],
                    axis=0)
                pltpu.matmul_acc_lhs(O_ADDR, vta, mxu_index=hh, load_staged_rhs=1)

        stage_q()
        stage_a(0, 0)
        stage_a(1, 1)

        def body(g, carry):
            for s in range(S_SLOTS):
                c = g * S_SLOTS + s
                stage_a(c + 2, (s + 2) % S_SLOTS)
                stage_b(c, s, False)
            return carry

        lax.fori_loop(0, n_iter - 1, body, 0)
        c_last = (n_iter - 1) * S_SLOTS
        stage_a(c_last + 2, 2, restage=False)
        for s in range(S_SLOTS):
            stage_b(c_last + s, s, True)

        outs = []
        for hh in range(2):
            o = pltpu.matmul_pop(O_ADDR, (HEAD_DIM + ONES_ROWS, tq), F32, mxu_index=hh)
            outs.append(o[:HEAD_DIM, :] / o[HEAD_DIM:HEAD_DIM + 1, :])
        ot_ref[:, t * tq:(t + 1) * tq] = jnp.concatenate(outs, axis=0)
        return carry

    for t in range(bq // tq):
        q_tile(t, 0)
    o_ref[...] = (ot_ref[...].T * _silu(z_ref[...])).astype(BF16)


def _attention_mxu(qt, k, caug, vt, c, z):
    D, S = qt.shape
    bq = STREAM_BLOCK_Q
    c3 = c.reshape(N_PAIRS, 2, S)
    return pl.pallas_call(
        _attn_mxu_kernel,
        grid=(N_PAIRS, S // bq),
        in_specs=[
            pl.BlockSpec((LANES, bq), lambda p, i: (p, i)),
            pl.BlockSpec((S, LANES), lambda p, i: (0, p)),
            pl.BlockSpec((S, LANES), lambda p, i: (0, 0)),
            pl.BlockSpec((LANES, S), lambda p, i: (p, 0)),
            pl.BlockSpec((1, 2, bq), lambda p, i: (p, 0, i)),
            pl.BlockSpec((bq, LANES), lambda p, i: (i, p)),
        ],
        out_specs=pl.BlockSpec((bq, LANES), lambda p, i: (i, p)),
        out_shape=jax.ShapeDtypeStruct((S, D), BF16),
        scratch_shapes=[pltpu.VMEM((2 * LANES, 2 * bq), BF16), pltpu.VMEM((LANES, bq), F32)],
        compiler_params=pltpu.CompilerParams(
            dimension_semantics=("arbitrary", "arbitrary"), vmem_limit_bytes=VMEM_LIMIT_BYTES),
        name="fox_attention_mxu",
    )(qt, k, caug, vt, c3, z)


def _out_proj_kernel(x_ref, a_ref, w_ref, o_ref):
    o_ref[...] = x_ref[...] + jnp.dot(a_ref[...], w_ref[...].astype(BF16),
                                      preferred_element_type=F32)


def _out_proj(x, a, w_out):
    S, D = x.shape
    T = OUT_ROW_TILE
    return pl.pallas_call(
        _out_proj_kernel,
        grid=(S // T,),
        in_specs=[
            pl.BlockSpec((T, D), lambda i: (i, 0)),
            pl.BlockSpec((T, D), lambda i: (i, 0)),
            pl.BlockSpec((D, D), lambda i: (0, 0), pipeline_mode=pl.Buffered(1)),
        ],
        out_specs=pl.BlockSpec((T, D), lambda i: (i, 0)),
        out_shape=jax.ShapeDtypeStruct((S, D), F32),
        compiler_params=pltpu.CompilerParams(
            dimension_semantics=("arbitrary",), vmem_limit_bytes=VMEM_LIMIT_BYTES),
        name="attn_out_proj",
    )(x, a, w_out)


def kernel(x, conv_norm_g, conv_w_in, conv_w, conv_w_out, attn_norm_g, attn_w_in,
           attn_b_f, attn_q_norm_g, attn_k_norm_g, attn_w_out):
    B, S, D = x.shape
    assert (B, D) == (1, D_MODEL)
    assert all(S % t == 0 for t in (BLOCK_Q, STREAM_BLOCK_Q, ROW_TILE, OUT_ROW_TILE))
    assert conv_w_in.shape[0] == 1 and attn_w_in.shape[0] == 1
    x2 = x.reshape(S, D)

    x2 = _conv_layer(x2, conv_norm_g[0].reshape(1, D), conv_w_in[0], conv_w[0], conv_w_out[0])

    qt, k, vt, z, c, caug = _attn_proj(
        x2, attn_norm_g[0].reshape(1, D), jnp.swapaxes(attn_w_in, 1, 2),
        attn_b_f[0].reshape(N_HEADS, 1),
        jnp.tile(attn_q_norm_g[0], N_HEADS).reshape(D, 1),
        jnp.tile(attn_k_norm_g[0], N_HEADS).reshape(D, 1))
    logit_bound = (math.sqrt(HEAD_DIM) * jnp.max(jnp.abs(attn_q_norm_g[0]))
                   * jnp.max(jnp.abs(attn_k_norm_g[0])))
    a = lax.cond(logit_bound <= LOGIT_BOUND_MAX,
                 lambda: _attention_mxu(qt, k, caug, vt, c, z),
                 lambda: _attention(qt.T, k, vt.T, c, z))
    out = _out_proj(x2, a, attn_w_out[0])
    return out.reshape(B, S, D)
```

```python
import math

import jax
import jax.numpy as jnp
from jax import lax
from jax.experimental import pallas as pl
from jax.experimental.pallas import tpu as pltpu

D_MODEL = 1024
HEAD_DIM = 64
N_HEADS = D_MODEL // HEAD_DIM
N_PAIRS = N_HEADS // 2
CONV_WIDTH = 3
RMS_EPS = 1e-6
LOG2E = math.log2(math.e)

LANES = 128
SUBLANES = 8
VMEM_LIMIT_BYTES = 56 * 1024 * 1024

ROW_TILE = 512
OUT_ROW_TILE = 1024
BLOCK_Q = 512
BLOCK_K = 512
NEG_BIG = -1e30
STREAM_BLOCK_Q = 2048
AUG_MID = N_HEADS
AUG_LO = 2 * N_HEADS
AUG_ONE = 3 * N_HEADS
LOGIT_BOUND_MAX = 60.0

BF16 = jnp.bfloat16
F32 = jnp.float32


def _rmsnorm_rows(x, g):
    inv = lax.rsqrt(jnp.mean(x * x, axis=-1, keepdims=True) + RMS_EPS)
    return x * inv * g


def _silu(z):
    return z * jax.nn.sigmoid(z)


def _conv_layer_kernel(x_ref, g_ref, w_in_ref, cw_ref, w_out_ref, o_ref, tail_ref):
    i = pl.program_id(0)
    D = D_MODEL
    T = x_ref.shape[0]

    @pl.when(i == 0)
    def _():
        tail_ref[...] = jnp.zeros_like(tail_ref)

    x = x_ref[...]
    h = _rmsnorm_rows(x, g_ref[...])
    proj = jnp.dot(h, w_in_ref[...], preferred_element_type=F32)
    b_g = proj[:, 0 * D:1 * D]
    c_g = proj[:, 1 * D:2 * D]
    xin = proj[:, 2 * D:3 * D]
    z = proj[:, 3 * D:4 * D]
    u = c_g * xin

    tail = tail_ref[...]
    prev1 = tail[SUBLANES - 1:SUBLANES, :]
    prev2 = tail[SUBLANES - 2:SUBLANES - 1, :]
    row = lax.broadcasted_iota(jnp.int32, (T, 1), 0)
    u1 = jnp.where(row == 0, prev1, pltpu.roll(u, 1, axis=0))
    u2 = jnp.where(row == 0, prev2, jnp.where(row == 1, prev1, pltpu.roll(u, 2, axis=0)))
    tail_ref[...] = u[T - SUBLANES:, :]

    cw = cw_ref[...]
    y = cw[0:1, :] * u2 + cw[1:2, :] * u1 + cw[2:3, :] * u
    y = b_g * y * _silu(z)
    o_ref[...] = x + jnp.dot(y, w_out_ref[...], preferred_element_type=F32)


def _conv_layer(x, g, w_in, cw, w_out):
    S, D = x.shape
    const = lambda i: (0, 0)
    return pl.pallas_call(
        _conv_layer_kernel,
        grid=(S // ROW_TILE,),
        in_specs=[
            pl.BlockSpec((ROW_TILE, D), lambda i: (i, 0)),
            pl.BlockSpec((1, D), const),
            pl.BlockSpec((D, 4 * D), const, pipeline_mode=pl.Buffered(1)),
            pl.BlockSpec((CONV_WIDTH, D), const),
            pl.BlockSpec((D, D), const, pipeline_mode=pl.Buffered(1)),
        ],
        out_specs=pl.BlockSpec((ROW_TILE, D), lambda i: (i, 0)),
        out_shape=jax.ShapeDtypeStruct((S, D), F32),
        scratch_shapes=[pltpu.VMEM((SUBLANES, D), F32)],
        compiler_params=pltpu.CompilerParams(
            dimension_semantics=("arbitrary",), vmem_limit_bytes=VMEM_LIMIT_BYTES),
        name="conv_layer",
    )(x, g, w_in, cw, w_out)


def _split3(x):
    hi = x.astype(BF16).astype(F32)
    r = x - hi
    mid = r.astype(BF16).astype(F32)
    lo = (r - mid).astype(BF16).astype(F32)
    return hi, mid, lo


def _head_rmsnorm_t(xt, g_col):
    t = xt.shape[1]
    x3 = xt.reshape(N_HEADS, HEAD_DIM, t)
    inv = lax.rsqrt(jnp.mean(x3 * x3, axis=1, keepdims=True) + RMS_EPS)
    return (x3 * inv * g_col.reshape(N_HEADS, HEAD_DIM, 1)).reshape(N_HEADS * HEAD_DIM, t)


def _attn_proj_kernel(x_ref, g_ref, wq_ref, wk_ref, wv_ref, wz_ref, wf_ref, bf_ref, qg_ref, kg_ref,
                      tri_ref,
                      qt_ref, k_ref, vt_ref, z_ref, c_ref, caug_ref, run_ref):
    i = pl.program_id(0)
    D = D_MODEL
    T = x_ref.shape[0]

    @pl.when(i == 0)
    def _():
        run_ref[...] = jnp.zeros_like(run_ref)

    trans_b = (((1,), (1,)), ((), ()))
    h = _rmsnorm_rows(x_ref[...], g_ref[...])
    z_ref[...] = lax.dot_general(h, wz_ref[...], trans_b, preferred_element_type=F32)

    qt = lax.dot_general(wq_ref[...], h, trans_b, preferred_element_type=F32)
    scale = LOG2E / math.sqrt(HEAD_DIM)
    qt_ref[...] = (_head_rmsnorm_t(qt, qg_ref[...]) * scale).astype(BF16)
    kt = lax.dot_general(wk_ref[...], h, trans_b, preferred_element_type=F32)
    k_ref[...] = _head_rmsnorm_t(kt, kg_ref[...]).T.astype(BF16)
    vt_ref[...] = lax.dot_general(wv_ref[...], h, trans_b,
                                  preferred_element_type=F32).astype(BF16)

    f = lax.dot_general(wf_ref[...], h, trans_b, preferred_element_type=F32) + bf_ref[...]
    logf = -(jnp.maximum(-f, 0.0) + jnp.log1p(jnp.exp(-jnp.abs(f))))
    pieces = jnp.concatenate(_split3(logf), axis=0).astype(BF16)
    cs = jnp.dot(pieces, tri_ref[...], preferred_element_type=F32)
    cs = cs[:N_HEADS] + cs[N_HEADS:2 * N_HEADS] + cs[2 * N_HEADS:]
    c = cs + run_ref[...][:, 0:1]
    c_ref[...] = c
    run_ref[...] = jnp.broadcast_to(c[:, T - 1:T], run_ref.shape)

    crow = jnp.concatenate([c, jnp.zeros((LANES - N_HEADS, T), F32)], axis=0).T
    hi, mid, lo = _split3(crow * LOG2E)
    lane = lax.broadcasted_iota(jnp.int32, (1, LANES), 1)
    ones = jnp.where((lane >= AUG_ONE) & (lane < AUG_ONE + 3), 1.0, 0.0)
    caug = hi + pltpu.roll(mid, AUG_MID, axis=1) + pltpu.roll(lo, AUG_LO, axis=1) + ones
    caug_ref[...] = caug.astype(BF16)


def _attn_proj(x, g, w_in_t, b_f, q_g_col, k_g_col):
    S, D = x.shape
    T = ROW_TILE
    tri = (jnp.arange(T)[:, None] <= jnp.arange(T)[None, :]).astype(BF16)
    const = lambda i: (0, 0)
    rows = lambda i: (i, 0)
    cols = lambda i: (0, i)
    return pl.pallas_call(
        _attn_proj_kernel,
        grid=(S // T,),
        in_specs=[
            pl.BlockSpec((T, D), rows),
            pl.BlockSpec((1, D), const),
            pl.BlockSpec((None, D, D), lambda i: (0, 0, 0), pipeline_mode=pl.Buffered(1)),
            pl.BlockSpec((None, D, D), lambda i: (0, 1, 0), pipeline_mode=pl.Buffered(1)),
            pl.BlockSpec((None, D, D), lambda i: (0, 2, 0), pipeline_mode=pl.Buffered(1)),
            pl.BlockSpec((None, D, D), lambda i: (0, 3, 0), pipeline_mode=pl.Buffered(1)),
            pl.BlockSpec((None, N_HEADS, D), lambda i: (0, 4 * D // N_HEADS, 0)),
            pl.BlockSpec((N_HEADS, 1), const),
            pl.BlockSpec((D, 1), const),
            pl.BlockSpec((D, 1), const),
            pl.BlockSpec((T, T), const),
        ],
        out_specs=[
            pl.BlockSpec((D, T), cols),
            pl.BlockSpec((T, D), rows),
            pl.BlockSpec((D, T), cols),
            pl.BlockSpec((T, D), rows),
            pl.BlockSpec((N_HEADS, T), cols),
            pl.BlockSpec((T, LANES), rows),
        ],
        out_shape=[
            jax.ShapeDtypeStruct((D, S), BF16),
            jax.ShapeDtypeStruct((S, D), BF16),
            jax.ShapeDtypeStruct((D, S), BF16),
            jax.ShapeDtypeStruct((S, D), F32),
            jax.ShapeDtypeStruct((N_HEADS, S), F32),
            jax.ShapeDtypeStruct((S, LANES), BF16),
        ],
        scratch_shapes=[pltpu.VMEM((N_HEADS, LANES), F32)],
        compiler_params=pltpu.CompilerParams(
            dimension_semantics=("arbitrary",), vmem_limit_bytes=VMEM_LIMIT_BYTES),
        name="attn_proj",
    )(x, g, w_in_t, w_in_t, w_in_t, w_in_t, w_in_t, b_f, q_g_col, k_g_col, tri)


def _attn_kernel(q_ref, k_ref, v_ref, c_ref, z_ref, o_ref, m_ref, l_ref, acc_ref):
    i = pl.program_id(1)
    bq, bk = BLOCK_Q, BLOCK_K
    q2 = q_ref[...]
    lane = lax.broadcasted_iota(jnp.int32, (1, LANES), 1)
    zero = jnp.zeros_like(q2)
    q_heads = (jnp.where(lane < HEAD_DIM, q2, zero), jnp.where(lane >= HEAD_DIM, q2, zero))

    m_ref[...] = jnp.full_like(m_ref, NEG_BIG)
    l_ref[...] = jnp.zeros_like(l_ref)
    acc_ref[...] = jnp.zeros_like(acc_ref)

    def step(j, masked):
        start = pl.multiple_of(j * bk, bk)
        kb = k_ref[pl.ds(start, bk), :]
        vb = v_ref[pl.ds(start, bk), :]
        cb = c_ref[0, :, pl.ds(start, bk)] * LOG2E
        for hh in range(2):
            s = lax.dot_general(q_heads[hh], kb, (((1,), (1,)), ((), ())),
                                preferred_element_type=F32)
            s = s - cb[hh:hh + 1, :]
            if masked:
                row = lax.broadcasted_iota(jnp.int32, (bq, bk), 0)
                col = lax.broadcasted_iota(jnp.int32, (bq, bk), 1)
                s = jnp.where(col <= row, s, NEG_BIG)
            m_old = m_ref[hh]
            m_new = jnp.maximum(m_old, jnp.max(s, axis=-1, keepdims=True))
            alpha = jnp.exp2(m_old - m_new)
            p = jnp.exp2(s - m_new)
            l_ref[hh] = alpha * l_ref[hh] + jnp.sum(p, axis=-1, keepdims=True)
            acc_ref[hh] = alpha * acc_ref[hh] + jnp.dot(
                p.astype(BF16), vb, preferred_element_type=F32)
            m_ref[hh] = m_new

    def body(j, carry):
        step(j, masked=False)
        return carry

    lax.fori_loop(0, i, body, 0)
    step(i, masked=True)

    o0 = acc_ref[0] / l_ref[0]
    o1 = acc_ref[1] / l_ref[1]
    o = jnp.where(lane < HEAD_DIM, o0, o1)
    o_ref[...] = (o * _silu(z_ref[...])).astype(BF16)


def _attention(q, k, v, c, z):
    S, D = q.shape
    bq = BLOCK_Q
    c3 = c.reshape(N_PAIRS, 2, S)
    return pl.pallas_call(
        _attn_kernel,
        grid=(N_PAIRS, S // bq),
        in_specs=[
            pl.BlockSpec((bq, LANES), lambda p, i: (i, p)),
            pl.BlockSpec((S, LANES), lambda p, i: (0, p)),
            pl.BlockSpec((S, LANES), lambda p, i: (0, p)),
            pl.BlockSpec((1, 2, S), lambda p, i: (p, 0, 0)),
            pl.BlockSpec((bq, LANES), lambda p, i: (i, p)),
        ],
        out_specs=pl.BlockSpec((bq, LANES), lambda p, i: (i, p)),
        out_shape=jax.ShapeDtypeStruct((S, D), BF16),
        scratch_shapes=[
            pltpu.VMEM((2, bq, 1), F32),
            pltpu.VMEM((2, bq, 1), F32),
            pltpu.VMEM((2, bq, LANES), F32),
        ],
        compiler_params=pltpu.CompilerParams(
            dimension_semantics=("arbitrary", "arbitrary"), vmem_limit_bytes=VMEM_LIMIT_BYTES),
        name="fox_attention",
    )(q, k, v, c3, z)


MXU_TILE = 256
S_SLOTS = 3
S_TILE_ENTRIES = MXU_TILE * MXU_TILE // (SUBLANES * LANES)
S_ADDR = tuple(s * S_TILE_ENTRIES for s in range(S_SLOTS))
O_ADDR = S_SLOTS * S_TILE_ENTRIES
ONES_ROWS = 16


def _attn_mxu_kernel(qt_ref, k_ref, caug_ref, vt_ref, c_ref, z_ref, o_ref, qa_ref, ot_ref):
    pair = pl.program_id(0)
    i = pl.program_id(1)
    bq, tq, kc = STREAM_BLOCK_Q, MXU_TILE, MXU_TILE

    qt = qt_ref[...]
    r = lax.broadcasted_iota(jnp.int32, (LANES, 1), 0)
    zero = jnp.zeros_like(qt)
    top = jnp.concatenate([jnp.where(r < HEAD_DIM, qt, zero),
                           jnp.where(r >= HEAD_DIM, qt, zero)], axis=1)
    ci = c_ref[0] * LOG2E
    halves = []
    for hh in range(2):
        head = 2 * pair + hh
        c_hi, c_mid, c_lo = _split3(ci[hh:hh + 1, :])
        minus_cj = jnp.where((r == head) | (r == AUG_MID + head) | (r == AUG_LO + head), -1.0, 0.0)
        halves.append(minus_cj + jnp.where(r == AUG_ONE, c_hi, 0.0)
                      + jnp.where(r == AUG_ONE + 1, c_mid, 0.0)
                      + jnp.where(r == AUG_ONE + 2, c_lo, 0.0))
    qa_ref[...] = jnp.concatenate([top, jnp.concatenate(halves, axis=1).astype(BF16)], axis=0)
    ones_rows = jnp.ones((ONES_ROWS, kc), BF16)

    def q_tile(t, carry):
        q0 = i * bq + t * tq
        n_real = q0 // kc + 1
        n_iter = (n_real + S_SLOTS - 1) // S_SLOTS

        def chunk_start(c):
            return pl.multiple_of(jnp.minimum(c, n_real - 1) * kc, kc)

        def stage_q():
            for hh in range(2):
                pltpu.matmul_push_rhs(qa_ref[:, pl.ds(hh * bq + t * tq, tq)],
                                      staging_register=0, mxu_index=hh)

        def stage_a(c, slot, restage=True):
            start = chunk_start(c)
            ka = jnp.concatenate([k_ref[pl.ds(start, kc), :], caug_ref[pl.ds(start, kc), :]],
                                 axis=1)
            for hh in range(2):
                pltpu.matmul_acc_lhs(S_ADDR[slot], ka, mxu_index=hh, load_staged_rhs=0)
            if restage:
                stage_q()

        def stage_b(c, slot, masked):
            start = chunk_start(c)
            for hh in range(2):
                s = pltpu.matmul_pop(S_ADDR[slot], (kc, tq), F32, mxu_index=hh)
                if masked:
                    key = c * kc + lax.broadcasted_iota(jnp.int32, (kc, tq), 0)
                    qry = q0 + lax.broadcasted_iota(jnp.int32, (kc, tq), 1)
                    s = jnp.where(key <= qry, s, NEG_BIG)
                p = jnp.exp2(s).astype(BF16)
                pltpu.matmul_push_rhs(p, staging_register=1, mxu_index=hh)
                vta = jnp.concatenate(
                    [vt_ref[hh * HEAD_DIM:(hh + 1) * HEAD_DIM, pl.ds(start, kc)], ones_rows],
                    axis=0)
                pltpu.matmul_acc_lhs(O_ADDR, vta, mxu_index=hh, load_staged_rhs=1)

        stage_q()
        stage_a(0, 0)
        stage_a(1, 1)

        def body(g, carry):
            for s in range(S_SLOTS):
                c = g * S_SLOTS + s
                stage_a(c + 2, (s + 2) % S_SLOTS)
                stage_b(c, s, False)
            return carry

        lax.fori_loop(0, n_iter - 1, body, 0)
        c_last = (n_iter - 1) * S_SLOTS
        stage_a(c_last + 2, 2, restage=False)
        for s in range(S_SLOTS):
            stage_b(c_last + s, s, True)

        outs = []
        for hh in range(2):
            o = pltpu.matmul_pop(O_ADDR, (HEAD_DIM + ONES_ROWS, tq), F32, mxu_index=hh)
            outs.append(o[:HEAD_DIM, :] / o[HEAD_DIM:HEAD_DIM + 1, :])
        ot_ref[:, t * tq:(t + 1) * tq] = jnp.concatenate(outs, axis=0)
        return carry

    for t in range(bq // tq):
        q_tile(t, 0)
    o_ref[...] = (ot_ref[...].T * _silu(z_ref[...])).astype(BF16)


def _attention_mxu(qt, k, caug, vt, c, z):
    D, S = qt.shape
    bq = STREAM_BLOCK_Q
    c3 = c.reshape(N_PAIRS, 2, S)
    return pl.pallas_call(
        _attn_mxu_kernel,
        grid=(N_PAIRS, S // bq),
        in_specs=[
            pl.BlockSpec((LANES, bq), lambda p, i: (p, i)),
            pl.BlockSpec((S, LANES), lambda p, i: (0, p)),
            pl.BlockSpec((S, LANES), lambda p, i: (0, 0)),
            pl.BlockSpec((LANES, S), lambda p, i: (p, 0)),
            pl.BlockSpec((1, 2, bq), lambda p, i: (p, 0, i)),
            pl.BlockSpec((bq, LANES), lambda p, i: (i, p)),
        ],
        out_specs=pl.BlockSpec((bq, LANES), lambda p, i: (i, p)),
        out_shape=jax.ShapeDtypeStruct((S, D), BF16),
        scratch_shapes=[pltpu.VMEM((2 * LANES, 2 * bq), BF16), pltpu.VMEM((LANES, bq), F32)],
        compiler_params=pltpu.CompilerParams(
            dimension_semantics=("arbitrary", "arbitrary"), vmem_limit_bytes=VMEM_LIMIT_BYTES),
        name="fox_attention_mxu",
    )(qt, k, caug, vt, c3, z)


def _out_proj_kernel(x_ref, a_ref, w_ref, o_ref):
    o_ref[...] = x_ref[...] + jnp.dot(a_ref[...], w_ref[...].astype(BF16),
                                      preferred_element_type=F32)


def _out_proj(x, a, w_out):
    S, D = x.shape
    T = OUT_ROW_TILE
    return pl.pallas_call(
        _out_proj_kernel,
        grid=(S // T,),
        in_specs=[
            pl.BlockSpec((T, D), lambda i: (i, 0)),
            pl.BlockSpec((T, D), lambda i: (i, 0)),
            pl.BlockSpec((D, D), lambda i: (0, 0), pipeline_mode=pl.Buffered(1)),
        ],
        out_specs=pl.BlockSpec((T, D), lambda i: (i, 0)),
        out_shape=jax.ShapeDtypeStruct((S, D), F32),
        compiler_params=pltpu.CompilerParams(
            dimension_semantics=("arbitrary",), vmem_limit_bytes=VMEM_LIMIT_BYTES),
        name="attn_out_proj",
    )(x, a, w_out)


def kernel(x, conv_norm_g, conv_w_in, conv_w, conv_w_out, attn_norm_g, attn_w_in,
           attn_b_f, attn_q_norm_g, attn_k_norm_g, attn_w_out):
    B, S, D = x.shape
    assert (B, D) == (1, D_MODEL)
    assert all(S % t == 0 for t in (BLOCK_Q, STREAM_BLOCK_Q, ROW_TILE, OUT_ROW_TILE))
    assert conv_w_in.shape[0] == 1 and attn_w_in.shape[0] == 1
    x2 = x.reshape(S, D)

    x2 = _conv_layer(x2, conv_norm_g[0].reshape(1, D), conv_w_in[0], conv_w[0], conv_w_out[0])

    qt, k, vt, z, c, caug = _attn_proj(
        x2, attn_norm_g[0].reshape(1, D), jnp.swapaxes(attn_w_in, 1, 2),
        attn_b_f[0].reshape(N_HEADS, 1),
        jnp.tile(attn_q_norm_g[0], N_HEADS).reshape(D, 1),
        jnp.tile(attn_k_norm_g[0], N_HEADS).reshape(D, 1))
    logit_bound = (math.sqrt(HEAD_DIM) * jnp.max(jnp.abs(attn_q_norm_g[0]))
                   * jnp.max(jnp.abs(attn_k_norm_g[0])))
    a = lax.cond(logit_bound <= LOGIT_BOUND_MAX,
                 lambda: _attention_mxu(qt, k, caug, vt, c, z),
                 lambda: _attention(qt.T, k, vt.T, c, z))
    out = _out_proj(x2, a, attn_w_out[0])
    return out.reshape(B, S, D)
```

```python
import math

import jax
import jax.numpy as jnp
from jax import lax
from jax.experimental import pallas as pl
from jax.experimental.pallas import tpu as pltpu

D_MODEL = 1024
HEAD_DIM = 64
N_HEADS = D_MODEL // HEAD_DIM
N_PAIRS = N_HEADS // 2
CONV_WIDTH = 3
RMS_EPS = 1e-6
LOG2E = math.log2(math.e)

LANES = 128
SUBLANES = 8
VMEM_LIMIT_BYTES = 56 * 1024 * 1024

ROW_TILE = 512
OUT_ROW_TILE = 1024
BLOCK_Q = 512
BLOCK_K = 512
NEG_BIG = -1e30
STREAM_BLOCK_Q = 2048
AUG_MID = N_HEADS
AUG_LO = 2 * N_HEADS
AUG_ONE = 3 * N_HEADS
ONES_ROWS = 16
V_ROWS = HEAD_DIM + ONES_ROWS
LOGIT_BOUND_MAX = 60.0

BF16 = jnp.bfloat16
F32 = jnp.float32


def _rmsnorm_rows(x, g):
    inv = lax.rsqrt(jnp.mean(x * x, axis=-1, keepdims=True) + RMS_EPS)
    return x * inv * g


def _silu(z):
    return z * jax.nn.sigmoid(z)


def _conv_layer_kernel(x_ref, g_ref, w_in_ref, cw_ref, w_out_ref, o_ref, tail_ref):
    i = pl.program_id(0)
    D = D_MODEL
    T = x_ref.shape[0]

    @pl.when(i == 0)
    def _():
        tail_ref[...] = jnp.zeros_like(tail_ref)

    x = x_ref[...]
    h = _rmsnorm_rows(x, g_ref[...])
    proj = jnp.dot(h, w_in_ref[...], preferred_element_type=F32)
    b_g = proj[:, 0 * D:1 * D]
    c_g = proj[:, 1 * D:2 * D]
    xin = proj[:, 2 * D:3 * D]
    z = proj[:, 3 * D:4 * D]
    u = c_g * xin

    tail = tail_ref[...]
    prev1 = tail[SUBLANES - 1:SUBLANES, :]
    prev2 = tail[SUBLANES - 2:SUBLANES - 1, :]
    row = lax.broadcasted_iota(jnp.int32, (T, 1), 0)
    u1 = jnp.where(row == 0, prev1, pltpu.roll(u, 1, axis=0))
    u2 = jnp.where(row == 0, prev2, jnp.where(row == 1, prev1, pltpu.roll(u, 2, axis=0)))
    tail_ref[...] = u[T - SUBLANES:, :]

    cw = cw_ref[...]
    y = cw[0:1, :] * u2 + cw[1:2, :] * u1 + cw[2:3, :] * u
    y = b_g * y * _silu(z)
    o_ref[...] = x + jnp.dot(y, w_out_ref[...], preferred_element_type=F32)


def _conv_layer(x, g, w_in, cw, w_out):
    S, D = x.shape
    const = lambda i: (0, 0)
    return pl.pallas_call(
        _conv_layer_kernel,
        grid=(S // ROW_TILE,),
        in_specs=[
            pl.BlockSpec((ROW_TILE, D), lambda i: (i, 0)),
            pl.BlockSpec((1, D), const),
            pl.BlockSpec((D, 4 * D), const, pipeline_mode=pl.Buffered(1)),
            pl.BlockSpec((CONV_WIDTH, D), const),
            pl.BlockSpec((D, D), const, pipeline_mode=pl.Buffered(1)),
        ],
        out_specs=pl.BlockSpec((ROW_TILE, D), lambda i: (i, 0)),
        out_shape=jax.ShapeDtypeStruct((S, D), F32),
        scratch_shapes=[pltpu.VMEM((SUBLANES, D), F32)],
        compiler_params=pltpu.CompilerParams(
            dimension_semantics=("arbitrary",), vmem_limit_bytes=VMEM_LIMIT_BYTES),
        name="conv_layer",
    )(x, g, w_in, cw, w_out)


def _split3(x):
    hi = x.astype(BF16).astype(F32)
    r = x - hi
    mid = r.astype(BF16).astype(F32)
    lo = (r - mid).astype(BF16).astype(F32)
    return hi, mid, lo


def _head_rmsnorm_t(xt, g_col):
    t = xt.shape[1]
    x3 = xt.reshape(N_HEADS, HEAD_DIM, t)
    inv = lax.rsqrt(jnp.mean(x3 * x3, axis=1, keepdims=True) + RMS_EPS)
    return (x3 * inv * g_col.reshape(N_HEADS, HEAD_DIM, 1)).reshape(N_HEADS * HEAD_DIM, t)


def _attn_proj_kernel(x_ref, g_ref, wq_ref, wk_ref, wv_ref, wz_ref, wf_ref, bf_ref, qg_ref, kg_ref,
                      tri_ref,
                      qt_ref, k_ref, vt_ref, z_ref, c_ref, caug_ref, run_ref):
    i = pl.program_id(0)
    D = D_MODEL
    T = x_ref.shape[0]

    @pl.when(i == 0)
    def _():
        run_ref[...] = jnp.zeros_like(run_ref)

    trans_b = (((1,), (1,)), ((), ()))
    h = _rmsnorm_rows(x_ref[...], g_ref[...])
    z_ref[...] = lax.dot_general(h, wz_ref[...], trans_b, preferred_element_type=F32)

    qt = lax.dot_general(wq_ref[...], h, trans_b, preferred_element_type=F32)
    scale = LOG2E / math.sqrt(HEAD_DIM)
    qt_ref[...] = (_head_rmsnorm_t(qt, qg_ref[...]) * scale).astype(BF16)
    kt = lax.dot_general(wk_ref[...], h, trans_b, preferred_element_type=F32)
    k_ref[...] = _head_rmsnorm_t(kt, kg_ref[...]).T.astype(BF16)
    vt = lax.dot_general(wv_ref[...], h, trans_b, preferred_element_type=F32)
    vta = jnp.concatenate([vt.reshape(N_HEADS, HEAD_DIM, T),
                           jnp.ones((N_HEADS, ONES_ROWS, T), F32)], axis=1)
    vt_ref[...] = vta.reshape(N_HEADS * V_ROWS, T).astype(BF16)

    f = lax.dot_general(wf_ref[...], h, trans_b, preferred_element_type=F32) + bf_ref[...]
    logf = -(jnp.maximum(-f, 0.0) + jnp.log1p(jnp.exp(-jnp.abs(f))))
    pieces = jnp.concatenate(_split3(logf), axis=0).astype(BF16)
    cs = jnp.dot(pieces, tri_ref[...], preferred_element_type=F32)
    cs = cs[:N_HEADS] + cs[N_HEADS:2 * N_HEADS] + cs[2 * N_HEADS:]
    c = cs + run_ref[...][:, 0:1]
    c_ref[...] = c
    run_ref[...] = jnp.broadcast_to(c[:, T - 1:T], run_ref.shape)

    crow = jnp.concatenate([c, jnp.zeros((LANES - N_HEADS, T), F32)], axis=0).T
    hi, mid, lo = _split3(crow * LOG2E)
    lane = lax.broadcasted_iota(jnp.int32, (1, LANES), 1)
    ones = jnp.where((lane >= AUG_ONE) & (lane < AUG_ONE + 3), 1.0, 0.0)
    caug = hi + pltpu.roll(mid, AUG_MID, axis=1) + pltpu.roll(lo, AUG_LO, axis=1) + ones
    caug_ref[...] = caug.astype(BF16)


def _attn_proj(x, g, w_in_t, b_f, q_g_col, k_g_col):
    S, D = x.shape
    T = ROW_TILE
    tri = (jnp.arange(T)[:, None] <= jnp.arange(T)[None, :]).astype(BF16)
    const = lambda i: (0, 0)
    rows = lambda i: (i, 0)
    cols = lambda i: (0, i)
    return pl.pallas_call(
        _attn_proj_kernel,
        grid=(S // T,),
        in_specs=[
            pl.BlockSpec((T, D), rows),
            pl.BlockSpec((1, D), const),
            pl.BlockSpec((None, D, D), lambda i: (0, 0, 0), pipeline_mode=pl.Buffered(1)),
            pl.BlockSpec((None, D, D), lambda i: (0, 1, 0), pipeline_mode=pl.Buffered(1)),
            pl.BlockSpec((None, D, D), lambda i: (0, 2, 0), pipeline_mode=pl.Buffered(1)),
            pl.BlockSpec((None, D, D), lambda i: (0, 3, 0), pipeline_mode=pl.Buffered(1)),
            pl.BlockSpec((None, N_HEADS, D), lambda i: (0, 4 * D // N_HEADS, 0)),
            pl.BlockSpec((N_HEADS, 1), const),
            pl.BlockSpec((D, 1), const),
            pl.BlockSpec((D, 1), const),
            pl.BlockSpec((T, T), const),
        ],
        out_specs=[
            pl.BlockSpec((D, T), cols),
            pl.BlockSpec((T, D), rows),
            pl.BlockSpec((N_HEADS * V_ROWS, T), cols),
            pl.BlockSpec((T, D), rows),
            pl.BlockSpec((N_HEADS, T), cols),
            pl.BlockSpec((T, LANES), rows),
        ],
        out_shape=[
            jax.ShapeDtypeStruct((D, S), BF16),
            jax.ShapeDtypeStruct((S, D), BF16),
            jax.ShapeDtypeStruct((N_HEADS * V_ROWS, S), BF16),
            jax.ShapeDtypeStruct((S, D), F32),
            jax.ShapeDtypeStruct((N_HEADS, S), F32),
            jax.ShapeDtypeStruct((S, LANES), BF16),
        ],
        scratch_shapes=[pltpu.VMEM((N_HEADS, LANES), F32)],
        compiler_params=pltpu.CompilerParams(
            dimension_semantics=("arbitrary",), vmem_limit_bytes=VMEM_LIMIT_BYTES),
        name="attn_proj",
    )(x, g, w_in_t, w_in_t, w_in_t, w_in_t, w_in_t, b_f, q_g_col, k_g_col, tri)


def _attn_kernel(q_ref, k_ref, v_ref, c_ref, z_ref, o_ref, m_ref, l_ref, acc_ref):
    i = pl.program_id(1)
    bq, bk = BLOCK_Q, BLOCK_K
    q2 = q_ref[...]
    lane = lax.broadcasted_iota(jnp.int32, (1, LANES), 1)
    zero = jnp.zeros_like(q2)
    q_heads = (jnp.where(lane < HEAD_DIM, q2, zero), jnp.where(lane >= HEAD_DIM, q2, zero))

    m_ref[...] = jnp.full_like(m_ref, NEG_BIG)
    l_ref[...] = jnp.zeros_like(l_ref)
    acc_ref[...] = jnp.zeros_like(acc_ref)

    def step(j, masked):
        start = pl.multiple_of(j * bk, bk)
        kb = k_ref[pl.ds(start, bk), :]
        vb = v_ref[pl.ds(start, bk), :]
        cb = c_ref[0, :, pl.ds(start, bk)] * LOG2E
        for hh in range(2):
            s = lax.dot_general(q_heads[hh], kb, (((1,), (1,)), ((), ())),
                                preferred_element_type=F32)
            s = s - cb[hh:hh + 1, :]
            if masked:
                row = lax.broadcasted_iota(jnp.int32, (bq, bk), 0)
                col = lax.broadcasted_iota(jnp.int32, (bq, bk), 1)
                s = jnp.where(col <= row, s, NEG_BIG)
            m_old = m_ref[hh]
            m_new = jnp.maximum(m_old, jnp.max(s, axis=-1, keepdims=True))
            alpha = jnp.exp2(m_old - m_new)
            p = jnp.exp2(s - m_new)
            l_ref[hh] = alpha * l_ref[hh] + jnp.sum(p, axis=-1, keepdims=True)
            acc_ref[hh] = alpha * acc_ref[hh] + jnp.dot(
                p.astype(BF16), vb, preferred_element_type=F32)
            m_ref[hh] = m_new

    def body(j, carry):
        step(j, masked=False)
        return carry

    lax.fori_loop(0, i, body, 0)
    step(i, masked=True)

    o0 = acc_ref[0] / l_ref[0]
    o1 = acc_ref[1] / l_ref[1]
    o = jnp.where(lane < HEAD_DIM, o0, o1)
    o_ref[...] = (o * _silu(z_ref[...])).astype(BF16)


def _attention(q, k, v, c, z):
    S, D = q.shape
    bq = BLOCK_Q
    c3 = c.reshape(N_PAIRS, 2, S)
    return pl.pallas_call(
        _attn_kernel,
        grid=(N_PAIRS, S // bq),
        in_specs=[
            pl.BlockSpec((bq, LANES), lambda p, i: (i, p)),
            pl.BlockSpec((S, LANES), lambda p, i: (0, p)),
            pl.BlockSpec((S, LANES), lambda p, i: (0, p)),
            pl.BlockSpec((1, 2, S), lambda p, i: (p, 0, 0)),
            pl.BlockSpec((bq, LANES), lambda p, i: (i, p)),
        ],
        out_specs=pl.BlockSpec((bq, LANES), lambda p, i: (i, p)),
        out_shape=jax.ShapeDtypeStruct((S, D), BF16),
        scratch_shapes=[
            pltpu.VMEM((2, bq, 1), F32),
            pltpu.VMEM((2, bq, 1), F32),
            pltpu.VMEM((2, bq, LANES), F32),
        ],
        compiler_params=pltpu.CompilerParams(
            dimension_semantics=("arbitrary", "arbitrary"), vmem_limit_bytes=VMEM_LIMIT_BYTES),
        name="fox_attention",
    )(q, k, v, c3, z)


MXU_TILE = 256
S_SLOTS = 3
S_TILE_ENTRIES = MXU_TILE * MXU_TILE // (SUBLANES * LANES)
S_ADDR = tuple(s * S_TILE_ENTRIES for s in range(S_SLOTS))
O_ADDR = S_SLOTS * S_TILE_ENTRIES


def _attn_mxu_kernel(qt_ref, k_ref, caug_ref, vt_ref, c_ref, z_ref, o_ref, qa_ref, ot_ref):
    pair = pl.program_id(0)
    i = pl.program_id(1)
    bq, tq, kc = STREAM_BLOCK_Q, MXU_TILE, MXU_TILE

    qt = qt_ref[...]
    r = lax.broadcasted_iota(jnp.int32, (LANES, 1), 0)
    zero = jnp.zeros_like(qt)
    top = jnp.concatenate([jnp.where(r < HEAD_DIM, qt, zero),
                           jnp.where(r >= HEAD_DIM, qt, zero)], axis=1)
    ci = c_ref[0] * LOG2E
    halves = []
    for hh in range(2):
        head = 2 * pair + hh
        c_hi, c_mid, c_lo = _split3(ci[hh:hh + 1, :])
        minus_cj = jnp.where((r == head) | (r == AUG_MID + head) | (r == AUG_LO + head), -1.0, 0.0)
        halves.append(minus_cj + jnp.where(r == AUG_ONE, c_hi, 0.0)
                      + jnp.where(r == AUG_ONE + 1, c_mid, 0.0)
                      + jnp.where(r == AUG_ONE + 2, c_lo, 0.0))
    qa_ref[...] = jnp.concatenate([top, jnp.concatenate(halves, axis=1).astype(BF16)], axis=0)

    def q_tile(t, carry):
        q0 = i * bq + t * tq
        n_real = q0 // kc + 1
        n_iter = (n_real + S_SLOTS - 1) // S_SLOTS

        def chunk_start(c):
            return pl.multiple_of(jnp.minimum(c, n_real - 1) * kc, kc)

        def stage_q():
            for hh in range(2):
                pltpu.matmul_push_rhs(qa_ref[:, pl.ds(hh * bq + t * tq, tq)],
                                      staging_register=0, mxu_index=hh)

        def stage_a(c, slot, restage=True):
            start = chunk_start(c)
            ka = jnp.concatenate([k_ref[pl.ds(start, kc), :], caug_ref[pl.ds(start, kc), :]],
                                 axis=1)
            for hh in range(2):
                pltpu.matmul_acc_lhs(S_ADDR[slot], ka, mxu_index=hh, load_staged_rhs=0)
            if restage:
                stage_q()

        def stage_b(c, slot, masked):
            start = chunk_start(c)
            for hh in range(2):
                s = pltpu.matmul_pop(S_ADDR[slot], (kc, tq), F32, mxu_index=hh)
                if masked:
                    key = c * kc + lax.broadcasted_iota(jnp.int32, (kc, tq), 0)
                    qry = q0 + lax.broadcasted_iota(jnp.int32, (kc, tq), 1)
                    s = jnp.where(key <= qry, s, NEG_BIG)
                p = jnp.exp2(s).astype(BF16)
                pltpu.matmul_push_rhs(p, staging_register=1, mxu_index=hh)
                pltpu.matmul_acc_lhs(O_ADDR, vt_ref[hh * V_ROWS:(hh + 1) * V_ROWS, pl.ds(start, kc)],
                                     mxu_index=hh, load_staged_rhs=1)

        stage_q()
        stage_a(0, 0)
        stage_a(1, 1)

        def body(g, carry):
            for s in range(S_SLOTS):
                c = g * S_SLOTS + s
                stage_a(c + 2, (s + 2) % S_SLOTS)
                stage_b(c, s, False)
            return carry

        lax.fori_loop(0, n_iter - 1, body, 0)
        c_last = (n_iter - 1) * S_SLOTS
        stage_a(c_last + 2, 2, restage=False)
        for s in range(S_SLOTS):
            stage_b(c_last + s, s, True)

        outs = []
        for hh in range(2):
            o = pltpu.matmul_pop(O_ADDR, (V_ROWS, tq), F32, mxu_index=hh)
            outs.append(o[:HEAD_DIM, :] / o[HEAD_DIM:HEAD_DIM + 1, :])
        ot_ref[:, t * tq:(t + 1) * tq] = jnp.concatenate(outs, axis=0)
        return carry

    for t in range(bq // tq):
        q_tile(t, 0)
    o_ref[...] = (ot_ref[...].T * _silu(z_ref[...])).astype(BF16)


def _attention_mxu(qt, k, caug, vt, c, z):
    D, S = qt.shape
    bq = STREAM_BLOCK_Q
    c3 = c.reshape(N_PAIRS, 2, S)
    return pl.pallas_call(
        _attn_mxu_kernel,
        grid=(N_PAIRS, S // bq),
        in_specs=[
            pl.BlockSpec((LANES, bq), lambda p, i: (p, i)),
            pl.BlockSpec((S, LANES), lambda p, i: (0, p)),
            pl.BlockSpec((S, LANES), lambda p, i: (0, 0)),
            pl.BlockSpec((2 * V_ROWS, S), lambda p, i: (p, 0)),
            pl.BlockSpec((1, 2, bq), lambda p, i: (p, 0, i)),
            pl.BlockSpec((bq, LANES), lambda p, i: (i, p)),
        ],
        out_specs=pl.BlockSpec((bq, LANES), lambda p, i: (i, p)),
        out_shape=jax.ShapeDtypeStruct((S, D), BF16),
        scratch_shapes=[pltpu.VMEM((2 * LANES, 2 * bq), BF16), pltpu.VMEM((LANES, bq), F32)],
        compiler_params=pltpu.CompilerParams(
            dimension_semantics=("arbitrary", "arbitrary"), vmem_limit_bytes=VMEM_LIMIT_BYTES),
        name="fox_attention_mxu",
    )(qt, k, caug, vt, c3, z)


def _out_proj_kernel(x_ref, a_ref, w_ref, o_ref):
    o_ref[...] = x_ref[...] + jnp.dot(a_ref[...], w_ref[...].astype(BF16),
                                      preferred_element_type=F32)


def _out_proj(x, a, w_out):
    S, D = x.shape
    T = OUT_ROW_TILE
    return pl.pallas_call(
        _out_proj_kernel,
        grid=(S // T,),
        in_specs=[
            pl.BlockSpec((T, D), lambda i: (i, 0)),
            pl.BlockSpec((T, D), lambda i: (i, 0)),
            pl.BlockSpec((D, D), lambda i: (0, 0), pipeline_mode=pl.Buffered(1)),
        ],
        out_specs=pl.BlockSpec((T, D), lambda i: (i, 0)),
        out_shape=jax.ShapeDtypeStruct((S, D), F32),
        compiler_params=pltpu.CompilerParams(
            dimension_semantics=("arbitrary",), vmem_limit_bytes=VMEM_LIMIT_BYTES),
        name="attn_out_proj",
    )(x, a, w_out)


def kernel(x, conv_norm_g, conv_w_in, conv_w, conv_w_out, attn_norm_g, attn_w_in,
           attn_b_f, attn_q_norm_g, attn_k_norm_g, attn_w_out):
    B, S, D = x.shape
    assert (B, D) == (1, D_MODEL)
    assert all(S % t == 0 for t in (BLOCK_Q, STREAM_BLOCK_Q, ROW_TILE, OUT_ROW_TILE))
    assert conv_w_in.shape[0] == 1 and attn_w_in.shape[0] == 1
    x2 = x.reshape(S, D)

    x2 = _conv_layer(x2, conv_norm_g[0].reshape(1, D), conv_w_in[0], conv_w[0], conv_w_out[0])

    qt, k, vt, z, c, caug = _attn_proj(
        x2, attn_norm_g[0].reshape(1, D), jnp.swapaxes(attn_w_in, 1, 2),
        attn_b_f[0].reshape(N_HEADS, 1),
        jnp.tile(attn_q_norm_g[0], N_HEADS).reshape(D, 1),
        jnp.tile(attn_k_norm_g[0], N_HEADS).reshape(D, 1))
    logit_bound = (math.sqrt(HEAD_DIM) * jnp.max(jnp.abs(attn_q_norm_g[0]))
                   * jnp.max(jnp.abs(attn_k_norm_g[0])))
    a = lax.cond(logit_bound <= LOGIT_BOUND_MAX,
                 lambda: _attention_mxu(qt, k, caug, vt, c, z),
                 lambda: _attention(
                     qt.T, k, vt.reshape(N_HEADS, V_ROWS, S)[:, :HEAD_DIM].reshape(D, S).T, c, z))
    out = _out_proj(x2, a, attn_w_out[0])
    return out.reshape(B, S, D)
```

```python
import math

import jax
import jax.numpy as jnp
from jax import lax
from jax.experimental import pallas as pl
from jax.experimental.pallas import tpu as pltpu

D_MODEL = 1024
HEAD_DIM = 64
N_HEADS = D_MODEL // HEAD_DIM
N_PAIRS = N_HEADS // 2
CONV_WIDTH = 3
RMS_EPS = 1e-6
LOG2E = math.log2(math.e)

LANES = 128
MXU_TILE = 256
SUBLANES = 8
VMEM_LIMIT_BYTES = 56 * 1024 * 1024

ROW_TILE = 512
OUT_ROW_TILE = 1024
BLOCK_Q = 512
BLOCK_K = 512
NEG_BIG = -1e30
STREAM_BLOCK_Q = 2048
AUG_MID = N_HEADS
AUG_LO = 2 * N_HEADS
AUG_ONE = 3 * N_HEADS
ONES_ROWS = 16
V_ROWS = HEAD_DIM + ONES_ROWS
LOGIT_BOUND_MAX = 60.0

BF16 = jnp.bfloat16
F32 = jnp.float32


def _rmsnorm_rows(x, g):
    inv = lax.rsqrt(jnp.mean(x * x, axis=-1, keepdims=True) + RMS_EPS)
    return x * inv * g


def _silu(z):
    return z * jax.nn.sigmoid(z)


def _conv_layer_kernel(x_ref, g_ref, w_in_ref, cw_ref, w_out_ref, o_ref, tail_ref):
    i = pl.program_id(0)
    D = D_MODEL
    T = x_ref.shape[0]

    @pl.when(i == 0)
    def _():
        tail_ref[...] = jnp.zeros_like(tail_ref)

    x = x_ref[...]
    h = _rmsnorm_rows(x, g_ref[...])
    proj = jnp.dot(h, w_in_ref[...], preferred_element_type=F32)
    b_g = proj[:, 0 * D:1 * D]
    c_g = proj[:, 1 * D:2 * D]
    xin = proj[:, 2 * D:3 * D]
    z = proj[:, 3 * D:4 * D]
    u = c_g * xin

    tail = tail_ref[...]
    prev1 = tail[SUBLANES - 1:SUBLANES, :]
    prev2 = tail[SUBLANES - 2:SUBLANES - 1, :]
    row = lax.broadcasted_iota(jnp.int32, (T, 1), 0)
    u1 = jnp.where(row == 0, prev1, pltpu.roll(u, 1, axis=0))
    u2 = jnp.where(row == 0, prev2, jnp.where(row == 1, prev1, pltpu.roll(u, 2, axis=0)))
    tail_ref[...] = u[T - SUBLANES:, :]

    cw = cw_ref[...]
    y = cw[0:1, :] * u2 + cw[1:2, :] * u1 + cw[2:3, :] * u
    y = b_g * y * _silu(z)
    o_ref[...] = x + jnp.dot(y, w_out_ref[...], preferred_element_type=F32)


def _conv_layer(x, g, w_in, cw, w_out):
    S, D = x.shape
    const = lambda i: (0, 0)
    return pl.pallas_call(
        _conv_layer_kernel,
        grid=(S // ROW_TILE,),
        in_specs=[
            pl.BlockSpec((ROW_TILE, D), lambda i: (i, 0)),
            pl.BlockSpec((1, D), const),
            pl.BlockSpec((D, 4 * D), const, pipeline_mode=pl.Buffered(1)),
            pl.BlockSpec((CONV_WIDTH, D), const),
            pl.BlockSpec((D, D), const, pipeline_mode=pl.Buffered(1)),
        ],
        out_specs=pl.BlockSpec((ROW_TILE, D), lambda i: (i, 0)),
        out_shape=jax.ShapeDtypeStruct((S, D), F32),
        scratch_shapes=[pltpu.VMEM((SUBLANES, D), F32)],
        compiler_params=pltpu.CompilerParams(
            dimension_semantics=("arbitrary",), vmem_limit_bytes=VMEM_LIMIT_BYTES),
        name="conv_layer",
    )(x, g, w_in, cw, w_out)


def _split3(x):
    hi = x.astype(BF16).astype(F32)
    r = x - hi
    mid = r.astype(BF16).astype(F32)
    lo = (r - mid).astype(BF16).astype(F32)
    return hi, mid, lo


def _head_rmsnorm_t(xt, g_col):
    t = xt.shape[1]
    x3 = xt.reshape(N_HEADS, HEAD_DIM, t)
    inv = lax.rsqrt(jnp.mean(x3 * x3, axis=1, keepdims=True) + RMS_EPS)
    return (x3 * inv * g_col.reshape(N_HEADS, HEAD_DIM, 1)).reshape(N_HEADS * HEAD_DIM, t)


def _attn_proj_kernel(x_ref, g_ref, wq_ref, wk_ref, wv_ref, wz_ref, wf_ref, bf_ref, qg_ref, kg_ref,
                      tri_ref,
                      qt_ref, k_ref, vt_ref, z_ref, c_ref, caug_ref, run_ref):
    i = pl.program_id(0)
    D = D_MODEL
    T = x_ref.shape[0]

    @pl.when(i == 0)
    def _():
        run_ref[...] = jnp.zeros_like(run_ref)

    trans_b = (((1,), (1,)), ((), ()))
    h = _rmsnorm_rows(x_ref[...], g_ref[...])
    z_ref[...] = lax.dot_general(h, wz_ref[...], trans_b, preferred_element_type=F32)

    qt = lax.dot_general(wq_ref[...], h, trans_b, preferred_element_type=F32)
    scale = LOG2E / math.sqrt(HEAD_DIM)
    qt_ref[...] = (_head_rmsnorm_t(qt, qg_ref[...]) * scale).astype(BF16)
    kt = lax.dot_general(wk_ref[...], h, trans_b, preferred_element_type=F32)
    k_ref[...] = _head_rmsnorm_t(kt, kg_ref[...]).T.astype(BF16)
    vt = lax.dot_general(wv_ref[...], h, trans_b, preferred_element_type=F32)
    vta = jnp.concatenate([vt.reshape(N_HEADS, HEAD_DIM, T),
                           jnp.ones((N_HEADS, ONES_ROWS, T), F32)], axis=1)
    vta = vta.reshape(N_HEADS * V_ROWS, T).astype(BF16)
    for j in range(T // MXU_TILE):
        vt_ref[j] = vta[:, j * MXU_TILE:(j + 1) * MXU_TILE]

    f = lax.dot_general(wf_ref[...], h, trans_b, preferred_element_type=F32) + bf_ref[...]
    logf = -(jnp.maximum(-f, 0.0) + jnp.log1p(jnp.exp(-jnp.abs(f))))
    pieces = jnp.concatenate(_split3(logf), axis=0).astype(BF16)
    cs = jnp.dot(pieces, tri_ref[...], preferred_element_type=F32)
    cs = cs[:N_HEADS] + cs[N_HEADS:2 * N_HEADS] + cs[2 * N_HEADS:]
    c = cs + run_ref[...][:, 0:1]
    c_ref[...] = c
    run_ref[...] = jnp.broadcast_to(c[:, T - 1:T], run_ref.shape)

    crow = jnp.concatenate([c, jnp.zeros((LANES - N_HEADS, T), F32)], axis=0).T
    hi, mid, lo = _split3(crow * LOG2E)
    lane = lax.broadcasted_iota(jnp.int32, (1, LANES), 1)
    ones = jnp.where((lane >= AUG_ONE) & (lane < AUG_ONE + 3), 1.0, 0.0)
    caug = hi + pltpu.roll(mid, AUG_MID, axis=1) + pltpu.roll(lo, AUG_LO, axis=1) + ones
    caug_ref[...] = caug.astype(BF16)


def _attn_proj(x, g, w_in_t, b_f, q_g_col, k_g_col):
    S, D = x.shape
    T = ROW_TILE
    tri = (jnp.arange(T)[:, None] <= jnp.arange(T)[None, :]).astype(BF16)
    const = lambda i: (0, 0)
    rows = lambda i: (i, 0)
    cols = lambda i: (0, i)
    return pl.pallas_call(
        _attn_proj_kernel,
        grid=(S // T,),
        in_specs=[
            pl.BlockSpec((T, D), rows),
            pl.BlockSpec((1, D), const),
            pl.BlockSpec((None, D, D), lambda i: (0, 0, 0), pipeline_mode=pl.Buffered(1)),
            pl.BlockSpec((None, D, D), lambda i: (0, 1, 0), pipeline_mode=pl.Buffered(1)),
            pl.BlockSpec((None, D, D), lambda i: (0, 2, 0), pipeline_mode=pl.Buffered(1)),
            pl.BlockSpec((None, D, D), lambda i: (0, 3, 0), pipeline_mode=pl.Buffered(1)),
            pl.BlockSpec((None, N_HEADS, D), lambda i: (0, 4 * D // N_HEADS, 0)),
            pl.BlockSpec((N_HEADS, 1), const),
            pl.BlockSpec((D, 1), const),
            pl.BlockSpec((D, 1), const),
            pl.BlockSpec((T, T), const),
        ],
        out_specs=[
            pl.BlockSpec((D, T), cols),
            pl.BlockSpec((T, D), rows),
            pl.BlockSpec((T // MXU_TILE, N_HEADS * V_ROWS, MXU_TILE), lambda i: (i, 0, 0)),
            pl.BlockSpec((T, D), rows),
            pl.BlockSpec((N_HEADS, T), cols),
            pl.BlockSpec((T, LANES), rows),
        ],
        out_shape=[
            jax.ShapeDtypeStruct((D, S), BF16),
            jax.ShapeDtypeStruct((S, D), BF16),
            jax.ShapeDtypeStruct((S // MXU_TILE, N_HEADS * V_ROWS, MXU_TILE), BF16),
            jax.ShapeDtypeStruct((S, D), F32),
            jax.ShapeDtypeStruct((N_HEADS, S), F32),
            jax.ShapeDtypeStruct((S, LANES), BF16),
        ],
        scratch_shapes=[pltpu.VMEM((N_HEADS, LANES), F32)],
        compiler_params=pltpu.CompilerParams(
            dimension_semantics=("arbitrary",), vmem_limit_bytes=VMEM_LIMIT_BYTES),
        name="attn_proj",
    )(x, g, w_in_t, w_in_t, w_in_t, w_in_t, w_in_t, b_f, q_g_col, k_g_col, tri)


def _attn_kernel(q_ref, k_ref, v_ref, c_ref, z_ref, o_ref, m_ref, l_ref, acc_ref):
    i = pl.program_id(1)
    bq, bk = BLOCK_Q, BLOCK_K
    q2 = q_ref[...]
    lane = lax.broadcasted_iota(jnp.int32, (1, LANES), 1)
    zero = jnp.zeros_like(q2)
    q_heads = (jnp.where(lane < HEAD_DIM, q2, zero), jnp.where(lane >= HEAD_DIM, q2, zero))

    m_ref[...] = jnp.full_like(m_ref, NEG_BIG)
    l_ref[...] = jnp.zeros_like(l_ref)
    acc_ref[...] = jnp.zeros_like(acc_ref)

    def step(j, masked):
        start = pl.multiple_of(j * bk, bk)
        kb = k_ref[pl.ds(start, bk), :]
        vb = v_ref[pl.ds(start, bk), :]
        cb = c_ref[0, :, pl.ds(start, bk)] * LOG2E
        for hh in range(2):
            s = lax.dot_general(q_heads[hh], kb, (((1,), (1,)), ((), ())),
                                preferred_element_type=F32)
            s = s - cb[hh:hh + 1, :]
            if masked:
                row = lax.broadcasted_iota(jnp.int32, (bq, bk), 0)
                col = lax.broadcasted_iota(jnp.int32, (bq, bk), 1)
                s = jnp.where(col <= row, s, NEG_BIG)
            m_old = m_ref[hh]
            m_new = jnp.maximum(m_old, jnp.max(s, axis=-1, keepdims=True))
            alpha = jnp.exp2(m_old - m_new)
            p = jnp.exp2(s - m_new)
            l_ref[hh] = alpha * l_ref[hh] + jnp.sum(p, axis=-1, keepdims=True)
            acc_ref[hh] = alpha * acc_ref[hh] + jnp.dot(
                p.astype(BF16), vb, preferred_element_type=F32)
            m_ref[hh] = m_new

    def body(j, carry):
        step(j, masked=False)
        return carry

    lax.fori_loop(0, i, body, 0)
    step(i, masked=True)

    o0 = acc_ref[0] / l_ref[0]
    o1 = acc_ref[1] / l_ref[1]
    o = jnp.where(lane < HEAD_DIM, o0, o1)
    o_ref[...] = (o * _silu(z_ref[...])).astype(BF16)


def _attention(q, k, v, c, z):
    S, D = q.shape
    bq = BLOCK_Q
    c3 = c.reshape(N_PAIRS, 2, S)
    return pl.pallas_call(
        _attn_kernel,
        grid=(N_PAIRS, S // bq),
        in_specs=[
            pl.BlockSpec((bq, LANES), lambda p, i: (i, p)),
            pl.BlockSpec((S, LANES), lambda p, i: (0, p)),
            pl.BlockSpec((S, LANES), lambda p, i: (0, p)),
            pl.BlockSpec((1, 2, S), lambda p, i: (p, 0, 0)),
            pl.BlockSpec((bq, LANES), lambda p, i: (i, p)),
        ],
        out_specs=pl.BlockSpec((bq, LANES), lambda p, i: (i, p)),
        out_shape=jax.ShapeDtypeStruct((S, D), BF16),
        scratch_shapes=[
            pltpu.VMEM((2, bq, 1), F32),
            pltpu.VMEM((2, bq, 1), F32),
            pltpu.VMEM((2, bq, LANES), F32),
        ],
        compiler_params=pltpu.CompilerParams(
            dimension_semantics=("arbitrary", "arbitrary"), vmem_limit_bytes=VMEM_LIMIT_BYTES),
        name="fox_attention",
    )(q, k, v, c3, z)


S_SLOTS = 3
S_TILE_ENTRIES = MXU_TILE * MXU_TILE // (SUBLANES * LANES)
S_ADDR = tuple(s * S_TILE_ENTRIES for s in range(S_SLOTS))
O_ADDR = S_SLOTS * S_TILE_ENTRIES


def _attn_mxu_kernel(qt_ref, k_ref, caug_ref, vt_ref, c_ref, z_ref, o_ref, qa_ref, ot_ref):
    pair = pl.program_id(0)
    i = pl.program_id(1)
    bq, tq, kc = STREAM_BLOCK_Q, MXU_TILE, MXU_TILE

    qt = qt_ref[...]
    r = lax.broadcasted_iota(jnp.int32, (LANES, 1), 0)
    zero = jnp.zeros_like(qt)
    top = jnp.concatenate([jnp.where(r < HEAD_DIM, qt, zero),
                           jnp.where(r >= HEAD_DIM, qt, zero)], axis=1)
    ci = c_ref[0] * LOG2E
    halves = []
    for hh in range(2):
        head = 2 * pair + hh
        c_hi, c_mid, c_lo = _split3(ci[hh:hh + 1, :])
        minus_cj = jnp.where((r == head) | (r == AUG_MID + head) | (r == AUG_LO + head), -1.0, 0.0)
        halves.append(minus_cj + jnp.where(r == AUG_ONE, c_hi, 0.0)
                      + jnp.where(r == AUG_ONE + 1, c_mid, 0.0)
                      + jnp.where(r == AUG_ONE + 2, c_lo, 0.0))
    qa_ref[...] = jnp.concatenate([top, jnp.concatenate(halves, axis=1).astype(BF16)], axis=0)

    def q_tile(t, carry):
        q0 = i * bq + t * tq
        n_real = q0 // kc + 1
        n_iter = (n_real + S_SLOTS - 1) // S_SLOTS

        def chunk_index(c):
            return jnp.minimum(c, n_real - 1)

        def stage_q():
            for hh in range(2):
                pltpu.matmul_push_rhs(qa_ref[:, pl.ds(hh * bq + t * tq, tq)],
                                      staging_register=0, mxu_index=hh)

        def stage_a(c, slot, restage=True):
            start = pl.multiple_of(chunk_index(c) * kc, kc)
            ka = jnp.concatenate([k_ref[pl.ds(start, kc), :], caug_ref[pl.ds(start, kc), :]],
                                 axis=1)
            for hh in range(2):
                pltpu.matmul_acc_lhs(S_ADDR[slot], ka, mxu_index=hh, load_staged_rhs=0)
            if restage:
                stage_q()

        def stage_b(c, slot, masked):
            chunk = chunk_index(c)
            for hh in range(2):
                s = pltpu.matmul_pop(S_ADDR[slot], (kc, tq), F32, mxu_index=hh)
                if masked:
                    key = c * kc + lax.broadcasted_iota(jnp.int32, (kc, tq), 0)
                    qry = q0 + lax.broadcasted_iota(jnp.int32, (kc, tq), 1)
                    s = jnp.where(key <= qry, s, NEG_BIG)
                p = jnp.exp2(s).astype(BF16)
                pltpu.matmul_push_rhs(p, staging_register=1, mxu_index=hh)
                pltpu.matmul_acc_lhs(O_ADDR, vt_ref[chunk, hh * V_ROWS:(hh + 1) * V_ROWS, :],
                                     mxu_index=hh, load_staged_rhs=1)

        stage_q()
        stage_a(0, 0)
        stage_a(1, 1)

        def body(g, carry):
            for s in range(S_SLOTS):
                c = g * S_SLOTS + s
                stage_a(c + 2, (s + 2) % S_SLOTS)
                stage_b(c, s, False)
            return carry

        lax.fori_loop(0, n_iter - 1, body, 0)
        c_last = (n_iter - 1) * S_SLOTS
        stage_a(c_last + 2, 2, restage=False)
        for s in range(S_SLOTS):
            stage_b(c_last + s, s, True)

        outs = []
        for hh in range(2):
            o = pltpu.matmul_pop(O_ADDR, (V_ROWS, tq), F32, mxu_index=hh)
            outs.append(o[:HEAD_DIM, :] / o[HEAD_DIM:HEAD_DIM + 1, :])
        ot_ref[:, t * tq:(t + 1) * tq] = jnp.concatenate(outs, axis=0)
        return carry

    for t in range(bq // tq):
        q_tile(t, 0)
    o_ref[...] = (ot_ref[...].T * _silu(z_ref[...])).astype(BF16)


def _attention_mxu(qt, k, caug, vt, c, z):
    D, S = qt.shape
    bq = STREAM_BLOCK_Q
    c3 = c.reshape(N_PAIRS, 2, S)
    return pl.pallas_call(
        _attn_mxu_kernel,
        grid=(N_PAIRS, S // bq),
        in_specs=[
            pl.BlockSpec((LANES, bq), lambda p, i: (p, i)),
            pl.BlockSpec((S, LANES), lambda p, i: (0, p)),
            pl.BlockSpec((S, LANES), lambda p, i: (0, 0)),
            pl.BlockSpec((S // MXU_TILE, 2 * V_ROWS, MXU_TILE), lambda p, i: (0, p, 0)),
            pl.BlockSpec((1, 2, bq), lambda p, i: (p, 0, i)),
            pl.BlockSpec((bq, LANES), lambda p, i: (i, p)),
        ],
        out_specs=pl.BlockSpec((bq, LANES), lambda p, i: (i, p)),
        out_shape=jax.ShapeDtypeStruct((S, D), BF16),
        scratch_shapes=[pltpu.VMEM((2 * LANES, 2 * bq), BF16), pltpu.VMEM((LANES, bq), F32)],
        compiler_params=pltpu.CompilerParams(
            dimension_semantics=("arbitrary", "arbitrary"), vmem_limit_bytes=VMEM_LIMIT_BYTES),
        name="fox_attention_mxu",
    )(qt, k, caug, vt, c3, z)


def _out_proj_kernel(x_ref, a_ref, w_ref, o_ref):
    o_ref[...] = x_ref[...] + jnp.dot(a_ref[...], w_ref[...].astype(BF16),
                                      preferred_element_type=F32)


def _out_proj(x, a, w_out):
    S, D = x.shape
    T = OUT_ROW_TILE
    return pl.pallas_call(
        _out_proj_kernel,
        grid=(S // T,),
        in_specs=[
            pl.BlockSpec((T, D), lambda i: (i, 0)),
            pl.BlockSpec((T, D), lambda i: (i, 0)),
            pl.BlockSpec((D, D), lambda i: (0, 0), pipeline_mode=pl.Buffered(1)),
        ],
        out_specs=pl.BlockSpec((T, D), lambda i: (i, 0)),
        out_shape=jax.ShapeDtypeStruct((S, D), F32),
        compiler_params=pltpu.CompilerParams(
            dimension_semantics=("arbitrary",), vmem_limit_bytes=VMEM_LIMIT_BYTES),
        name="attn_out_proj",
    )(x, a, w_out)


def _values_row_major(vt):
    n_chunks = vt.shape[0]
    v = vt.reshape(n_chunks, N_HEADS, V_ROWS, MXU_TILE)[:, :, :HEAD_DIM]
    return v.transpose(0, 3, 1, 2).reshape(n_chunks * MXU_TILE, D_MODEL)


def kernel(x, conv_norm_g, conv_w_in, conv_w, conv_w_out, attn_norm_g, attn_w_in,
           attn_b_f, attn_q_norm_g, attn_k_norm_g, attn_w_out):
    B, S, D = x.shape
    assert (B, D) == (1, D_MODEL)
    assert all(S % t == 0 for t in (BLOCK_Q, STREAM_BLOCK_Q, ROW_TILE, OUT_ROW_TILE))
    assert conv_w_in.shape[0] == 1 and attn_w_in.shape[0] == 1
    x2 = x.reshape(S, D)

    x2 = _conv_layer(x2, conv_norm_g[0].reshape(1, D), conv_w_in[0], conv_w[0], conv_w_out[0])

    qt, k, vt, z, c, caug = _attn_proj(
        x2, attn_norm_g[0].reshape(1, D), jnp.swapaxes(attn_w_in, 1, 2),
        attn_b_f[0].reshape(N_HEADS, 1),
        jnp.tile(attn_q_norm_g[0], N_HEADS).reshape(D, 1),
        jnp.tile(attn_k_norm_g[0], N_HEADS).reshape(D, 1))
    logit_bound = (math.sqrt(HEAD_DIM) * jnp.max(jnp.abs(attn_q_norm_g[0]))
                   * jnp.max(jnp.abs(attn_k_norm_g[0])))
    a = lax.cond(logit_bound <= LOGIT_BOUND_MAX,
                 lambda: _attention_mxu(qt, k, caug, vt, c, z),
                 lambda: _attention(qt.T, k, _values_row_major(vt), c, z))
    out = _out_proj(x2, a, attn_w_out[0])
    return out.reshape(B, S, D)
```

```python
import math

import jax
import jax.numpy as jnp
from jax import lax
from jax.experimental import pallas as pl
from jax.experimental.pallas import tpu as pltpu

D_MODEL = 1024
HEAD_DIM = 64
N_HEADS = D_MODEL // HEAD_DIM
N_PAIRS = N_HEADS // 2
CONV_WIDTH = 3
RMS_EPS = 1e-6
LOG2E = math.log2(math.e)

LANES = 128
SUBLANES = 8
VMEM_LIMIT_BYTES = 56 * 1024 * 1024

ROW_TILE = 512
OUT_ROW_TILE = 1024
BLOCK_Q = 512
BLOCK_K = 512
NEG_BIG = -1e30
STREAM_BLOCK_Q = 2048
AUG_MID = N_HEADS
AUG_LO = 2 * N_HEADS
AUG_ONE = 3 * N_HEADS
LOGIT_BOUND_MAX = 60.0

BF16 = jnp.bfloat16
F32 = jnp.float32


def _rmsnorm_rows(x, g):
    inv = lax.rsqrt(jnp.mean(x * x, axis=-1, keepdims=True) + RMS_EPS)
    return x * inv * g


def _silu(z):
    return z * jax.nn.sigmoid(z)


def _conv_layer_kernel(x_ref, g_ref, w_in_ref, cw_ref, w_out_ref, o_ref, tail_ref):
    i = pl.program_id(0)
    D = D_MODEL
    T = x_ref.shape[0]

    @pl.when(i == 0)
    def _():
        tail_ref[...] = jnp.zeros_like(tail_ref)

    x = x_ref[...]
    h = _rmsnorm_rows(x, g_ref[...])
    proj = jnp.dot(h, w_in_ref[...], preferred_element_type=F32)
    b_g = proj[:, 0 * D:1 * D]
    c_g = proj[:, 1 * D:2 * D]
    xin = proj[:, 2 * D:3 * D]
    z = proj[:, 3 * D:4 * D]
    u = c_g * xin

    tail = tail_ref[...]
    prev1 = tail[SUBLANES - 1:SUBLANES, :]
    prev2 = tail[SUBLANES - 2:SUBLANES - 1, :]
    row = lax.broadcasted_iota(jnp.int32, (T, 1), 0)
    u1 = jnp.where(row == 0, prev1, pltpu.roll(u, 1, axis=0))
    u2 = jnp.where(row == 0, prev2, jnp.where(row == 1, prev1, pltpu.roll(u, 2, axis=0)))
    tail_ref[...] = u[T - SUBLANES:, :]

    cw = cw_ref[...]
    y = cw[0:1, :] * u2 + cw[1:2, :] * u1 + cw[2:3, :] * u
    y = b_g * y * _silu(z)
    o_ref[...] = x + jnp.dot(y, w_out_ref[...], preferred_element_type=F32)


def _conv_layer(x, g, w_in, cw, w_out):
    S, D = x.shape
    const = lambda i: (0, 0)
    return pl.pallas_call(
        _conv_layer_kernel,
        grid=(S // ROW_TILE,),
        in_specs=[
            pl.BlockSpec((ROW_TILE, D), lambda i: (i, 0)),
            pl.BlockSpec((1, D), const),
            pl.BlockSpec((D, 4 * D), const, pipeline_mode=pl.Buffered(1)),
            pl.BlockSpec((CONV_WIDTH, D), const),
            pl.BlockSpec((D, D), const, pipeline_mode=pl.Buffered(1)),
        ],
        out_specs=pl.BlockSpec((ROW_TILE, D), lambda i: (i, 0)),
        out_shape=jax.ShapeDtypeStruct((S, D), F32),
        scratch_shapes=[pltpu.VMEM((SUBLANES, D), F32)],
        compiler_params=pltpu.CompilerParams(
            dimension_semantics=("arbitrary",), vmem_limit_bytes=VMEM_LIMIT_BYTES),
        name="conv_layer",
    )(x, g, w_in, cw, w_out)


def _split3(x):
    hi = x.astype(BF16).astype(F32)
    r = x - hi
    mid = r.astype(BF16).astype(F32)
    lo = (r - mid).astype(BF16).astype(F32)
    return hi, mid, lo


def _head_rmsnorm_t(xt, g_col):
    t = xt.shape[1]
    x3 = xt.reshape(N_HEADS, HEAD_DIM, t)
    inv = lax.rsqrt(jnp.mean(x3 * x3, axis=1, keepdims=True) + RMS_EPS)
    return (x3 * inv * g_col.reshape(N_HEADS, HEAD_DIM, 1)).reshape(N_HEADS * HEAD_DIM, t)


def _attn_proj_kernel(x_ref, g_ref, wq_ref, wk_ref, wv_ref, wz_ref, wf_ref, bf_ref, qg_ref, kg_ref,
                      tri_ref,
                      qt_ref, k_ref, vt_ref, z_ref, c_ref, caug_ref, run_ref):
    i = pl.program_id(0)
    D = D_MODEL
    T = x_ref.shape[0]

    @pl.when(i == 0)
    def _():
        run_ref[...] = jnp.zeros_like(run_ref)

    trans_b = (((1,), (1,)), ((), ()))
    h = _rmsnorm_rows(x_ref[...], g_ref[...])
    z_ref[...] = lax.dot_general(h, wz_ref[...], trans_b, preferred_element_type=F32)

    qt = lax.dot_general(wq_ref[...], h, trans_b, preferred_element_type=F32)
    scale = LOG2E / math.sqrt(HEAD_DIM)
    qt_ref[...] = (_head_rmsnorm_t(qt, qg_ref[...]) * scale).astype(BF16)
    kt = lax.dot_general(wk_ref[...], h, trans_b, preferred_element_type=F32)
    k_ref[...] = _head_rmsnorm_t(kt, kg_ref[...]).T.astype(BF16)
    vt_ref[...] = lax.dot_general(wv_ref[...], h, trans_b,
                                  preferred_element_type=F32).astype(BF16)

    f = lax.dot_general(wf_ref[...], h, trans_b, preferred_element_type=F32) + bf_ref[...]
    logf = -(jnp.maximum(-f, 0.0) + jnp.log1p(jnp.exp(-jnp.abs(f))))
    pieces = jnp.concatenate(_split3(logf), axis=0).astype(BF16)
    cs = jnp.dot(pieces, tri_ref[...], preferred_element_type=F32)
    cs = cs[:N_HEADS] + cs[N_HEADS:2 * N_HEADS] + cs[2 * N_HEADS:]
    c = cs + run_ref[...][:, 0:1]
    c_ref[...] = c
    run_ref[...] = jnp.broadcast_to(c[:, T - 1:T], run_ref.shape)

    crow = jnp.concatenate([c, jnp.zeros((LANES - N_HEADS, T), F32)], axis=0).T
    hi, mid, lo = _split3(crow * LOG2E)
    lane = lax.broadcasted_iota(jnp.int32, (1, LANES), 1)
    ones = jnp.where((lane >= AUG_ONE) & (lane < AUG_ONE + 3), 1.0, 0.0)
    caug = hi + pltpu.roll(mid, AUG_MID, axis=1) + pltpu.roll(lo, AUG_LO, axis=1) + ones
    caug_ref[...] = caug.astype(BF16)


def _attn_proj(x, g, w_in_t, b_f, q_g_col, k_g_col):
    S, D = x.shape
    T = ROW_TILE
    tri = (jnp.arange(T)[:, None] <= jnp.arange(T)[None, :]).astype(BF16)
    const = lambda i: (0, 0)
    rows = lambda i: (i, 0)
    cols = lambda i: (0, i)
    return pl.pallas_call(
        _attn_proj_kernel,
        grid=(S // T,),
        in_specs=[
            pl.BlockSpec((T, D), rows),
            pl.BlockSpec((1, D), const),
            pl.BlockSpec((None, D, D), lambda i: (0, 0, 0), pipeline_mode=pl.Buffered(1)),
            pl.BlockSpec((None, D, D), lambda i: (0, 1, 0), pipeline_mode=pl.Buffered(1)),
            pl.BlockSpec((None, D, D), lambda i: (0, 2, 0), pipeline_mode=pl.Buffered(1)),
            pl.BlockSpec((None, D, D), lambda i: (0, 3, 0), pipeline_mode=pl.Buffered(1)),
            pl.BlockSpec((None, N_HEADS, D), lambda i: (0, 4 * D // N_HEADS, 0)),
            pl.BlockSpec((N_HEADS, 1), const),
            pl.BlockSpec((D, 1), const),
            pl.BlockSpec((D, 1), const),
            pl.BlockSpec((T, T), const),
        ],
        out_specs=[
            pl.BlockSpec((D, T), cols),
            pl.BlockSpec((T, D), rows),
            pl.BlockSpec((D, T), cols),
            pl.BlockSpec((T, D), rows),
            pl.BlockSpec((N_HEADS, T), cols),
            pl.BlockSpec((T, LANES), rows),
        ],
        out_shape=[
            jax.ShapeDtypeStruct((D, S), BF16),
            jax.ShapeDtypeStruct((S, D), BF16),
            jax.ShapeDtypeStruct((D, S), BF16),
            jax.ShapeDtypeStruct((S, D), F32),
            jax.ShapeDtypeStruct((N_HEADS, S), F32),
            jax.ShapeDtypeStruct((S, LANES), BF16),
        ],
        scratch_shapes=[pltpu.VMEM((N_HEADS, LANES), F32)],
        compiler_params=pltpu.CompilerParams(
            dimension_semantics=("arbitrary",), vmem_limit_bytes=VMEM_LIMIT_BYTES),
        name="attn_proj",
    )(x, g, w_in_t, w_in_t, w_in_t, w_in_t, w_in_t, b_f, q_g_col, k_g_col, tri)


def _attn_kernel(q_ref, k_ref, v_ref, c_ref, z_ref, o_ref, m_ref, l_ref, acc_ref):
    i = pl.program_id(1)
    bq, bk = BLOCK_Q, BLOCK_K
    q2 = q_ref[...]
    lane = lax.broadcasted_iota(jnp.int32, (1, LANES), 1)
    zero = jnp.zeros_like(q2)
    q_heads = (jnp.where(lane < HEAD_DIM, q2, zero), jnp.where(lane >= HEAD_DIM, q2, zero))

    m_ref[...] = jnp.full_like(m_ref, NEG_BIG)
    l_ref[...] = jnp.zeros_like(l_ref)
    acc_ref[...] = jnp.zeros_like(acc_ref)

    def step(j, masked):
        start = pl.multiple_of(j * bk, bk)
        kb = k_ref[pl.ds(start, bk), :]
        vb = v_ref[pl.ds(start, bk), :]
        cb = c_ref[0, :, pl.ds(start, bk)] * LOG2E
        for hh in range(2):
            s = lax.dot_general(q_heads[hh], kb, (((1,), (1,)), ((), ())),
                                preferred_element_type=F32)
            s = s - cb[hh:hh + 1, :]
            if masked:
                row = lax.broadcasted_iota(jnp.int32, (bq, bk), 0)
                col = lax.broadcasted_iota(jnp.int32, (bq, bk), 1)
                s = jnp.where(col <= row, s, NEG_BIG)
            m_old = m_ref[hh]
            m_new = jnp.maximum(m_old, jnp.max(s, axis=-1, keepdims=True))
            alpha = jnp.exp2(m_old - m_new)
            p = jnp.exp2(s - m_new)
            l_ref[hh] = alpha * l_ref[hh] + jnp.sum(p, axis=-1, keepdims=True)
            acc_ref[hh] = alpha * acc_ref[hh] + jnp.dot(
                p.astype(BF16), vb, preferred_element_type=F32)
            m_ref[hh] = m_new

    def body(j, carry):
        step(j, masked=False)
        return carry

    lax.fori_loop(0, i, body, 0)
    step(i, masked=True)

    o0 = acc_ref[0] / l_ref[0]
    o1 = acc_ref[1] / l_ref[1]
    o = jnp.where(lane < HEAD_DIM, o0, o1)
    o_ref[...] = (o * _silu(z_ref[...])).astype(BF16)


def _attention(q, k, v, c, z):
    S, D = q.shape
    bq = BLOCK_Q
    c3 = c.reshape(N_PAIRS, 2, S)
    return pl.pallas_call(
        _attn_kernel,
        grid=(N_PAIRS, S // bq),
        in_specs=[
            pl.BlockSpec((bq, LANES), lambda p, i: (i, p)),
            pl.BlockSpec((S, LANES), lambda p, i: (0, p)),
            pl.BlockSpec((S, LANES), lambda p, i: (0, p)),
            pl.BlockSpec((1, 2, S), lambda p, i: (p, 0, 0)),
            pl.BlockSpec((bq, LANES), lambda p, i: (i, p)),
        ],
        out_specs=pl.BlockSpec((bq, LANES), lambda p, i: (i, p)),
        out_shape=jax.ShapeDtypeStruct((S, D), BF16),
        scratch_shapes=[
            pltpu.VMEM((2, bq, 1), F32),
            pltpu.VMEM((2, bq, 1), F32),
            pltpu.VMEM((2, bq, LANES), F32),
        ],
        compiler_params=pltpu.CompilerParams(
            dimension_semantics=("arbitrary", "arbitrary"), vmem_limit_bytes=VMEM_LIMIT_BYTES),
        name="fox_attention",
    )(q, k, v, c3, z)


MXU_TILE = 256
S_SLOTS = 3
S_TILE_ENTRIES = MXU_TILE * MXU_TILE // (SUBLANES * LANES)
S_ADDR = tuple(s * S_TILE_ENTRIES for s in range(S_SLOTS))
O_ADDR = S_SLOTS * S_TILE_ENTRIES
ONES_ROWS = 16


def _attn_mxu_kernel(qt_ref, k_ref, caug_ref, vt_ref, c_ref, z_ref, o_ref, qa_ref, ot_ref, pt_ref):
    pair = pl.program_id(0)
    i = pl.program_id(1)
    bq, tq, kc = STREAM_BLOCK_Q, MXU_TILE, MXU_TILE

    qt = qt_ref[...]
    r = lax.broadcasted_iota(jnp.int32, (LANES, 1), 0)
    zero = jnp.zeros_like(qt)
    top = jnp.concatenate([jnp.where(r < HEAD_DIM, qt, zero),
                           jnp.where(r >= HEAD_DIM, qt, zero)], axis=1)
    ci = c_ref[0] * LOG2E
    halves = []
    for hh in range(2):
        head = 2 * pair + hh
        c_hi, c_mid, c_lo = _split3(ci[hh:hh + 1, :])
        minus_cj = jnp.where((r == head) | (r == AUG_MID + head) | (r == AUG_LO + head), -1.0, 0.0)
        halves.append(minus_cj + jnp.where(r == AUG_ONE, c_hi, 0.0)
                      + jnp.where(r == AUG_ONE + 1, c_mid, 0.0)
                      + jnp.where(r == AUG_ONE + 2, c_lo, 0.0))
    qa_ref[...] = jnp.concatenate([top, jnp.concatenate(halves, axis=1).astype(BF16)], axis=0)
    ones_rows = jnp.ones((ONES_ROWS, kc), BF16)

    def q_tile(t, carry):
        q0 = i * bq + t * tq
        n_real = q0 // kc + 1
        n_iter = (n_real + S_SLOTS - 1) // S_SLOTS

        def chunk_start(c):
            return pl.multiple_of(jnp.minimum(c, n_real - 1) * kc, kc)

        def stage_q():
            for hh in range(2):
                pltpu.matmul_push_rhs(qa_ref[:, pl.ds(hh * bq + t * tq, tq)],
                                      staging_register=0, mxu_index=hh)

        def stage_a(c, slot, restage=True):
            start = chunk_start(c)
            ka = jnp.concatenate([k_ref[pl.ds(start, kc), :], caug_ref[pl.ds(start, kc), :]],
                                 axis=1)
            for hh in range(2):
                pltpu.matmul_acc_lhs(S_ADDR[slot], ka, mxu_index=hh, load_staged_rhs=0)
            if restage:
                stage_q()

        def stage_b(c, slot, masked):
            start = chunk_start(c)
            for hh in range(2):
                s = pltpu.matmul_pop(S_ADDR[slot], (kc, tq), F32, mxu_index=hh)
                if masked:
                    key = c * kc + lax.broadcasted_iota(jnp.int32, (kc, tq), 0)
                    qry = q0 + lax.broadcasted_iota(jnp.int32, (kc, tq), 1)
                    s = jnp.where(key <= qry, s, NEG_BIG)
                pt_ref[hh, slot] = jnp.exp2(s).astype(BF16)
                pltpu.matmul_push_rhs(pt_ref[hh, slot], staging_register=1, mxu_index=hh)
                vta = jnp.concatenate(
                    [vt_ref[hh * HEAD_DIM:(hh + 1) * HEAD_DIM, pl.ds(start, kc)], ones_rows],
                    axis=0)
                pltpu.matmul_acc_lhs(O_ADDR, vta, mxu_index=hh, load_staged_rhs=1)

        stage_q()
        stage_a(0, 0)
        stage_a(1, 1)

        def body(g, carry):
            for s in range(S_SLOTS):
                c = g * S_SLOTS + s
                stage_a(c + 2, (s + 2) % S_SLOTS)
                stage_b(c, s, False)
            return carry

        lax.fori_loop(0, n_iter - 1, body, 0)
        c_last = (n_iter - 1) * S_SLOTS
        stage_a(c_last + 2, 2, restage=False)
        for s in range(S_SLOTS):
            stage_b(c_last + s, s, True)

        outs = []
        for hh in range(2):
            o = pltpu.matmul_pop(O_ADDR, (HEAD_DIM + ONES_ROWS, tq), F32, mxu_index=hh)
            outs.append(o[:HEAD_DIM, :] / o[HEAD_DIM:HEAD_DIM + 1, :])
        ot_ref[:, t * tq:(t + 1) * tq] = jnp.concatenate(outs, axis=0)
        return carry

    for t in range(bq // tq):
        q_tile(t, 0)
    o_ref[...] = (ot_ref[...].T * _silu(z_ref[...])).astype(BF16)


def _attention_mxu(qt, k, caug, vt, c, z):
    D, S = qt.shape
    bq = STREAM_BLOCK_Q
    c3 = c.reshape(N_PAIRS, 2, S)
    return pl.pallas_call(
        _attn_mxu_kernel,
        grid=(N_PAIRS, S // bq),
        in_specs=[
            pl.BlockSpec((LANES, bq), lambda p, i: (p, i)),
            pl.BlockSpec((S, LANES), lambda p, i: (0, p)),
            pl.BlockSpec((S, LANES), lambda p, i: (0, 0)),
            pl.BlockSpec((LANES, S), lambda p, i: (p, 0)),
            pl.BlockSpec((1, 2, bq), lambda p, i: (p, 0, i)),
            pl.BlockSpec((bq, LANES), lambda p, i: (i, p)),
        ],
        out_specs=pl.BlockSpec((bq, LANES), lambda p, i: (i, p)),
        out_shape=jax.ShapeDtypeStruct((S, D), BF16),
        scratch_shapes=[pltpu.VMEM((2 * LANES, 2 * bq), BF16), pltpu.VMEM((LANES, bq), F32),
                        pltpu.VMEM((2, S_SLOTS, MXU_TILE, MXU_TILE), BF16)],
        compiler_params=pltpu.CompilerParams(
            dimension_semantics=("arbitrary", "arbitrary"), vmem_limit_bytes=VMEM_LIMIT_BYTES),
        name="fox_attention_mxu",
    )(qt, k, caug, vt, c3, z)


def _out_proj_kernel(x_ref, a_ref, w_ref, o_ref):
    o_ref[...] = x_ref[...] + jnp.dot(a_ref[...], w_ref[...].astype(BF16),
                                      preferred_element_type=F32)


def _out_proj(x, a, w_out):
    S, D = x.shape
    T = OUT_ROW_TILE
    return pl.pallas_call(
        _out_proj_kernel,
        grid=(S // T,),
        in_specs=[
            pl.BlockSpec((T, D), lambda i: (i, 0)),
            pl.BlockSpec((T, D), lambda i: (i, 0)),
            pl.BlockSpec((D, D), lambda i: (0, 0), pipeline_mode=pl.Buffered(1)),
        ],
        out_specs=pl.BlockSpec((T, D), lambda i: (i, 0)),
        out_shape=jax.ShapeDtypeStruct((S, D), F32),
        compiler_params=pltpu.CompilerParams(
            dimension_semantics=("arbitrary",), vmem_limit_bytes=VMEM_LIMIT_BYTES),
        name="attn_out_proj",
    )(x, a, w_out)


def kernel(x, conv_norm_g, conv_w_in, conv_w, conv_w_out, attn_norm_g, attn_w_in,
           attn_b_f, attn_q_norm_g, attn_k_norm_g, attn_w_out):
    B, S, D = x.shape
    assert (B, D) == (1, D_MODEL)
    assert all(S % t == 0 for t in (BLOCK_Q, STREAM_BLOCK_Q, ROW_TILE, OUT_ROW_TILE))
    assert conv_w_in.shape[0] == 1 and attn_w_in.shape[0] == 1
    x2 = x.reshape(S, D)

    x2 = _conv_layer(x2, conv_norm_g[0].reshape(1, D), conv_w_in[0], conv_w[0], conv_w_out[0])

    qt, k, vt, z, c, caug = _attn_proj(
        x2, attn_norm_g[0].reshape(1, D), jnp.swapaxes(attn_w_in, 1, 2),
        attn_b_f[0].reshape(N_HEADS, 1),
        jnp.tile(attn_q_norm_g[0], N_HEADS).reshape(D, 1),
        jnp.tile(attn_k_norm_g[0], N_HEADS).reshape(D, 1))
    logit_bound = (math.sqrt(HEAD_DIM) * jnp.max(jnp.abs(attn_q_norm_g[0]))
                   * jnp.max(jnp.abs(attn_k_norm_g[0])))
    a = lax.cond(logit_bound <= LOGIT_BOUND_MAX,
                 lambda: _attention_mxu(qt, k, caug, vt, c, z),
                 lambda: _attention(qt.T, k, vt.T, c, z))
    out = _out_proj(x2, a, attn_w_out[0])
    return out.reshape(B, S, D)
```

```python
import math

import jax
import jax.numpy as jnp
from jax import lax
from jax.experimental import pallas as pl
from jax.experimental.pallas import tpu as pltpu

D_MODEL = 1024
HEAD_DIM = 64
N_HEADS = D_MODEL // HEAD_DIM
N_PAIRS = N_HEADS // 2
CONV_WIDTH = 3
RMS_EPS = 1e-6
LOG2E = math.log2(math.e)

LANES = 128
SUBLANES = 8
VMEM_LIMIT_BYTES = 56 * 1024 * 1024

ROW_TILE = 512
OUT_ROW_TILE = 1024
BLOCK_Q = 512
BLOCK_K = 512
NEG_BIG = -1e30
STREAM_BLOCK_Q = 2048
AUG_MID = N_HEADS
AUG_LO = 2 * N_HEADS
AUG_ONE = 3 * N_HEADS
LOGIT_BOUND_MAX = 60.0

BF16 = jnp.bfloat16
F32 = jnp.float32


def _rmsnorm_rows(x, g):
    inv = lax.rsqrt(jnp.mean(x * x, axis=-1, keepdims=True) + RMS_EPS)
    return x * inv * g


def _silu(z):
    return z * jax.nn.sigmoid(z)


def _conv_layer_kernel(x_ref, g_ref, w_in_ref, cw_ref, w_out_ref, o_ref, tail_ref):
    i = pl.program_id(0)
    D = D_MODEL
    T = x_ref.shape[0]

    @pl.when(i == 0)
    def _():
        tail_ref[...] = jnp.zeros_like(tail_ref)

    x = x_ref[...]
    h = _rmsnorm_rows(x, g_ref[...])
    proj = jnp.dot(h, w_in_ref[...], preferred_element_type=F32)
    b_g = proj[:, 0 * D:1 * D]
    c_g = proj[:, 1 * D:2 * D]
    xin = proj[:, 2 * D:3 * D]
    z = proj[:, 3 * D:4 * D]
    u = c_g * xin

    tail = tail_ref[...]
    prev1 = tail[SUBLANES - 1:SUBLANES, :]
    prev2 = tail[SUBLANES - 2:SUBLANES - 1, :]
    row = lax.broadcasted_iota(jnp.int32, (T, 1), 0)
    u1 = jnp.where(row == 0, prev1, pltpu.roll(u, 1, axis=0))
    u2 = jnp.where(row == 0, prev2, jnp.where(row == 1, prev1, pltpu.roll(u, 2, axis=0)))
    tail_ref[...] = u[T - SUBLANES:, :]

    cw = cw_ref[...]
    y = cw[0:1, :] * u2 + cw[1:2, :] * u1 + cw[2:3, :] * u
    y = b_g * y * _silu(z)
    o_ref[...] = x + jnp.dot(y, w_out_ref[...], preferred_element_type=F32)


def _conv_layer(x, g, w_in, cw, w_out):
    S, D = x.shape
    const = lambda i: (0, 0)
    return pl.pallas_call(
        _conv_layer_kernel,
        grid=(S // ROW_TILE,),
        in_specs=[
            pl.BlockSpec((ROW_TILE, D), lambda i: (i, 0)),
            pl.BlockSpec((1, D), const),
            pl.BlockSpec((D, 4 * D), const, pipeline_mode=pl.Buffered(1)),
            pl.BlockSpec((CONV_WIDTH, D), const),
            pl.BlockSpec((D, D), const, pipeline_mode=pl.Buffered(1)),
        ],
        out_specs=pl.BlockSpec((ROW_TILE, D), lambda i: (i, 0)),
        out_shape=jax.ShapeDtypeStruct((S, D), F32),
        scratch_shapes=[pltpu.VMEM((SUBLANES, D), F32)],
        compiler_params=pltpu.CompilerParams(
            dimension_semantics=("arbitrary",), vmem_limit_bytes=VMEM_LIMIT_BYTES),
        name="conv_layer",
    )(x, g, w_in, cw, w_out)


def _split3(x):
    hi = x.astype(BF16).astype(F32)
    r = x - hi
    mid = r.astype(BF16).astype(F32)
    lo = (r - mid).astype(BF16).astype(F32)
    return hi, mid, lo


def _head_rmsnorm_t(xt, g_col):
    t = xt.shape[1]
    x3 = xt.reshape(N_HEADS, HEAD_DIM, t)
    inv = lax.rsqrt(jnp.mean(x3 * x3, axis=1, keepdims=True) + RMS_EPS)
    return (x3 * inv * g_col.reshape(N_HEADS, HEAD_DIM, 1)).reshape(N_HEADS * HEAD_DIM, t)


def _attn_proj_kernel(x_ref, g_ref, wq_ref, wk_ref, wv_ref, wz_ref, wf_ref, bf_ref, qg_ref, kg_ref,
                      tri_ref,
                      qt_ref, k_ref, vt_ref, z_ref, c_ref, caug_ref, run_ref):
    i = pl.program_id(0)
    D = D_MODEL
    T = x_ref.shape[0]

    @pl.when(i == 0)
    def _():
        run_ref[...] = jnp.zeros_like(run_ref)

    trans_b = (((1,), (1,)), ((), ()))
    h = _rmsnorm_rows(x_ref[...], g_ref[...])
    z_ref[...] = lax.dot_general(h, wz_ref[...], trans_b, preferred_element_type=F32)

    qt = lax.dot_general(wq_ref[...], h, trans_b, preferred_element_type=F32)
    scale = LOG2E / math.sqrt(HEAD_DIM)
    qt_ref[...] = (_head_rmsnorm_t(qt, qg_ref[...]) * scale).astype(BF16)
    kt = lax.dot_general(wk_ref[...], h, trans_b, preferred_element_type=F32)
    k_ref[...] = _head_rmsnorm_t(kt, kg_ref[...]).T.astype(BF16)
    vt_ref[...] = lax.dot_general(wv_ref[...], h, trans_b,
                                  preferred_element_type=F32).astype(BF16)

    f = lax.dot_general(wf_ref[...], h, trans_b, preferred_element_type=F32) + bf_ref[...]
    logf = -(jnp.maximum(-f, 0.0) + jnp.log1p(jnp.exp(-jnp.abs(f))))
    pieces = jnp.concatenate(_split3(logf), axis=0).astype(BF16)
    cs = jnp.dot(pieces, tri_ref[...], preferred_element_type=F32)
    cs = cs[:N_HEADS] + cs[N_HEADS:2 * N_HEADS] + cs[2 * N_HEADS:]
    c = cs + run_ref[...][:, 0:1]
    c_ref[...] = c
    run_ref[...] = jnp.broadcast_to(c[:, T - 1:T], run_ref.shape)

    crow = jnp.concatenate([c, jnp.zeros((LANES - N_HEADS, T), F32)], axis=0).T
    hi, mid, lo = _split3(crow * LOG2E)
    lane = lax.broadcasted_iota(jnp.int32, (1, LANES), 1)
    ones = jnp.where((lane >= AUG_ONE) & (lane < AUG_ONE + 3), 1.0, 0.0)
    caug = hi + pltpu.roll(mid, AUG_MID, axis=1) + pltpu.roll(lo, AUG_LO, axis=1) + ones
    caug_ref[...] = caug.astype(BF16)


def _attn_proj(x, g, w_in_t, b_f, q_g_col, k_g_col):
    S, D = x.shape
    T = ROW_TILE
    tri = (jnp.arange(T)[:, None] <= jnp.arange(T)[None, :]).astype(BF16)
    const = lambda i: (0, 0)
    rows = lambda i: (i, 0)
    cols = lambda i: (0, i)
    return pl.pallas_call(
        _attn_proj_kernel,
        grid=(S // T,),
        in_specs=[
            pl.BlockSpec((T, D), rows),
            pl.BlockSpec((1, D), const),
            pl.BlockSpec((None, D, D), lambda i: (0, 0, 0), pipeline_mode=pl.Buffered(1)),
            pl.BlockSpec((None, D, D), lambda i: (0, 1, 0), pipeline_mode=pl.Buffered(1)),
            pl.BlockSpec((None, D, D), lambda i: (0, 2, 0), pipeline_mode=pl.Buffered(1)),
            pl.BlockSpec((None, D, D), lambda i: (0, 3, 0), pipeline_mode=pl.Buffered(1)),
            pl.BlockSpec((None, N_HEADS, D), lambda i: (0, 4 * D // N_HEADS, 0)),
            pl.BlockSpec((N_HEADS, 1), const),
            pl.BlockSpec((D, 1), const),
            pl.BlockSpec((D, 1), const),
            pl.BlockSpec((T, T), const),
        ],
        out_specs=[
            pl.BlockSpec((D, T), cols),
            pl.BlockSpec((T, D), rows),
            pl.BlockSpec((D, T), cols),
            pl.BlockSpec((T, D), rows),
            pl.BlockSpec((N_HEADS, T), cols),
            pl.BlockSpec((T, LANES), rows),
        ],
        out_shape=[
            jax.ShapeDtypeStruct((D, S), BF16),
            jax.ShapeDtypeStruct((S, D), BF16),
            jax.ShapeDtypeStruct((D, S), BF16),
            jax.ShapeDtypeStruct((S, D), F32),
            jax.ShapeDtypeStruct((N_HEADS, S), F32),
            jax.ShapeDtypeStruct((S, LANES), BF16),
        ],
        scratch_shapes=[pltpu.VMEM((N_HEADS, LANES), F32)],
        compiler_params=pltpu.CompilerParams(
            dimension_semantics=("arbitrary",), vmem_limit_bytes=VMEM_LIMIT_BYTES),
        name="attn_proj",
    )(x, g, w_in_t, w_in_t, w_in_t, w_in_t, w_in_t, b_f, q_g_col, k_g_col, tri)


def _attn_kernel(q_ref, k_ref, v_ref, c_ref, z_ref, o_ref, m_ref, l_ref, acc_ref):
    i = pl.program_id(1)
    bq, bk = BLOCK_Q, BLOCK_K
    q2 = q_ref[...]
    lane = lax.broadcasted_iota(jnp.int32, (1, LANES), 1)
    zero = jnp.zeros_like(q2)
    q_heads = (jnp.where(lane < HEAD_DIM, q2, zero), jnp.where(lane >= HEAD_DIM, q2, zero))

    m_ref[...] = jnp.full_like(m_ref, NEG_BIG)
    l_ref[...] = jnp.zeros_like(l_ref)
    acc_ref[...] = jnp.zeros_like(acc_ref)

    def step(j, masked):
        start = pl.multiple_of(j * bk, bk)
        kb = k_ref[pl.ds(start, bk), :]
        vb = v_ref[pl.ds(start, bk), :]
        cb = c_ref[0, :, pl.ds(start, bk)] * LOG2E
        for hh in range(2):
            s = lax.dot_general(q_heads[hh], kb, (((1,), (1,)), ((), ())),
                                preferred_element_type=F32)
            s = s - cb[hh:hh + 1, :]
            if masked:
                row = lax.broadcasted_iota(jnp.int32, (bq, bk), 0)
                col = lax.broadcasted_iota(jnp.int32, (bq, bk), 1)
                s = jnp.where(col <= row, s, NEG_BIG)
            m_old = m_ref[hh]
            m_new = jnp.maximum(m_old, jnp.max(s, axis=-1, keepdims=True))
            alpha = jnp.exp2(m_old - m_new)
            p = jnp.exp2(s - m_new)
            l_ref[hh] = alpha * l_ref[hh] + jnp.sum(p, axis=-1, keepdims=True)
            acc_ref[hh] = alpha * acc_ref[hh] + jnp.dot(
                p.astype(BF16), vb, preferred_element_type=F32)
            m_ref[hh] = m_new

    def body(j, carry):
        step(j, masked=False)
        return carry

    lax.fori_loop(0, i, body, 0)
    step(i, masked=True)

    o0 = acc_ref[0] / l_ref[0]
    o1 = acc_ref[1] / l_ref[1]
    o = jnp.where(lane < HEAD_DIM, o0, o1)
    o_ref[...] = (o * _silu(z_ref[...])).astype(BF16)


def _attention(q, k, v, c, z):
    S, D = q.shape
    bq = BLOCK_Q
    c3 = c.reshape(N_PAIRS, 2, S)
    return pl.pallas_call(
        _attn_kernel,
        grid=(N_PAIRS, S // bq),
        in_specs=[
            pl.BlockSpec((bq, LANES), lambda p, i: (i, p)),
            pl.BlockSpec((S, LANES), lambda p, i: (0, p)),
            pl.BlockSpec((S, LANES), lambda p, i: (0, p)),
            pl.BlockSpec((1, 2, S), lambda p, i: (p, 0, 0)),
            pl.BlockSpec((bq, LANES), lambda p, i: (i, p)),
        ],
        out_specs=pl.BlockSpec((bq, LANES), lambda p, i: (i, p)),
        out_shape=jax.ShapeDtypeStruct((S, D), BF16),
        scratch_shapes=[
            pltpu.VMEM((2, bq, 1), F32),
            pltpu.VMEM((2, bq, 1), F32),
            pltpu.VMEM((2, bq, LANES), F32),
        ],
        compiler_params=pltpu.CompilerParams(
            dimension_semantics=("arbitrary", "arbitrary"), vmem_limit_bytes=VMEM_LIMIT_BYTES),
        name="fox_attention",
    )(q, k, v, c3, z)


MXU_TILE = 256
S_SLOTS = 3
S_TILE_ENTRIES = MXU_TILE * MXU_TILE // (SUBLANES * LANES)
S_ADDR = tuple(s * S_TILE_ENTRIES for s in range(S_SLOTS))
O_ADDR = S_SLOTS * S_TILE_ENTRIES
ONES_ROWS = 16


def _attn_mxu_kernel(qt_ref, k_ref, caug_ref, vt_ref, c_ref, z_ref, o_ref, qa_ref, ot_ref, pt_ref):
    pair = pl.program_id(0)
    i = pl.program_id(1)
    bq, tq, kc = STREAM_BLOCK_Q, MXU_TILE, MXU_TILE

    qt = qt_ref[...]
    r = lax.broadcasted_iota(jnp.int32, (LANES, 1), 0)
    zero = jnp.zeros_like(qt)
    top = jnp.concatenate([jnp.where(r < HEAD_DIM, qt, zero),
                           jnp.where(r >= HEAD_DIM, qt, zero)], axis=1)
    ci = c_ref[0] * LOG2E
    halves = []
    for hh in range(2):
        head = 2 * pair + hh
        c_hi, c_mid, c_lo = _split3(ci[hh:hh + 1, :])
        minus_cj = jnp.where((r == head) | (r == AUG_MID + head) | (r == AUG_LO + head), -1.0, 0.0)
        halves.append(minus_cj + jnp.where(r == AUG_ONE, c_hi, 0.0)
                      + jnp.where(r == AUG_ONE + 1, c_mid, 0.0)
                      + jnp.where(r == AUG_ONE + 2, c_lo, 0.0))
    qa_ref[...] = jnp.concatenate([top, jnp.concatenate(halves, axis=1).astype(BF16)], axis=0)
    ones_rows = jnp.ones((ONES_ROWS, kc), BF16)

    def q_tile(t, carry):
        q0 = i * bq + t * tq
        n_real = q0 // kc + 1
        n_iter = (n_real + S_SLOTS - 1) // S_SLOTS

        def chunk_start(c):
            return pl.multiple_of(jnp.minimum(c, n_real - 1) * kc, kc)

        def stage_q():
            for hh in range(2):
                pltpu.matmul_push_rhs(qa_ref[:, pl.ds(hh * bq + t * tq, tq)],
                                      staging_register=0, mxu_index=hh)

        def stage_a(c, slot, restage=True):
            start = chunk_start(c)
            ka = jnp.concatenate([k_ref[pl.ds(start, kc), :], caug_ref[pl.ds(start, kc), :]],
                                 axis=1)
            for hh in range(2):
                pltpu.matmul_acc_lhs(S_ADDR[slot], ka, mxu_index=hh, load_staged_rhs=0)
            if restage:
                stage_q()

        def stage_b(c, slot, masked, l):
            start = chunk_start(c)
            l = list(l)
            for hh in range(2):
                s = pltpu.matmul_pop(S_ADDR[slot], (kc, tq), F32, mxu_index=hh)
                if masked:
                    key = c * kc + lax.broadcasted_iota(jnp.int32, (kc, tq), 0)
                    qry = q0 + lax.broadcasted_iota(jnp.int32, (kc, tq), 1)
                    s = jnp.where(key <= qry, s, NEG_BIG)
                p = jnp.exp2(s)
                l[hh] = l[hh] + jnp.sum(p.reshape(kc // SUBLANES, SUBLANES, tq), axis=0)
                pt_ref[hh, slot] = p.astype(BF16)
                pltpu.matmul_push_rhs(pt_ref[hh, slot], staging_register=1, mxu_index=hh)
                pltpu.matmul_acc_lhs(O_ADDR, vt_ref[hh * HEAD_DIM:(hh + 1) * HEAD_DIM, pl.ds(start, kc)],
                                     mxu_index=hh, load_staged_rhs=1)
            return tuple(l)

        stage_q()
        stage_a(0, 0)
        stage_a(1, 1)

        def body(g, l):
            for s in range(S_SLOTS):
                c = g * S_SLOTS + s
                stage_a(c + 2, (s + 2) % S_SLOTS)
                l = stage_b(c, s, False, l)
            return l

        l = lax.fori_loop(0, n_iter - 1, body, (jnp.zeros((SUBLANES, tq), F32),) * 2)
        c_last = (n_iter - 1) * S_SLOTS
        stage_a(c_last + 2, 2, restage=False)
        for s in range(S_SLOTS):
            l = stage_b(c_last + s, s, True, l)

        outs = []
        for hh in range(2):
            o = pltpu.matmul_pop(O_ADDR, (HEAD_DIM, tq), F32, mxu_index=hh)
            outs.append(o / jnp.sum(l[hh], axis=0, keepdims=True))
        ot_ref[:, t * tq:(t + 1) * tq] = jnp.concatenate(outs, axis=0)
        return carry

    for t in range(bq // tq):
        q_tile(t, 0)
    o_ref[...] = (ot_ref[...].T * _silu(z_ref[...])).astype(BF16)


def _attention_mxu(qt, k, caug, vt, c, z):
    D, S = qt.shape
    bq = STREAM_BLOCK_Q
    c3 = c.reshape(N_PAIRS, 2, S)
    return pl.pallas_call(
        _attn_mxu_kernel,
        grid=(N_PAIRS, S // bq),
        in_specs=[
            pl.BlockSpec((LANES, bq), lambda p, i: (p, i)),
            pl.BlockSpec((S, LANES), lambda p, i: (0, p)),
            pl.BlockSpec((S, LANES), lambda p, i: (0, 0)),
            pl.BlockSpec((LANES, S), lambda p, i: (p, 0)),
            pl.BlockSpec((1, 2, bq), lambda p, i: (p, 0, i)),
            pl.BlockSpec((bq, LANES), lambda p, i: (i, p)),
        ],
        out_specs=pl.BlockSpec((bq, LANES), lambda p, i: (i, p)),
        out_shape=jax.ShapeDtypeStruct((S, D), BF16),
        scratch_shapes=[pltpu.VMEM((2 * LANES, 2 * bq), BF16), pltpu.VMEM((LANES, bq), F32),
                        pltpu.VMEM((2, S_SLOTS, MXU_TILE, MXU_TILE), BF16)],
        compiler_params=pltpu.CompilerParams(
            dimension_semantics=("arbitrary", "arbitrary"), vmem_limit_bytes=VMEM_LIMIT_BYTES),
        name="fox_attention_mxu",
    )(qt, k, caug, vt, c3, z)


def _out_proj_kernel(x_ref, a_ref, w_ref, o_ref):
    o_ref[...] = x_ref[...] + jnp.dot(a_ref[...], w_ref[...].astype(BF16),
                                      preferred_element_type=F32)


def _out_proj(x, a, w_out):
    S, D = x.shape
    T = OUT_ROW_TILE
    return pl.pallas_call(
        _out_proj_kernel,
        grid=(S // T,),
        in_specs=[
            pl.BlockSpec((T, D), lambda i: (i, 0)),
            pl.BlockSpec((T, D), lambda i: (i, 0)),
            pl.BlockSpec((D, D), lambda i: (0, 0), pipeline_mode=pl.Buffered(1)),
        ],
        out_specs=pl.BlockSpec((T, D), lambda i: (i, 0)),
        out_shape=jax.ShapeDtypeStruct((S, D), F32),
        compiler_params=pltpu.CompilerParams(
            dimension_semantics=("arbitrary",), vmem_limit_bytes=VMEM_LIMIT_BYTES),
        name="attn_out_proj",
    )(x, a, w_out)


def kernel(x, conv_norm_g, conv_w_in, conv_w, conv_w_out, attn_norm_g, attn_w_in,
           attn_b_f, attn_q_norm_g, attn_k_norm_g, attn_w_out):
    B, S, D = x.shape
    assert (B, D) == (1, D_MODEL)
    assert all(S % t == 0 for t in (BLOCK_Q, STREAM_BLOCK_Q, ROW_TILE, OUT_ROW_TILE))
    assert conv_w_in.shape[0] == 1 and attn_w_in.shape[0] == 1
    x2 = x.reshape(S, D)

    x2 = _conv_layer(x2, conv_norm_g[0].reshape(1, D), conv_w_in[0], conv_w[0], conv_w_out[0])

    qt, k, vt, z, c, caug = _attn_proj(
        x2, attn_norm_g[0].reshape(1, D), jnp.swapaxes(attn_w_in, 1, 2),
        attn_b_f[0].reshape(N_HEADS, 1),
        jnp.tile(attn_q_norm_g[0], N_HEADS).reshape(D, 1),
        jnp.tile(attn_k_norm_g[0], N_HEADS).reshape(D, 1))
    logit_bound = (math.sqrt(HEAD_DIM) * jnp.max(jnp.abs(attn_q_norm_g[0]))
                   * jnp.max(jnp.abs(attn_k_norm_g[0])))
    a = lax.cond(logit_bound <= LOGIT_BOUND_MAX,
                 lambda: _attention_mxu(qt, k, caug, vt, c, z),
                 lambda: _attention(qt.T, k, vt.T, c, z))
    out = _out_proj(x2, a, attn_w_out[0])
    return out.reshape(B, S, D)
```

```python
import math

import jax
import jax.numpy as jnp
from jax import lax
from jax.experimental import pallas as pl
from jax.experimental.pallas import tpu as pltpu

D_MODEL = 1024
HEAD_DIM = 64
N_HEADS = D_MODEL // HEAD_DIM
N_PAIRS = N_HEADS // 2
CONV_WIDTH = 3
RMS_EPS = 1e-6
LOG2E = math.log2(math.e)

LANES = 128
SUBLANES = 8
VMEM_LIMIT_BYTES = 56 * 1024 * 1024

ROW_TILE = 512
OUT_ROW_TILE = 1024
BLOCK_Q = 512
BLOCK_K = 512
NEG_BIG = -1e30
STREAM_BLOCK_Q = 2048
AUG_MID = N_HEADS
AUG_LO = 2 * N_HEADS
AUG_ONE = 3 * N_HEADS
LOGIT_BOUND_MAX = 60.0

BF16 = jnp.bfloat16
F32 = jnp.float32


def _rmsnorm_rows(x, g):
    inv = lax.rsqrt(jnp.mean(x * x, axis=-1, keepdims=True) + RMS_EPS)
    return x * inv * g


def _silu(z):
    return z * jax.nn.sigmoid(z)


def _conv_layer_kernel(x_ref, g_ref, w_in_ref, cw_ref, w_out_ref, o_ref, tail_ref):
    i = pl.program_id(0)
    D = D_MODEL
    T = x_ref.shape[0]

    @pl.when(i == 0)
    def _():
        tail_ref[...] = jnp.zeros_like(tail_ref)

    x = x_ref[...]
    h = _rmsnorm_rows(x, g_ref[...])
    proj = jnp.dot(h, w_in_ref[...], preferred_element_type=F32)
    b_g = proj[:, 0 * D:1 * D]
    c_g = proj[:, 1 * D:2 * D]
    xin = proj[:, 2 * D:3 * D]
    z = proj[:, 3 * D:4 * D]
    u = c_g * xin

    tail = tail_ref[...]
    prev1 = tail[SUBLANES - 1:SUBLANES, :]
    prev2 = tail[SUBLANES - 2:SUBLANES - 1, :]
    row = lax.broadcasted_iota(jnp.int32, (T, 1), 0)
    u1 = jnp.where(row == 0, prev1, pltpu.roll(u, 1, axis=0))
    u2 = jnp.where(row == 0, prev2, jnp.where(row == 1, prev1, pltpu.roll(u, 2, axis=0)))
    tail_ref[...] = u[T - SUBLANES:, :]

    cw = cw_ref[...]
    y = cw[0:1, :] * u2 + cw[1:2, :] * u1 + cw[2:3, :] * u
    y = b_g * y * _silu(z)
    o_ref[...] = x + jnp.dot(y, w_out_ref[...], preferred_element_type=F32)


def _conv_layer(x, g, w_in, cw, w_out):
    S, D = x.shape
    const = lambda i: (0, 0)
    return pl.pallas_call(
        _conv_layer_kernel,
        grid=(S // ROW_TILE,),
        in_specs=[
            pl.BlockSpec((ROW_TILE, D), lambda i: (i, 0)),
            pl.BlockSpec((1, D), const),
            pl.BlockSpec((D, 4 * D), const, pipeline_mode=pl.Buffered(1)),
            pl.BlockSpec((CONV_WIDTH, D), const),
            pl.BlockSpec((D, D), const, pipeline_mode=pl.Buffered(1)),
        ],
        out_specs=pl.BlockSpec((ROW_TILE, D), lambda i: (i, 0)),
        out_shape=jax.ShapeDtypeStruct((S, D), F32),
        scratch_shapes=[pltpu.VMEM((SUBLANES, D), F32)],
        compiler_params=pltpu.CompilerParams(
            dimension_semantics=("arbitrary",), vmem_limit_bytes=VMEM_LIMIT_BYTES),
        name="conv_layer",
    )(x, g, w_in, cw, w_out)


def _split3(x):
    hi = x.astype(BF16).astype(F32)
    r = x - hi
    mid = r.astype(BF16).astype(F32)
    lo = (r - mid).astype(BF16).astype(F32)
    return hi, mid, lo


def _head_rmsnorm_t(xt, g_col):
    t = xt.shape[1]
    x3 = xt.reshape(N_HEADS, HEAD_DIM, t)
    inv = lax.rsqrt(jnp.mean(x3 * x3, axis=1, keepdims=True) + RMS_EPS)
    return (x3 * inv * g_col.reshape(N_HEADS, HEAD_DIM, 1)).reshape(N_HEADS * HEAD_DIM, t)


def _attn_proj_kernel(x_ref, g_ref, wq_ref, wk_ref, wv_ref, wz_ref, wf_ref, bf_ref, qg_ref, kg_ref,
                      tri_ref,
                      qt_ref, k_ref, vt_ref, z_ref, c_ref, caug_ref, run_ref):
    i = pl.program_id(0)
    D = D_MODEL
    T = x_ref.shape[0]

    @pl.when(i == 0)
    def _():
        run_ref[...] = jnp.zeros_like(run_ref)

    trans_b = (((1,), (1,)), ((), ()))
    h = _rmsnorm_rows(x_ref[...], g_ref[...])
    z_ref[...] = lax.dot_general(h, wz_ref[...], trans_b, preferred_element_type=F32)

    qt = lax.dot_general(wq_ref[...], h, trans_b, preferred_element_type=F32)
    scale = LOG2E / math.sqrt(HEAD_DIM)
    qt_ref[...] = (_head_rmsnorm_t(qt, qg_ref[...]) * scale).astype(BF16)
    kt = lax.dot_general(wk_ref[...], h, trans_b, preferred_element_type=F32)
    k_ref[...] = _head_rmsnorm_t(kt, kg_ref[...]).T.astype(BF16)
    vt_ref[...] = lax.dot_general(wv_ref[...], h, trans_b,
                                  preferred_element_type=F32).astype(BF16)

    f = lax.dot_general(wf_ref[...], h, trans_b, preferred_element_type=F32) + bf_ref[...]
    logf = -(jnp.maximum(-f, 0.0) + jnp.log1p(jnp.exp(-jnp.abs(f))))
    pieces = jnp.concatenate(_split3(logf), axis=0).astype(BF16)
    cs = jnp.dot(pieces, tri_ref[...], preferred_element_type=F32)
    cs = cs[:N_HEADS] + cs[N_HEADS:2 * N_HEADS] + cs[2 * N_HEADS:]
    c = cs + run_ref[...][:, 0:1]
    c_ref[...] = c
    run_ref[...] = jnp.broadcast_to(c[:, T - 1:T], run_ref.shape)

    crow = jnp.concatenate([c, jnp.zeros((LANES - N_HEADS, T), F32)], axis=0).T
    hi, mid, lo = _split3(crow * LOG2E)
    lane = lax.broadcasted_iota(jnp.int32, (1, LANES), 1)
    ones = jnp.where((lane >= AUG_ONE) & (lane < AUG_ONE + 3), 1.0, 0.0)
    caug = hi + pltpu.roll(mid, AUG_MID, axis=1) + pltpu.roll(lo, AUG_LO, axis=1) + ones
    caug_ref[...] = caug.astype(BF16)


def _attn_proj(x, g, w_in_t, b_f, q_g_col, k_g_col):
    S, D = x.shape
    T = ROW_TILE
    tri = (jnp.arange(T)[:, None] <= jnp.arange(T)[None, :]).astype(BF16)
    const = lambda i: (0, 0)
    rows = lambda i: (i, 0)
    cols = lambda i: (0, i)
    return pl.pallas_call(
        _attn_proj_kernel,
        grid=(S // T,),
        in_specs=[
            pl.BlockSpec((T, D), rows),
            pl.BlockSpec((1, D), const),
            pl.BlockSpec((None, D, D), lambda i: (0, 0, 0), pipeline_mode=pl.Buffered(1)),
            pl.BlockSpec((None, D, D), lambda i: (0, 1, 0), pipeline_mode=pl.Buffered(1)),
            pl.BlockSpec((None, D, D), lambda i: (0, 2, 0), pipeline_mode=pl.Buffered(1)),
            pl.BlockSpec((None, D, D), lambda i: (0, 3, 0), pipeline_mode=pl.Buffered(1)),
            pl.BlockSpec((None, N_HEADS, D), lambda i: (0, 4 * D // N_HEADS, 0)),
            pl.BlockSpec((N_HEADS, 1), const),
            pl.BlockSpec((D, 1), const),
            pl.BlockSpec((D, 1), const),
            pl.BlockSpec((T, T), const),
        ],
        out_specs=[
            pl.BlockSpec((D, T), cols),
            pl.BlockSpec((T, D), rows),
            pl.BlockSpec((D, T), cols),
            pl.BlockSpec((T, D), rows),
            pl.BlockSpec((N_HEADS, T), cols),
            pl.BlockSpec((T, LANES), rows),
        ],
        out_shape=[
            jax.ShapeDtypeStruct((D, S), BF16),
            jax.ShapeDtypeStruct((S, D), BF16),
            jax.ShapeDtypeStruct((D, S), BF16),
            jax.ShapeDtypeStruct((S, D), F32),
            jax.ShapeDtypeStruct((N_HEADS, S), F32),
            jax.ShapeDtypeStruct((S, LANES), BF16),
        ],
        scratch_shapes=[pltpu.VMEM((N_HEADS, LANES), F32)],
        compiler_params=pltpu.CompilerParams(
            dimension_semantics=("arbitrary",), vmem_limit_bytes=VMEM_LIMIT_BYTES),
        name="attn_proj",
    )(x, g, w_in_t, w_in_t, w_in_t, w_in_t, w_in_t, b_f, q_g_col, k_g_col, tri)


def _attn_kernel(q_ref, k_ref, v_ref, c_ref, z_ref, o_ref, m_ref, l_ref, acc_ref):
    i = pl.program_id(1)
    bq, bk = BLOCK_Q, BLOCK_K
    q2 = q_ref[...]
    lane = lax.broadcasted_iota(jnp.int32, (1, LANES), 1)
    zero = jnp.zeros_like(q2)
    q_heads = (jnp.where(lane < HEAD_DIM, q2, zero), jnp.where(lane >= HEAD_DIM, q2, zero))

    m_ref[...] = jnp.full_like(m_ref, NEG_BIG)
    l_ref[...] = jnp.zeros_like(l_ref)
    acc_ref[...] = jnp.zeros_like(acc_ref)

    def step(j, masked):
        start = pl.multiple_of(j * bk, bk)
        kb = k_ref[pl.ds(start, bk), :]
        vb = v_ref[pl.ds(start, bk), :]
        cb = c_ref[0, :, pl.ds(start, bk)] * LOG2E
        for hh in range(2):
            s = lax.dot_general(q_heads[hh], kb, (((1,), (1,)), ((), ())),
                                preferred_element_type=F32)
            s = s - cb[hh:hh + 1, :]
            if masked:
                row = lax.broadcasted_iota(jnp.int32, (bq, bk), 0)
                col = lax.broadcasted_iota(jnp.int32, (bq, bk), 1)
                s = jnp.where(col <= row, s, NEG_BIG)
            m_old = m_ref[hh]
            m_new = jnp.maximum(m_old, jnp.max(s, axis=-1, keepdims=True))
            alpha = jnp.exp2(m_old - m_new)
            p = jnp.exp2(s - m_new)
            l_ref[hh] = alpha * l_ref[hh] + jnp.sum(p, axis=-1, keepdims=True)
            acc_ref[hh] = alpha * acc_ref[hh] + jnp.dot(
                p.astype(BF16), vb, preferred_element_type=F32)
            m_ref[hh] = m_new

    def body(j, carry):
        step(j, masked=False)
        return carry

    lax.fori_loop(0, i, body, 0)
    step(i, masked=True)

    o0 = acc_ref[0] / l_ref[0]
    o1 = acc_ref[1] / l_ref[1]
    o = jnp.where(lane < HEAD_DIM, o0, o1)
    o_ref[...] = (o * _silu(z_ref[...])).astype(BF16)


def _attention(q, k, v, c, z):
    S, D = q.shape
    bq = BLOCK_Q
    c3 = c.reshape(N_PAIRS, 2, S)
    return pl.pallas_call(
        _attn_kernel,
        grid=(N_PAIRS, S // bq),
        in_specs=[
            pl.BlockSpec((bq, LANES), lambda p, i: (i, p)),
            pl.BlockSpec((S, LANES), lambda p, i: (0, p)),
            pl.BlockSpec((S, LANES), lambda p, i: (0, p)),
            pl.BlockSpec((1, 2, S), lambda p, i: (p, 0, 0)),
            pl.BlockSpec((bq, LANES), lambda p, i: (i, p)),
        ],
        out_specs=pl.BlockSpec((bq, LANES), lambda p, i: (i, p)),
        out_shape=jax.ShapeDtypeStruct((S, D), BF16),
        scratch_shapes=[
            pltpu.VMEM((2, bq, 1), F32),
            pltpu.VMEM((2, bq, 1), F32),
            pltpu.VMEM((2, bq, LANES), F32),
        ],
        compiler_params=pltpu.CompilerParams(
            dimension_semantics=("arbitrary", "arbitrary"), vmem_limit_bytes=VMEM_LIMIT_BYTES),
        name="fox_attention",
    )(q, k, v, c3, z)


MXU_TILE = 256
S_SLOTS = 3
S_TILE_ENTRIES = MXU_TILE * MXU_TILE // (SUBLANES * LANES)
S_ADDR = tuple(s * S_TILE_ENTRIES for s in range(S_SLOTS))
O_ADDR = S_SLOTS * S_TILE_ENTRIES
ONES_ROWS = 16


def _attn_mxu_kernel(qt_ref, k_ref, caug_ref, vt_ref, c_ref, z_ref, o_ref, qa_ref, ot_ref, pt_ref):
    pair = pl.program_id(0)
    i = pl.program_id(1)
    bq, tq, kc = STREAM_BLOCK_Q, MXU_TILE, MXU_TILE

    qt = qt_ref[...]
    r = lax.broadcasted_iota(jnp.int32, (LANES, 1), 0)
    zero = jnp.zeros_like(qt)
    top = jnp.concatenate([jnp.where(r < HEAD_DIM, qt, zero),
                           jnp.where(r >= HEAD_DIM, qt, zero)], axis=1)
    ci = c_ref[0] * LOG2E
    halves = []
    for hh in range(2):
        head = 2 * pair + hh
        c_hi, c_mid, c_lo = _split3(ci[hh:hh + 1, :])
        minus_cj = jnp.where((r == head) | (r == AUG_MID + head) | (r == AUG_LO + head), -1.0, 0.0)
        halves.append(minus_cj + jnp.where(r == AUG_ONE, c_hi, 0.0)
                      + jnp.where(r == AUG_ONE + 1, c_mid, 0.0)
                      + jnp.where(r == AUG_ONE + 2, c_lo, 0.0))
    qa_ref[...] = jnp.concatenate([top, jnp.concatenate(halves, axis=1).astype(BF16)], axis=0)
    ones_rows = jnp.ones((ONES_ROWS, kc), BF16)

    n_tiles = bq // tq

    def n_real(t):
        return (i * bq + t * tq) // kc + 1

    def chunk_start(t, c):
        return pl.multiple_of(jnp.minimum(c, n_real(t) - 1) * kc, kc)

    def stage_q(t):
        for hh in range(2):
            pltpu.matmul_push_rhs(qa_ref[:, pl.ds(hh * bq + t * tq, tq)],
                                  staging_register=0, mxu_index=hh)

    def stage_a(t, c, slot, restage):
        start = chunk_start(t, c)
        ka = jnp.concatenate([k_ref[pl.ds(start, kc), :], caug_ref[pl.ds(start, kc), :]],
                             axis=1)
        for hh in range(2):
            pltpu.matmul_acc_lhs(S_ADDR[slot], ka, mxu_index=hh, load_staged_rhs=0)
        if restage is not None:
            stage_q(restage)

    def stage_b(t, c, slot, masked):
        start = chunk_start(t, c)
        for hh in range(2):
            s = pltpu.matmul_pop(S_ADDR[slot], (kc, tq), F32, mxu_index=hh)
            if masked:
                key = c * kc + lax.broadcasted_iota(jnp.int32, (kc, tq), 0)
                qry = i * bq + t * tq + lax.broadcasted_iota(jnp.int32, (kc, tq), 1)
                s = jnp.where(key <= qry, s, NEG_BIG)
            pt_ref[hh, slot] = jnp.exp2(s).astype(BF16)
            pltpu.matmul_push_rhs(pt_ref[hh, slot], staging_register=1, mxu_index=hh)
            vta = jnp.concatenate(
                [vt_ref[hh * HEAD_DIM:(hh + 1) * HEAD_DIM, pl.ds(start, kc)], ones_rows],
                axis=0)
            pltpu.matmul_acc_lhs(O_ADDR, vta, mxu_index=hh, load_staged_rhs=1)

    stage_q(0)
    stage_a(0, 0, 0, restage=0)
    stage_a(0, 1, 1, restage=0)
    for t in range(n_tiles):
        n_iter = (n_real(t) + S_SLOTS - 1) // S_SLOTS

        def body(g, carry, t=t):
            for s in range(S_SLOTS):
                c = g * S_SLOTS + s
                stage_a(t, c + 2, (s + 2) % S_SLOTS, restage=t)
                stage_b(t, c, s, False)
            return carry

        lax.fori_loop(0, n_iter - 1, body, 0)
        c_last = (n_iter - 1) * S_SLOTS
        following = t + 1 if t + 1 < n_tiles else None
        stage_a(t, c_last + 2, 2, restage=following)
        stage_b(t, c_last, 0, True)
        if following is not None:
            stage_a(following, 0, 0, restage=following)
        stage_b(t, c_last + 1, 1, True)
        stage_b(t, c_last + 2, 2, True)

        outs = []
        for hh in range(2):
            o = pltpu.matmul_pop(O_ADDR, (HEAD_DIM + ONES_ROWS, tq), F32, mxu_index=hh)
            outs.append(o[:HEAD_DIM, :] / o[HEAD_DIM:HEAD_DIM + 1, :])
        ot_ref[:, t * tq:(t + 1) * tq] = jnp.concatenate(outs, axis=0)
        if following is not None:
            stage_a(following, 1, 1, restage=following)
    o_ref[...] = (ot_ref[...].T * _silu(z_ref[...])).astype(BF16)


def _attention_mxu(qt, k, caug, vt, c, z):
    D, S = qt.shape
    bq = STREAM_BLOCK_Q
    c3 = c.reshape(N_PAIRS, 2, S)
    return pl.pallas_call(
        _attn_mxu_kernel,
        grid=(N_PAIRS, S // bq),
        in_specs=[
            pl.BlockSpec((LANES, bq), lambda p, i: (p, i)),
            pl.BlockSpec((S, LANES), lambda p, i: (0, p)),
            pl.BlockSpec((S, LANES), lambda p, i: (0, 0)),
            pl.BlockSpec((LANES, S), lambda p, i: (p, 0)),
            pl.BlockSpec((1, 2, bq), lambda p, i: (p, 0, i)),
            pl.BlockSpec((bq, LANES), lambda p, i: (i, p)),
        ],
        out_specs=pl.BlockSpec((bq, LANES), lambda p, i: (i, p)),
        out_shape=jax.ShapeDtypeStruct((S, D), BF16),
        scratch_shapes=[pltpu.VMEM((2 * LANES, 2 * bq), BF16), pltpu.VMEM((LANES, bq), F32),
                        pltpu.VMEM((2, S_SLOTS, MXU_TILE, MXU_TILE), BF16)],
        compiler_params=pltpu.CompilerParams(
            dimension_semantics=("arbitrary", "arbitrary"), vmem_limit_bytes=VMEM_LIMIT_BYTES),
        name="fox_attention_mxu",
    )(qt, k, caug, vt, c3, z)


def _out_proj_kernel(x_ref, a_ref, w_ref, o_ref):
    o_ref[...] = x_ref[...] + jnp.dot(a_ref[...], w_ref[...].astype(BF16),
                                      preferred_element_type=F32)


def _out_proj(x, a, w_out):
    S, D = x.shape
    T = OUT_ROW_TILE
    return pl.pallas_call(
        _out_proj_kernel,
        grid=(S // T,),
        in_specs=[
            pl.BlockSpec((T, D), lambda i: (i, 0)),
            pl.BlockSpec((T, D), lambda i: (i, 0)),
            pl.BlockSpec((D, D), lambda i: (0, 0), pipeline_mode=pl.Buffered(1)),
        ],
        out_specs=pl.BlockSpec((T, D), lambda i: (i, 0)),
        out_shape=jax.ShapeDtypeStruct((S, D), F32),
        compiler_params=pltpu.CompilerParams(
            dimension_semantics=("arbitrary",), vmem_limit_bytes=VMEM_LIMIT_BYTES),
        name="attn_out_proj",
    )(x, a, w_out)


def kernel(x, conv_norm_g, conv_w_in, conv_w, conv_w_out, attn_norm_g, attn_w_in,
           attn_b_f, attn_q_norm_g, attn_k_norm_g, attn_w_out):
    B, S, D = x.shape
    assert (B, D) == (1, D_MODEL)
    assert all(S % t == 0 for t in (BLOCK_Q, STREAM_BLOCK_Q, ROW_TILE, OUT_ROW_TILE))
    assert conv_w_in.shape[0] == 1 and attn_w_in.shape[0] == 1
    x2 = x.reshape(S, D)

    x2 = _conv_layer(x2, conv_norm_g[0].reshape(1, D), conv_w_in[0], conv_w[0], conv_w_out[0])

    qt, k, vt, z, c, caug = _attn_proj(
        x2, attn_norm_g[0].reshape(1, D), jnp.swapaxes(attn_w_in, 1, 2),
        attn_b_f[0].reshape(N_HEADS, 1),
        jnp.tile(attn_q_norm_g[0], N_HEADS).reshape(D, 1),
        jnp.tile(attn_k_norm_g[0], N_HEADS).reshape(D, 1))
    logit_bound = (math.sqrt(HEAD_DIM) * jnp.max(jnp.abs(attn_q_norm_g[0]))
                   * jnp.max(jnp.abs(attn_k_norm_g[0])))
    a = lax.cond(logit_bound <= LOGIT_BOUND_MAX,
                 lambda: _attention_mxu(qt, k, caug, vt, c, z),
                 lambda: _attention(qt.T, k, vt.T, c, z))
    out = _out_proj(x2, a, attn_w_out[0])
    return out.reshape(B, S, D)
```

```python
import math

import jax
import jax.numpy as jnp
from jax import lax
from jax.experimental import pallas as pl
from jax.experimental.pallas import tpu as pltpu

D_MODEL = 1024
HEAD_DIM = 64
N_HEADS = D_MODEL // HEAD_DIM
N_PAIRS = N_HEADS // 2
CONV_WIDTH = 3
RMS_EPS = 1e-6
LOG2E = math.log2(math.e)

LANES = 128
SUBLANES = 8
VMEM_LIMIT_BYTES = 56 * 1024 * 1024

ROW_TILE = 512
OUT_ROW_TILE = 1024
BLOCK_Q = 512
BLOCK_K = 512
NEG_BIG = -1e30
STREAM_BLOCK_Q = 2048
AUG_MID = N_HEADS
AUG_LO = 2 * N_HEADS
AUG_ONE = 3 * N_HEADS
LOGIT_BOUND_MAX = 60.0

BF16 = jnp.bfloat16
F32 = jnp.float32


def _rmsnorm_rows(x, g):
    inv = lax.rsqrt(jnp.mean(x * x, axis=-1, keepdims=True) + RMS_EPS)
    return x * inv * g


def _silu(z):
    return z * jax.nn.sigmoid(z)


def _conv_layer_kernel(x_ref, g_ref, w_in_ref, cw_ref, w_out_ref, o_ref, tail_ref):
    i = pl.program_id(0)
    D = D_MODEL
    T = x_ref.shape[0]

    @pl.when(i == 0)
    def _():
        tail_ref[...] = jnp.zeros_like(tail_ref)

    x = x_ref[...]
    h = _rmsnorm_rows(x, g_ref[...])
    proj = jnp.dot(h, w_in_ref[...], preferred_element_type=F32)
    b_g = proj[:, 0 * D:1 * D]
    c_g = proj[:, 1 * D:2 * D]
    xin = proj[:, 2 * D:3 * D]
    z = proj[:, 3 * D:4 * D]
    u = c_g * xin

    tail = tail_ref[...]
    prev1 = tail[SUBLANES - 1:SUBLANES, :]
    prev2 = tail[SUBLANES - 2:SUBLANES - 1, :]
    row = lax.broadcasted_iota(jnp.int32, (T, 1), 0)
    u1 = jnp.where(row == 0, prev1, pltpu.roll(u, 1, axis=0))
    u2 = jnp.where(row == 0, prev2, jnp.where(row == 1, prev1, pltpu.roll(u, 2, axis=0)))
    tail_ref[...] = u[T - SUBLANES:, :]

    cw = cw_ref[...]
    y = cw[0:1, :] * u2 + cw[1:2, :] * u1 + cw[2:3, :] * u
    y = b_g * y * _silu(z)
    o_ref[...] = x + jnp.dot(y, w_out_ref[...], preferred_element_type=F32)


def _conv_layer(x, g, w_in, cw, w_out):
    S, D = x.shape
    const = lambda i: (0, 0)
    return pl.pallas_call(
        _conv_layer_kernel,
        grid=(S // ROW_TILE,),
        in_specs=[
            pl.BlockSpec((ROW_TILE, D), lambda i: (i, 0)),
            pl.BlockSpec((1, D), const),
            pl.BlockSpec((D, 4 * D), const, pipeline_mode=pl.Buffered(1)),
            pl.BlockSpec((CONV_WIDTH, D), const),
            pl.BlockSpec((D, D), const, pipeline_mode=pl.Buffered(1)),
        ],
        out_specs=pl.BlockSpec((ROW_TILE, D), lambda i: (i, 0)),
        out_shape=jax.ShapeDtypeStruct((S, D), F32),
        scratch_shapes=[pltpu.VMEM((SUBLANES, D), F32)],
        compiler_params=pltpu.CompilerParams(
            dimension_semantics=("arbitrary",), vmem_limit_bytes=VMEM_LIMIT_BYTES),
        name="conv_layer",
    )(x, g, w_in, cw, w_out)


def _split3(x):
    hi = x.astype(BF16).astype(F32)
    r = x - hi
    mid = r.astype(BF16).astype(F32)
    lo = (r - mid).astype(BF16).astype(F32)
    return hi, mid, lo


def _head_rmsnorm_t(xt, g_col):
    t = xt.shape[1]
    x3 = xt.reshape(N_HEADS, HEAD_DIM, t)
    inv = lax.rsqrt(jnp.mean(x3 * x3, axis=1, keepdims=True) + RMS_EPS)
    return (x3 * inv * g_col.reshape(N_HEADS, HEAD_DIM, 1)).reshape(N_HEADS * HEAD_DIM, t)


def _attn_proj_kernel(x_ref, g_ref, wq_ref, wk_ref, wv_ref, wz_ref, wf_ref, bf_ref, qg_ref, kg_ref,
                      tri_ref,
                      qt_ref, k_ref, vt_ref, z_ref, c_ref, caug_ref, run_ref):
    i = pl.program_id(0)
    D = D_MODEL
    T = x_ref.shape[0]

    @pl.when(i == 0)
    def _():
        run_ref[...] = jnp.zeros_like(run_ref)

    trans_b = (((1,), (1,)), ((), ()))
    h = _rmsnorm_rows(x_ref[...], g_ref[...])
    z_ref[...] = lax.dot_general(h, wz_ref[...], trans_b, preferred_element_type=F32)

    qt = lax.dot_general(wq_ref[...], h, trans_b, preferred_element_type=F32)
    scale = LOG2E / math.sqrt(HEAD_DIM)
    qt_ref[...] = (_head_rmsnorm_t(qt, qg_ref[...]) * scale).astype(BF16)
    kt = lax.dot_general(wk_ref[...], h, trans_b, preferred_element_type=F32)
    k_ref[...] = _head_rmsnorm_t(kt, kg_ref[...]).T.astype(BF16)
    vt_ref[...] = lax.dot_general(wv_ref[...], h, trans_b,
                                  preferred_element_type=F32).astype(BF16)

    f = lax.dot_general(wf_ref[...], h, trans_b, preferred_element_type=F32) + bf_ref[...]
    logf = -(jnp.maximum(-f, 0.0) + jnp.log1p(jnp.exp(-jnp.abs(f))))
    pieces = jnp.concatenate(_split3(logf), axis=0).astype(BF16)
    cs = jnp.dot(pieces, tri_ref[...], preferred_element_type=F32)
    cs = cs[:N_HEADS] + cs[N_HEADS:2 * N_HEADS] + cs[2 * N_HEADS:]
    c = cs + run_ref[...][:, 0:1]
    c_ref[...] = c
    run_ref[...] = jnp.broadcast_to(c[:, T - 1:T], run_ref.shape)

    crow = jnp.concatenate([c, jnp.zeros((LANES - N_HEADS, T), F32)], axis=0).T
    hi, mid, lo = _split3(crow * LOG2E)
    lane = lax.broadcasted_iota(jnp.int32, (1, LANES), 1)
    ones = jnp.where((lane >= AUG_ONE) & (lane < AUG_ONE + 3), 1.0, 0.0)
    caug = hi + pltpu.roll(mid, AUG_MID, axis=1) + pltpu.roll(lo, AUG_LO, axis=1) + ones
    caug_ref[...] = caug.astype(BF16)


def _attn_proj(x, g, w_in_t, b_f, q_g_col, k_g_col):
    S, D = x.shape
    T = ROW_TILE
    tri = (jnp.arange(T)[:, None] <= jnp.arange(T)[None, :]).astype(BF16)
    const = lambda i: (0, 0)
    rows = lambda i: (i, 0)
    cols = lambda i: (0, i)
    return pl.pallas_call(
        _attn_proj_kernel,
        grid=(S // T,),
        in_specs=[
            pl.BlockSpec((T, D), rows),
            pl.BlockSpec((1, D), const),
            pl.BlockSpec((None, D, D), lambda i: (0, 0, 0), pipeline_mode=pl.Buffered(1)),
            pl.BlockSpec((None, D, D), lambda i: (0, 1, 0), pipeline_mode=pl.Buffered(1)),
            pl.BlockSpec((None, D, D), lambda i: (0, 2, 0), pipeline_mode=pl.Buffered(1)),
            pl.BlockSpec((None, D, D), lambda i: (0, 3, 0), pipeline_mode=pl.Buffered(1)),
            pl.BlockSpec((None, N_HEADS, D), lambda i: (0, 4 * D // N_HEADS, 0)),
            pl.BlockSpec((N_HEADS, 1), const),
            pl.BlockSpec((D, 1), const),
            pl.BlockSpec((D, 1), const),
            pl.BlockSpec((T, T), const),
        ],
        out_specs=[
            pl.BlockSpec((D, T), cols),
            pl.BlockSpec((T, D), rows),
            pl.BlockSpec((D, T), cols),
            pl.BlockSpec((T, D), rows),
            pl.BlockSpec((N_HEADS, T), cols),
            pl.BlockSpec((T, LANES), rows),
        ],
        out_shape=[
            jax.ShapeDtypeStruct((D, S), BF16),
            jax.ShapeDtypeStruct((S, D), BF16),
            jax.ShapeDtypeStruct((D, S), BF16),
            jax.ShapeDtypeStruct((S, D), F32),
            jax.ShapeDtypeStruct((N_HEADS, S), F32),
            jax.ShapeDtypeStruct((S, LANES), BF16),
        ],
        scratch_shapes=[pltpu.VMEM((N_HEADS, LANES), F32)],
        compiler_params=pltpu.CompilerParams(
            dimension_semantics=("arbitrary",), vmem_limit_bytes=VMEM_LIMIT_BYTES),
        name="attn_proj",
    )(x, g, w_in_t, w_in_t, w_in_t, w_in_t, w_in_t, b_f, q_g_col, k_g_col, tri)


def _attn_kernel(q_ref, k_ref, v_ref, c_ref, z_ref, o_ref, m_ref, l_ref, acc_ref):
    i = pl.program_id(1)
    bq, bk = BLOCK_Q, BLOCK_K
    q2 = q_ref[...]
    lane = lax.broadcasted_iota(jnp.int32, (1, LANES), 1)
    zero = jnp.zeros_like(q2)
    q_heads = (jnp.where(lane < HEAD_DIM, q2, zero), jnp.where(lane >= HEAD_DIM, q2, zero))

    m_ref[...] = jnp.full_like(m_ref, NEG_BIG)
    l_ref[...] = jnp.zeros_like(l_ref)
    acc_ref[...] = jnp.zeros_like(acc_ref)

    def step(j, masked):
        start = pl.multiple_of(j * bk, bk)
        kb = k_ref[pl.ds(start, bk), :]
        vb = v_ref[pl.ds(start, bk), :]
        cb = c_ref[0, :, pl.ds(start, bk)] * LOG2E
        for hh in range(2):
            s = lax.dot_general(q_heads[hh], kb, (((1,), (1,)), ((), ())),
                                preferred_element_type=F32)
            s = s - cb[hh:hh + 1, :]
            if masked:
                row = lax.broadcasted_iota(jnp.int32, (bq, bk), 0)
                col = lax.broadcasted_iota(jnp.int32, (bq, bk), 1)
                s = jnp.where(col <= row, s, NEG_BIG)
            m_old = m_ref[hh]
            m_new = jnp.maximum(m_old, jnp.max(s, axis=-1, keepdims=True))
            alpha = jnp.exp2(m_old - m_new)
            p = jnp.exp2(s - m_new)
            l_ref[hh] = alpha * l_ref[hh] + jnp.sum(p, axis=-1, keepdims=True)
            acc_ref[hh] = alpha * acc_ref[hh] + jnp.dot(
                p.astype(BF16), vb, preferred_element_type=F32)
            m_ref[hh] = m_new

    def body(j, carry):
        step(j, masked=False)
        return carry

    lax.fori_loop(0, i, body, 0)
    step(i, masked=True)

    o0 = acc_ref[0] / l_ref[0]
    o1 = acc_ref[1] / l_ref[1]
    o = jnp.where(lane < HEAD_DIM, o0, o1)
    o_ref[...] = (o * _silu(z_ref[...])).astype(BF16)


def _attention(q, k, v, c, z):
    S, D = q.shape
    bq = BLOCK_Q
    c3 = c.reshape(N_PAIRS, 2, S)
    return pl.pallas_call(
        _attn_kernel,
        grid=(N_PAIRS, S // bq),
        in_specs=[
            pl.BlockSpec((bq, LANES), lambda p, i: (i, p)),
            pl.BlockSpec((S, LANES), lambda p, i: (0, p)),
            pl.BlockSpec((S, LANES), lambda p, i: (0, p)),
            pl.BlockSpec((1, 2, S), lambda p, i: (p, 0, 0)),
            pl.BlockSpec((bq, LANES), lambda p, i: (i, p)),
        ],
        out_specs=pl.BlockSpec((bq, LANES), lambda p, i: (i, p)),
        out_shape=jax.ShapeDtypeStruct((S, D), BF16),
        scratch_shapes=[
            pltpu.VMEM((2, bq, 1), F32),
            pltpu.VMEM((2, bq, 1), F32),
            pltpu.VMEM((2, bq, LANES), F32),
        ],
        compiler_params=pltpu.CompilerParams(
            dimension_semantics=("arbitrary", "arbitrary"), vmem_limit_bytes=VMEM_LIMIT_BYTES),
        name="fox_attention",
    )(q, k, v, c3, z)


MXU_TILE = 256
S_SLOTS = 3
S_TILE_ENTRIES = MXU_TILE * MXU_TILE // (SUBLANES * LANES)
S_ADDR = tuple(s * S_TILE_ENTRIES for s in range(S_SLOTS))
O_ADDR = S_SLOTS * S_TILE_ENTRIES
ONES_ROWS = 16


def _attn_mxu_kernel(qt_ref, k_ref, caug_ref, vt_ref, c_ref, z_ref, o_ref, qa_ref, ot_ref, pt_ref):
    pair = pl.program_id(0)
    i = pl.program_id(1)
    bq, tq, kc = STREAM_BLOCK_Q, MXU_TILE, MXU_TILE

    r = lax.broadcasted_iota(jnp.int32, (LANES, 1), 0)

    def build_q(t):
        cols = slice(t * tq, (t + 1) * tq)
        qt = qt_ref[:, cols]
        zero = jnp.zeros_like(qt)
        ci = c_ref[0, :, cols] * LOG2E
        for hh in range(2):
            head = 2 * pair + hh
            top = jnp.where((r < HEAD_DIM) if hh == 0 else (r >= HEAD_DIM), qt, zero)
            c_hi, c_mid, c_lo = _split3(ci[hh:hh + 1, :])
            minus_cj = jnp.where((r == head) | (r == AUG_MID + head) | (r == AUG_LO + head),
                                 -1.0, 0.0)
            aug = (minus_cj + jnp.where(r == AUG_ONE, c_hi, 0.0)
                   + jnp.where(r == AUG_ONE + 1, c_mid, 0.0)
                   + jnp.where(r == AUG_ONE + 2, c_lo, 0.0))
            qa_ref[:, hh * bq + t * tq:hh * bq + (t + 1) * tq] = jnp.concatenate(
                [top, aug.astype(BF16)], axis=0)

    build_q(0)
    ones_rows = jnp.ones((ONES_ROWS, kc), BF16)

    n_tiles = bq // tq

    def n_real(t):
        return (i * bq + t * tq) // kc + 1

    def chunk_start(t, c):
        return pl.multiple_of(jnp.minimum(c, n_real(t) - 1) * kc, kc)

    def stage_q(t):
        for hh in range(2):
            pltpu.matmul_push_rhs(qa_ref[:, pl.ds(hh * bq + t * tq, tq)],
                                  staging_register=0, mxu_index=hh)

    def stage_a(t, c, slot, restage):
        start = chunk_start(t, c)
        ka = jnp.concatenate([k_ref[pl.ds(start, kc), :], caug_ref[pl.ds(start, kc), :]],
                             axis=1)
        for hh in range(2):
            pltpu.matmul_acc_lhs(S_ADDR[slot], ka, mxu_index=hh, load_staged_rhs=0)
        if restage is not None:
            stage_q(restage)

    def stage_b(t, c, slot, masked):
        start = chunk_start(t, c)
        for hh in range(2):
            s = pltpu.matmul_pop(S_ADDR[slot], (kc, tq), F32, mxu_index=hh)
            if masked:
                key = c * kc + lax.broadcasted_iota(jnp.int32, (kc, tq), 0)
                qry = i * bq + t * tq + lax.broadcasted_iota(jnp.int32, (kc, tq), 1)
                s = jnp.where(key <= qry, s, NEG_BIG)
            pt_ref[hh, slot] = jnp.exp2(s).astype(BF16)
            pltpu.matmul_push_rhs(pt_ref[hh, slot], staging_register=1, mxu_index=hh)
            vta = jnp.concatenate(
                [vt_ref[hh * HEAD_DIM:(hh + 1) * HEAD_DIM, pl.ds(start, kc)], ones_rows],
                axis=0)
            pltpu.matmul_acc_lhs(O_ADDR, vta, mxu_index=hh, load_staged_rhs=1)

    stage_q(0)
    stage_a(0, 0, 0, restage=0)
    stage_a(0, 1, 1, restage=0)
    for t in range(n_tiles):
        n_iter = (n_real(t) + S_SLOTS - 1) // S_SLOTS

        def body(g, carry, t=t):
            for s in range(S_SLOTS):
                c = g * S_SLOTS + s
                stage_a(t, c + 2, (s + 2) % S_SLOTS, restage=t)
                stage_b(t, c, s, False)
            return carry

        lax.fori_loop(0, n_iter - 1, body, 0)
        c_last = (n_iter - 1) * S_SLOTS
        following = t + 1 if t + 1 < n_tiles else None
        if following is not None:
            build_q(following)
        stage_a(t, c_last + 2, 2, restage=following)
        stage_b(t, c_last, 0, True)
        if following is not None:
            stage_a(following, 0, 0, restage=following)
        stage_b(t, c_last + 1, 1, True)
        stage_b(t, c_last + 2, 2, True)

        outs = []
        for hh in range(2):
            o = pltpu.matmul_pop(O_ADDR, (HEAD_DIM + ONES_ROWS, tq), F32, mxu_index=hh)
            outs.append(o[:HEAD_DIM, :] / o[HEAD_DIM:HEAD_DIM + 1, :])
        ot_ref[:, t * tq:(t + 1) * tq] = jnp.concatenate(outs, axis=0)
        if following is not None:
            stage_a(following, 1, 1, restage=following)
    o_ref[...] = (ot_ref[...].T * _silu(z_ref[...])).astype(BF16)


def _attention_mxu(qt, k, caug, vt, c, z):
    D, S = qt.shape
    bq = STREAM_BLOCK_Q
    c3 = c.reshape(N_PAIRS, 2, S)
    return pl.pallas_call(
        _attn_mxu_kernel,
        grid=(N_PAIRS, S // bq),
        in_specs=[
            pl.BlockSpec((LANES, bq), lambda p, i: (p, i)),
            pl.BlockSpec((S, LANES), lambda p, i: (0, p)),
            pl.BlockSpec((S, LANES), lambda p, i: (0, 0)),
            pl.BlockSpec((LANES, S), lambda p, i: (p, 0)),
            pl.BlockSpec((1, 2, bq), lambda p, i: (p, 0, i)),
            pl.BlockSpec((bq, LANES), lambda p, i: (i, p)),
        ],
        out_specs=pl.BlockSpec((bq, LANES), lambda p, i: (i, p)),
        out_shape=jax.ShapeDtypeStruct((S, D), BF16),
        scratch_shapes=[pltpu.VMEM((2 * LANES, 2 * bq), BF16), pltpu.VMEM((LANES, bq), F32),
                        pltpu.VMEM((2, S_SLOTS, MXU_TILE, MXU_TILE), BF16)],
        compiler_params=pltpu.CompilerParams(
            dimension_semantics=("arbitrary", "arbitrary"), vmem_limit_bytes=VMEM_LIMIT_BYTES),
        name="fox_attention_mxu",
    )(qt, k, caug, vt, c3, z)


def _out_proj_kernel(x_ref, a_ref, w_ref, o_ref):
    o_ref[...] = x_ref[...] + jnp.dot(a_ref[...], w_ref[...].astype(BF16),
                                      preferred_element_type=F32)


def _out_proj(x, a, w_out):
    S, D = x.shape
    T = OUT_ROW_TILE
    return pl.pallas_call(
        _out_proj_kernel,
        grid=(S // T,),
        in_specs=[
            pl.BlockSpec((T, D), lambda i: (i, 0)),
            pl.BlockSpec((T, D), lambda i: (i, 0)),
            pl.BlockSpec((D, D), lambda i: (0, 0), pipeline_mode=pl.Buffered(1)),
        ],
        out_specs=pl.BlockSpec((T, D), lambda i: (i, 0)),
        out_shape=jax.ShapeDtypeStruct((S, D), F32),
        compiler_params=pltpu.CompilerParams(
            dimension_semantics=("arbitrary",), vmem_limit_bytes=VMEM_LIMIT_BYTES),
        name="attn_out_proj",
    )(x, a, w_out)


def kernel(x, conv_norm_g, conv_w_in, conv_w, conv_w_out, attn_norm_g, attn_w_in,
           attn_b_f, attn_q_norm_g, attn_k_norm_g, attn_w_out):
    B, S, D = x.shape
    assert (B, D) == (1, D_MODEL)
    assert all(S % t == 0 for t in (BLOCK_Q, STREAM_BLOCK_Q, ROW_TILE, OUT_ROW_TILE))
    assert conv_w_in.shape[0] == 1 and attn_w_in.shape[0] == 1
    x2 = x.reshape(S, D)

    x2 = _conv_layer(x2, conv_norm_g[0].reshape(1, D), conv_w_in[0], conv_w[0], conv_w_out[0])

    qt, k, vt, z, c, caug = _attn_proj(
        x2, attn_norm_g[0].reshape(1, D), jnp.swapaxes(attn_w_in, 1, 2),
        attn_b_f[0].reshape(N_HEADS, 1),
        jnp.tile(attn_q_norm_g[0], N_HEADS).reshape(D, 1),
        jnp.tile(attn_k_norm_g[0], N_HEADS).reshape(D, 1))
    logit_bound = (math.sqrt(HEAD_DIM) * jnp.max(jnp.abs(attn_q_norm_g[0]))
                   * jnp.max(jnp.abs(attn_k_norm_g[0])))
    a = lax.cond(logit_bound <= LOGIT_BOUND_MAX,
                 lambda: _attention_mxu(qt, k, caug, vt, c, z),
                 lambda: _attention(qt.T, k, vt.T, c, z))
    out = _out_proj(x2, a, attn_w_out[0])
    return out.reshape(B, S, D)
```
